```python
import jax
import jax.numpy as jnp
from jax import lax
import numpy as np


D_MODEL = 2048
BATCH = 8
SEQ = 4096
DEPTH = 2

GRID_W = 64
CTX_LEN = 256
NA_HEADS = 16
HEAD_DIM = 64
NA_WIDTH = NA_HEADS * HEAD_DIM
FOURIER_GROUPS = 8
FOURIER_GROUP_DIM = 128
FOURIER_WIDTH = FOURIER_GROUPS * FOURIER_GROUP_DIM
MIX_WIDTH = NA_WIDTH + FOURIER_WIDTH
PROJ_WIDTH = 3 * NA_WIDTH + FOURIER_WIDTH
WIN_ROWS_MAX = 8
WIN_COLS = 16
D_FF = 5632
CONV_WIDTH = 3
N_MOD = 6
EPS = 1e-6
ATTN_SCALE = HEAD_DIM ** -0.5

kernel_name = 'hybrid_natten_fnet_convffn_dit'


def rms_norm(x, g):
    xf = x.astype(jnp.float32)
    y = xf * lax.rsqrt(jnp.mean(xf * xf, axis=-1, keepdims=True) + EPS)
    return (y * g.astype(jnp.float32)).astype(x.dtype)


def adaln_mods(cond, w_ada, b_ada):
    m = jax.nn.silu(cond) @ w_ada + b_ada
    return [m[:, None, i * D_MODEL:(i + 1) * D_MODEL] for i in range(N_MOD)]


def modulate(h, shift, scale):
    return h * (1 + scale) + shift


def split_heads(t):
    return t.reshape(t.shape[0], t.shape[1], NA_HEADS, HEAD_DIM)


def context_attention(qc, kc, vc):
    s = jnp.einsum('bqhd,bkhd->bhqk', qc, kc).astype(jnp.float32) * ATTN_SCALE
    p = jax.nn.softmax(s, axis=-1).astype(vc.dtype)
    o = jnp.einsum('bhqk,bkhd->bqhd', p, vc)
    return o.reshape(o.shape[0], o.shape[1], NA_WIDTH)


def neighbourhood_attention(q, k, v, kc, vc, rpb):
    b, n = q.shape[0], q.shape[1]
    rows = n // GRID_W
    kr = min(WIN_ROWS_MAX, rows)
    n_loc = kr * WIN_COLS
    row_start = np.clip(np.arange(rows) - kr // 2, 0, rows - kr)
    col_start = np.clip(np.arange(GRID_W) - WIN_COLS // 2, 0, GRID_W - WIN_COLS)
    col_idx = col_start[:, None] + np.arange(WIN_COLS)[None, :]
    dr_idx = row_start[:, None] + np.arange(kr)[None, :] - np.arange(rows)[:, None] + WIN_ROWS_MAX - 1
    dc_idx = col_idx - np.arange(GRID_W)[:, None] + WIN_COLS - 1
    qg = q.reshape(b, rows, GRID_W, NA_HEADS, HEAD_DIM)
    kg = k.reshape(b, rows, GRID_W, NA_HEADS, HEAD_DIM)
    vg = v.reshape(b, rows, GRID_W, NA_HEADS, HEAD_DIM)

    def row_block(args):
        q_r, r0, dr = args
        k_win = lax.dynamic_slice_in_dim(kg, r0, kr, axis=1)[:, :, col_idx]
        v_win = lax.dynamic_slice_in_dim(vg, r0, kr, axis=1)[:, :, col_idx]
        bias = rpb[:, dr[None, :, None], dc_idx[:, None, :]].astype(jnp.float32)
        s_loc = jnp.einsum('bchd,brcjhd->bhcrj', q_r, k_win).astype(jnp.float32) * ATTN_SCALE + bias[None]
        s_ctx = jnp.einsum('bchd,bnhd->bhcn', q_r, kc).astype(jnp.float32) * ATTN_SCALE
        s = jnp.concatenate([s_loc.reshape(b, NA_HEADS, GRID_W, n_loc), s_ctx], axis=-1)
        p = jax.nn.softmax(s, axis=-1).astype(v.dtype)
        p_loc = p[..., :n_loc].reshape(b, NA_HEADS, GRID_W, kr, WIN_COLS)
        p_ctx = p[..., n_loc:]
        return (jnp.einsum('bhcrj,brcjhd->bchd', p_loc, v_win)
                + jnp.einsum('bhcn,bnhd->bchd', p_ctx, vc))

    out = lax.map(row_block, (jnp.moveaxis(qg, 1, 0),
                              jnp.asarray(row_start, dtype=jnp.int32),
                              jnp.asarray(dr_idx, dtype=jnp.int32)))
    return jnp.moveaxis(out, 0, 1).reshape(b, n, NA_WIDTH)


def fourier_mix(f, w_four):
    b, n = f.shape[0], f.shape[1]
    fg = f.reshape(b, n, FOURIER_GROUPS, FOURIER_GROUP_DIM).astype(jnp.float32)
    spec = jnp.fft.fftn(fg, axes=(1, 3), norm='ortho').real.astype(f.dtype)
    return jnp.einsum('bngc,gce->bnge', spec, w_four).reshape(b, n, FOURIER_WIDTH)


def conv_ffn(h, w_up, conv_w, conv_b, w_down):
    u = h @ w_up
    up = jnp.pad(u, ((0, 0), (1, 1), (0, 0)))
    u = up[:, :-2] * conv_w[0] + up[:, 1:-1] * conv_w[1] + up[:, 2:] * conv_w[2] + conv_b
    a, g = u[..., :D_FF], u[..., D_FF:]
    return (jax.nn.silu(g) * a) @ w_down


def setup_inputs(seed: int = 0) -> dict:
    key = jax.random.key(seed)
    ks = jax.random.split(key, 20)
    nrm = jax.random.normal
    f32 = jnp.float32
    return {
        'x': nrm(ks[0], (BATCH, SEQ, D_MODEL), f32),
        'c': nrm(ks[1], (BATCH, D_MODEL), f32),
        'ctx': nrm(ks[2], (BATCH, CTX_LEN, D_MODEL), f32),
        'c_ctx': nrm(ks[3], (D_MODEL,), f32),
        'w_ada': nrm(ks[4], (DEPTH, D_MODEL, N_MOD * D_MODEL), f32) * (0.5 * D_MODEL ** -0.5),
        'b_ada': nrm(ks[5], (DEPTH, N_MOD * D_MODEL), f32) * 0.01,
        'g_pre_mix': 1.0 + 0.05 * nrm(ks[6], (DEPTH, D_MODEL), f32),
        'w_in': nrm(ks[7], (DEPTH, D_MODEL, PROJ_WIDTH), f32) * D_MODEL ** -0.5,
        'rpb': nrm(ks[8], (DEPTH, NA_HEADS, 2 * WIN_ROWS_MAX - 1, 2 * WIN_COLS - 1), f32) * 0.1,
        'w_four': nrm(ks[9], (DEPTH, FOURIER_GROUPS, FOURIER_GROUP_DIM, FOURIER_GROUP_DIM), f32) * FOURIER_GROUP_DIM ** -0.5,
        'w_out': nrm(ks[10], (DEPTH, MIX_WIDTH, D_MODEL), f32) * MIX_WIDTH ** -0.5,
        'g_post_mix': 1.0 + 0.05 * nrm(ks[11], (DEPTH, D_MODEL), f32),
        'g_pre_ffn': 1.0 + 0.05 * nrm(ks[12], (DEPTH, D_MODEL), f32),
        'w_up': nrm(ks[13], (DEPTH, D_MODEL, 2 * D_FF), f32) * D_MODEL ** -0.5,
        'conv_w': nrm(ks[14], (DEPTH, CONV_WIDTH, 2 * D_FF), f32) * 0.5,
        'conv_b': nrm(ks[15], (DEPTH, 2 * D_FF), f32) * 0.01,
        'w_down': nrm(ks[16], (DEPTH, D_FF, D_MODEL), f32) * D_FF ** -0.5,
        'g_post_ffn': 1.0 + 0.05 * nrm(ks[17], (DEPTH, D_MODEL), f32),
    }


def reference(x, c, ctx, c_ctx, w_ada, b_ada, g_pre_mix, w_in, rpb, w_four, w_out,
              g_post_mix, g_pre_ffn, w_up, conv_w, conv_b, w_down, g_post_ffn):
    cx = ctx
    for l in range(DEPTH):
        last = l == DEPTH - 1
        sh_a, sc_a, gt_a, sh_f, sc_f, gt_f = adaln_mods(c, w_ada[l], b_ada[l])
        csh_a, csc_a, cgt_a, csh_f, csc_f, cgt_f = adaln_mods(c_ctx[None, :], w_ada[l], b_ada[l])

        h = modulate(rms_norm(x, g_pre_mix[l]), sh_a, sc_a)
        hc = modulate(rms_norm(cx, g_pre_mix[l]), csh_a, csc_a)
        proj = h @ w_in[l]
        q = split_heads(proj[..., :NA_WIDTH])
        k = split_heads(proj[..., NA_WIDTH:2 * NA_WIDTH])
        v = split_heads(proj[..., 2 * NA_WIDTH:3 * NA_WIDTH])
        f = proj[..., 3 * NA_WIDTH:]
        if last:
            kv_c = hc @ w_in[l][:, NA_WIDTH:3 * NA_WIDTH]
            kc = split_heads(kv_c[..., :NA_WIDTH])
            vc = split_heads(kv_c[..., NA_WIDTH:])
        else:
            proj_c = hc @ w_in[l]
            qc = split_heads(proj_c[..., :NA_WIDTH])
            kc = split_heads(proj_c[..., NA_WIDTH:2 * NA_WIDTH])
            vc = split_heads(proj_c[..., 2 * NA_WIDTH:3 * NA_WIDTH])
            fc = proj_c[..., 3 * NA_WIDTH:]

        attn = neighbourhood_attention(q, k, v, kc, vc, rpb[l])
        mix = jnp.concatenate([attn, fourier_mix(f, w_four[l])], axis=-1) @ w_out[l]
        x = x + gt_a * rms_norm(mix, g_post_mix[l])

        h = modulate(rms_norm(x, g_pre_ffn[l]), sh_f, sc_f)
        x = x + gt_f * rms_norm(conv_ffn(h, w_up[l], conv_w[l], conv_b[l], w_down[l]), g_post_ffn[l])

        if not last:
            mix_c = jnp.concatenate([context_attention(qc, kc, vc), fourier_mix(fc, w_four[l])], axis=-1) @ w_out[l]
            cx = cx + cgt_a * rms_norm(mix_c, g_post_mix[l])
            hc = modulate(rms_norm(cx, g_pre_ffn[l]), csh_f, csc_f)
            cx = cx + cgt_f * rms_norm(conv_ffn(hc, w_up[l], conv_w[l], conv_b[l], w_down[l]), g_post_ffn[l])
    return x
```

```python
import functools

import jax
import jax.numpy as jnp
import numpy as np
from jax import lax
from jax.experimental import pallas as pl
from jax.experimental.pallas import tpu as pltpu

D_MODEL = 2048
DEPTH = 2
GRID_W = 64
CTX_LEN = 256
NA_HEADS = 16
HEAD_DIM = 64
NA_WIDTH = NA_HEADS * HEAD_DIM
FOURIER_GROUPS = 8
FOURIER_GROUP_DIM = 128
FOURIER_WIDTH = FOURIER_GROUPS * FOURIER_GROUP_DIM
PROJ_WIDTH = 3 * NA_WIDTH + FOURIER_WIDTH
WIN_ROWS = 8
WIN_COLS = 16
D_FF = 5632
N_MOD = 6
EPS = 1e-6
ATTN_SCALE = HEAD_DIM ** -0.5
MOD_ROWS = 16
CTX_MOD_ROW = 8

LANES = 128
BF16_SUBLANES = 16
HEAD_PAIR = LANES // HEAD_DIM
N_PAIRS = NA_HEADS // HEAD_PAIR
Q_ROWS = 4
K_ROWS = Q_ROWS + WIN_ROWS - 1
TQ = Q_ROWS * GRID_W
NK = K_ROWS * GRID_W
NEG = -1e30
VMEM_LIMIT = 56 * 1024 * 1024

F32 = jnp.float32
BF16 = jnp.bfloat16


def _params(*sem):
    return pltpu.CompilerParams(dimension_semantics=sem, vmem_limit_bytes=VMEM_LIMIT)


def _dot(a, b):
    return jnp.dot(a, b, preferred_element_type=F32)


def _dot_nt(a, b):
    return lax.dot_general(a, b, (((1,), (1,)), ((), ())), preferred_element_type=F32)


def _rms(x, g):
    ms = jnp.mean(x * x, axis=-1, keepdims=True)
    return x * lax.rsqrt(ms + EPS) * g


def _norm_mod(x, g, shift, scale):
    return _rms(x, g) * (1.0 + scale) + shift


def _mods_kernel(cond_ref, w_ref, b_ref, o_ref):
    c = cond_ref[...]
    s = (c * jax.nn.sigmoid(c)).astype(BF16)
    o_ref[...] = _dot(s, w_ref[...].astype(BF16)) + b_ref[...]


def _mods(cond, w_ada, b_ada, tn=1024):
    depth, d, n = w_ada.shape
    return pl.pallas_call(
        _mods_kernel,
        grid=(depth, n // tn),
        in_specs=[
            pl.BlockSpec((MOD_ROWS, d), lambda l, j: (0, 0)),
            pl.BlockSpec((None, d, tn), lambda l, j: (l, 0, j)),
            pl.BlockSpec((None, 1, tn), lambda l, j: (l, 0, j)),
        ],
        out_specs=pl.BlockSpec((None, MOD_ROWS, tn), lambda l, j: (l, 0, j)),
        out_shape=jax.ShapeDtypeStruct((depth, MOD_ROWS, n), F32),
        compiler_params=_params("parallel", "parallel"),
        name="adaln_mods",
    )(cond, w_ada, b_ada.reshape(depth, 1, n))


def _mod_spec(which, row_fn):
    return pl.BlockSpec((None, None, 1, D_MODEL),
                        lambda i, *_: (row_fn(i), which, 0, 0))


def _inproj_kernel(x_ref, g_ref, sh_ref, sc_ref, w_ref, o_ref, h_ref):
    @pl.when(pl.program_id(1) == 0)
    def _():
        h_ref[...] = _norm_mod(x_ref[...], g_ref[...], sh_ref[...], sc_ref[...]).astype(BF16)

    o_ref[...] = _dot(h_ref[...], w_ref[...]).astype(o_ref.dtype)


def _inproj(x, g, mods, row_fn, w, tm, tn):
    t, d = x.shape
    n = w.shape[1]
    return pl.pallas_call(
        _inproj_kernel,
        grid=(t // tm, n // tn),
        in_specs=[
            pl.BlockSpec((tm, d), lambda i, j: (i, 0)),
            pl.BlockSpec((1, d), lambda i, j: (0, 0)),
            _mod_spec(0, row_fn),
            _mod_spec(1, row_fn),
            pl.BlockSpec((d, tn), lambda i, j: (0, j)),
        ],
        out_specs=pl.BlockSpec((tm, tn), lambda i, j: (i, j)),
        out_shape=jax.ShapeDtypeStruct((t, n), BF16),
        scratch_shapes=[pltpu.VMEM((tm, d), BF16)],
        compiler_params=_params("parallel", "arbitrary"),
        name="norm_inproj",
    )(x, g.reshape(1, d), mods, mods, w)


def _softmax_pv(parts):
    m = functools.reduce(jnp.maximum, [s.max(axis=-1, keepdims=True) for s, _ in parts])
    num = 0.0
    den = 0.0
    for s, v in parts:
        p = jnp.exp(s - m)
        den = den + p.sum(axis=-1, keepdims=True)
        num = num + _dot(p.astype(BF16), v)
    return num / den


def _head_masks():
    lane = lax.broadcasted_iota(jnp.int32, (1, LANES), 1)
    first = lane < HEAD_DIM
    return first, [first, jnp.logical_not(first)]


def _attn_kernel(q_ref, k_ref, v_ref, kc_ref, vc_ref, tab_ref, o_ref, *, n_tiles, rows):
    first, sels = _head_masks()
    kc = kc_ref[...]
    vc = vc_ref[...]

    def tile(t, carry):
        q0 = pl.multiple_of(t * TQ, TQ)
        k0 = pl.multiple_of(jnp.clip(t * Q_ROWS - WIN_ROWS // 2, 0, rows - K_ROWS) * GRID_W, GRID_W)
        pat = jnp.where(t == 0, 0, jnp.where(t == n_tiles - 1, 2, 1))
        q = q_ref[pl.ds(q0, TQ), :] * ATTN_SCALE
        kw = k_ref[pl.ds(k0, NK), :]
        vw = v_ref[pl.ds(k0, NK), :]
        outs = []
        for hh in range(HEAD_PAIR):
            qm = jnp.where(sels[hh], q, jnp.zeros_like(q))
            s_loc = _dot_nt(qm, kw) + tab_ref[pat, hh]
            s_ctx = _dot_nt(qm, kc)
            outs.append(_softmax_pv([(s_loc, vw), (s_ctx, vc)]))
        o_ref[pl.ds(q0, TQ), :] = jnp.where(first, outs[0], outs[1]).astype(o_ref.dtype)
        return carry

    lax.fori_loop(0, n_tiles, tile, 0)


def _attn(proj, cproj, table, kc_blk, vc_blk):
    b, n, _ = proj.shape
    rows = n // GRID_W
    n_tiles = rows // Q_ROWS
    lat = lambda off: pl.BlockSpec((None, n, LANES), lambda p, i: (i, 0, off + p))
    cx = lambda off: pl.BlockSpec((None, CTX_LEN, LANES), lambda p, i: (i, 0, off + p))
    return pl.pallas_call(
        functools.partial(_attn_kernel, n_tiles=n_tiles, rows=rows),
        grid=(N_PAIRS, b),
        in_specs=[
            lat(0), lat(N_PAIRS), lat(2 * N_PAIRS), cx(kc_blk), cx(vc_blk),
            pl.BlockSpec((3, HEAD_PAIR, TQ, NK), lambda p, i: (0, p, 0, 0)),
        ],
        out_specs=pl.BlockSpec((None, n, LANES), lambda p, i: (i, 0, p)),
        out_shape=jax.ShapeDtypeStruct((b, n, NA_WIDTH), BF16),
        compiler_params=_params("parallel", "parallel"),
        name="nbr_attn",
    )(proj, proj, proj, cproj, cproj, table)


def _ctx_attn_kernel(q_ref, k_ref, v_ref, o_ref):
    first, sels = _head_masks()
    q = q_ref[...] * ATTN_SCALE
    k = k_ref[...]
    v = v_ref[...]
    outs = []
    for hh in range(HEAD_PAIR):
        qm = jnp.where(sels[hh], q, jnp.zeros_like(q))
        outs.append(_softmax_pv([(_dot_nt(qm, k), v)]))
    o_ref[...] = jnp.where(first, outs[0], outs[1]).astype(o_ref.dtype)


def _ctx_attn(cproj):
    b, n, _ = cproj.shape
    spec = lambda off: pl.BlockSpec((None, n, LANES), lambda i, p: (i, 0, off + p))
    return pl.pallas_call(
        _ctx_attn_kernel,
        grid=(b, N_PAIRS),
        in_specs=[spec(0), spec(N_PAIRS), spec(2 * N_PAIRS)],
        out_specs=spec(0),
        out_shape=jax.ShapeDtypeStruct((b, n, NA_WIDTH), BF16),
        compiler_params=_params("parallel", "parallel"),
        name="ctx_attn",
    )(cproj, cproj, cproj)


def _attn_bias_table(rpb):
    rows = GRID_W
    qc = np.arange(GRID_W)[:, None]
    kc = np.arange(GRID_W)[None, :]
    cs = np.clip(qc - WIN_COLS // 2, 0, GRID_W - WIN_COLS)
    col_ok = (kc >= cs) & (kc < cs + WIN_COLS)
    dc = kc - qc + WIN_COLS - 1
    onehot = np.zeros((2 * WIN_COLS - 1, GRID_W, GRID_W), np.float32)
    qi, ki = np.nonzero(col_ok)
    onehot[dc[qi, ki], qi, ki] = 1.0
    t1 = jnp.einsum('hrd,dqk->hrqk', rpb, jnp.asarray(onehot), precision=lax.Precision.HIGHEST)
    t1 = jnp.where(jnp.asarray(col_ok)[None, None], t1, NEG)
    neg_slab = jnp.full((NA_HEADS, 1, GRID_W, GRID_W), NEG, F32)
    t1 = jnp.concatenate([t1, neg_slab], axis=1)
    n_tiles = rows // Q_ROWS
    idx = np.full((3, Q_ROWS, K_ROWS), 2 * WIN_ROWS - 1, np.int32)
    for p, t in enumerate((0, 1, n_tiles - 1)):
        k_start = int(np.clip(t * Q_ROWS - WIN_ROWS // 2, 0, rows - K_ROWS))
        for qr in range(Q_ROWS):
            r = t * Q_ROWS + qr
            rs = int(np.clip(r - WIN_ROWS // 2, 0, rows - WIN_ROWS))
            for kr in range(K_ROWS):
                krow = k_start + kr
                if rs <= krow < rs + WIN_ROWS:
                    idx[p, qr, kr] = krow - r + WIN_ROWS - 1
    g = jnp.take(t1, jnp.asarray(idx.reshape(-1)), axis=1)
    g = g.reshape(NA_HEADS, 3, Q_ROWS, K_ROWS, GRID_W, GRID_W)
    return g.transpose(1, 0, 2, 4, 3, 5).reshape(3, NA_HEADS, TQ, NK)


def _split_bf16(a):
    hi = a.astype(BF16)
    lo = (a - hi.astype(F32)).astype(BF16)
    return hi, lo


def _dot3(a, b):
    ah, al = _split_bf16(a)
    bh, bl = _split_bf16(b)
    return _dot(ah, bh) + (_dot(ah, bl) + _dot(al, bh))


def _four_w_kernel(cc_ref, sc_ref, w_ref, o_ref):
    w = w_ref[...]
    o_ref[:, :FOURIER_GROUP_DIM] = _dot3(cc_ref[...], w).astype(BF16)
    o_ref[:, FOURIER_GROUP_DIM:] = _dot3(sc_ref[...], w).astype(BF16)


def _four_w(cc, sc, w_four):
    depth, g, c, _ = w_four.shape
    tab = pl.BlockSpec((c, c), lambda l, i: (0, 0))
    return pl.pallas_call(
        _four_w_kernel,
        grid=(depth, g),
        in_specs=[tab, tab, pl.BlockSpec((None, None, c, c), lambda l, i: (l, i, 0, 0))],
        out_specs=pl.BlockSpec((None, None, c, 2 * c), lambda l, i: (l, i, 0, 0)),
        out_shape=jax.ShapeDtypeStruct((depth, g, c, 2 * c), BF16),
        compiler_params=_params("parallel", "parallel"),
        name="fourier_weights",
    )(cc, sc, w_four)


def _four_chan_kernel(f_ref, m_ref, yc_ref, ys_ref):
    c = FOURIER_GROUP_DIM
    for g in range(FOURIER_GROUPS):
        y = _dot(f_ref[:, g * c:(g + 1) * c], m_ref[g])
        yc_ref[:, g * c:(g + 1) * c] = y[:, :c].astype(BF16)
        ys_ref[:, g * c:(g + 1) * c] = y[:, c:].astype(BF16)


def _four_chan(proj, mw, tm):
    t = proj.shape[0]
    f_blk = proj.shape[1] // FOURIER_WIDTH - 1
    out = pl.BlockSpec((tm, FOURIER_WIDTH), lambda i: (i, 0))
    return pl.pallas_call(
        _four_chan_kernel,
        grid=(t // tm,),
        in_specs=[
            pl.BlockSpec((tm, FOURIER_WIDTH), lambda i: (i, f_blk)),
            pl.BlockSpec(mw.shape, lambda i: (0, 0, 0)),
        ],
        out_specs=[out, out],
        out_shape=[jax.ShapeDtypeStruct((t, FOURIER_WIDTH), BF16)] * 2,
        compiler_params=_params("parallel"),
        name="fourier_chan",
    )(proj, mw)


def _four_pos_kernel(cn_ref, sn_ref, yc_ref, ys_ref, o_ref):
    o_ref[...] = (_dot(cn_ref[...], yc_ref[...]) + _dot(sn_ref[...], ys_ref[...])).astype(o_ref.dtype)


def _four_pos(cn, sn, yc, ys, tm, tn):
    b, n, w = yc.shape
    mat = pl.BlockSpec((tm, n), lambda i, bb, j: (i, 0))
    y = pl.BlockSpec((None, n, tn), lambda i, bb, j: (bb, 0, j))
    return pl.pallas_call(
        _four_pos_kernel,
        grid=(n // tm, b, w // tn),
        in_specs=[mat, mat, y, y],
        out_specs=pl.BlockSpec((None, tm, tn), lambda i, bb, j: (bb, i, j)),
        out_shape=jax.ShapeDtypeStruct((b, n, w), BF16),
        compiler_params=_params("parallel", "parallel", "parallel"),
        name="fourier_pos",
    )(cn, sn, yc, ys)


def _dft_tables(n, split):
    j = jnp.arange(n, dtype=jnp.int32)[:, None]
    hi = jnp.arange(n // split, dtype=jnp.int32)[None, :] * split
    lo = jnp.arange(split, dtype=jnp.int32)[None, :]
    ang = lambda k: (2.0 * np.pi / n) * ((j * k) % n).astype(F32)
    ch, sh, cl, sl = jnp.cos(ang(hi)), jnp.sin(ang(hi)), jnp.cos(ang(lo)), jnp.sin(ang(lo))
    c = ch[:, :, None] * cl[:, None, :] - sh[:, :, None] * sl[:, None, :]
    s = sh[:, :, None] * cl[:, None, :] + ch[:, :, None] * sl[:, None, :]
    return c.reshape(n, n).astype(BF16), s.reshape(n, n).astype(BF16)


def _outproj_kernel(a_ref, f_ref, w_ref, x_ref, gpost_ref, gate_ref, gpre_ref, sh_ref, sc_ref,
                    xo_ref, h_ref):
    mix = _dot(a_ref[...], w_ref[:NA_WIDTH, :]) + _dot(f_ref[...], w_ref[NA_WIDTH:, :])
    x = x_ref[...] + gate_ref[...] * _rms(mix, gpost_ref[...])
    xo_ref[...] = x
    h_ref[...] = _norm_mod(x, gpre_ref[...], sh_ref[...], sc_ref[...]).astype(BF16)


def _outproj(attn, four, w, x, g_post, g_pre, mods, row_fn, tm):
    t, d = x.shape
    vec = pl.BlockSpec((1, d), lambda i: (0, 0))
    row = pl.BlockSpec((tm, d), lambda i: (i, 0))
    half = pl.BlockSpec((tm, NA_WIDTH), lambda i: (i, 0))
    return pl.pallas_call(
        _outproj_kernel,
        grid=(t // tm,),
        in_specs=[half, half, pl.BlockSpec(w.shape, lambda i: (0, 0)), row, vec,
                  _mod_spec(2, row_fn), vec, _mod_spec(3, row_fn), _mod_spec(4, row_fn)],
        out_specs=[row, row],
        out_shape=[jax.ShapeDtypeStruct((t, d), F32), jax.ShapeDtypeStruct((t, d), BF16)],
        compiler_params=_params("parallel"),
        name="outproj_residual",
    )(attn, four, w, x, g_post.reshape(1, d), mods, g_pre.reshape(1, d), mods, mods)


def _ffn_kernel(h_ref, hp_ref, hn_ref, wa_ref, wg_ref, cwa_ref, cwg_ref, cba_ref, cbg_ref, wd_ref,
                x_ref, gate_ref, gpost_ref, o_ref, hext_ref, acc_ref, *, seq_len, tm):
    i = pl.program_id(0)
    j = pl.program_id(1)
    halo = BF16_SUBLANES

    @pl.when(j == 0)
    def _():
        at_start = (i * tm) % seq_len == 0
        at_end = ((i + 1) * tm) % seq_len == 0
        hp = hp_ref[...]
        hn = hn_ref[...]
        hext_ref[0:halo, :] = jnp.where(at_start, jnp.zeros_like(hp), hp)
        hext_ref[halo:halo + tm, :] = h_ref[...]
        hext_ref[halo + tm:, :] = jnp.where(at_end, jnp.zeros_like(hn), hn)
        acc_ref[...] = jnp.zeros_like(acc_ref)

    hext = hext_ref[...]

    def conv(w_ref, cw_ref, cb_ref):
        u = _dot(hext, w_ref[...])
        cw = cw_ref[...]
        return (u[halo - 1:halo - 1 + tm] * cw[0:1] + u[halo:halo + tm] * cw[1:2]
                + u[halo + 1:halo + 1 + tm] * cw[2:3] + cb_ref[...])

    a = conv(wa_ref, cwa_ref, cba_ref)
    g = conv(wg_ref, cwg_ref, cbg_ref)
    act = (g * jax.nn.sigmoid(g) * a).astype(BF16)
    acc_ref[...] += _dot(act, wd_ref[...])

    @pl.when(j == pl.num_programs(1) - 1)
    def _():
        o_ref[...] = x_ref[...] + gate_ref[...] * _rms(acc_ref[...], gpost_ref[...])


def _ffn(h, w_up, conv_w, conv_b, w_down, x, g_post, mods, row_fn, seq_len, tm, tf):
    t, d = x.shape
    ff = w_down.shape[0]
    nj = ff // tf
    halo = BF16_SUBLANES
    hb = tm // halo
    n_hblk = t // halo
    vec = pl.BlockSpec((1, d), lambda i, j: (0, 0))
    row = pl.BlockSpec((tm, d), lambda i, j: (i, 0))
    cb = conv_b.reshape(1, 2 * ff)
    return pl.pallas_call(
        functools.partial(_ffn_kernel, seq_len=seq_len, tm=tm),
        grid=(t // tm, nj),
        in_specs=[
            row,
            pl.BlockSpec((halo, d), lambda i, j: (jnp.maximum(i * hb - 1, 0), 0)),
            pl.BlockSpec((halo, d), lambda i, j: (jnp.minimum((i + 1) * hb, n_hblk - 1), 0)),
            pl.BlockSpec((d, tf), lambda i, j: (0, j)),
            pl.BlockSpec((d, tf), lambda i, j: (0, nj + j)),
            pl.BlockSpec((3, tf), lambda i, j: (0, j)),
            pl.BlockSpec((3, tf), lambda i, j: (0, nj + j)),
            pl.BlockSpec((1, tf), lambda i, j: (0, j)),
            pl.BlockSpec((1, tf), lambda i, j: (0, nj + j)),
            pl.BlockSpec((tf, d), lambda i, j: (j, 0)),
            row,
            _mod_spec(5, row_fn),
            vec,
        ],
        out_specs=row,
        out_shape=jax.ShapeDtypeStruct((t, d), F32),
        scratch_shapes=[pltpu.VMEM((tm + 2 * halo, d), BF16), pltpu.VMEM((tm, d), F32)],
        compiler_params=_params("parallel", "arbitrary"),
        name="conv_ffn",
    )(h, h, h, w_up, w_up, conv_w, conv_w, cb, cb, w_down, x, mods, g_post.reshape(1, d))


def kernel(x, c, ctx, c_ctx, w_ada, b_ada, g_pre_mix, w_in, rpb, w_four, w_out, g_post_mix,
           g_pre_ffn, w_up, conv_w, conv_b, w_down, g_post_ffn):
    b, n, d = x.shape
    n_ctx = ctx.shape[1]
    assert (d, n, n_ctx, b) == (D_MODEL, GRID_W * GRID_W, CTX_LEN, CTX_MOD_ROW)
    t_lat, t_ctx = b * n, b * n_ctx
    tm_lat, tm_ctx = 512, 512
    lat_row = lambda i: (i * tm_lat) // n
    ctx_row = lambda i: CTX_MOD_ROW

    cond = jnp.zeros((MOD_ROWS, d), F32).at[:b].set(c).at[CTX_MOD_ROW].set(c_ctx)
    mods_all = _mods(cond, w_ada, b_ada)

    scale = (n * FOURIER_GROUP_DIM) ** -0.5
    scale_ctx = (n_ctx * FOURIER_GROUP_DIM) ** -0.5
    kk = np.outer(np.arange(FOURIER_GROUP_DIM), np.arange(FOURIER_GROUP_DIM)) % FOURIER_GROUP_DIM
    ang = 2.0 * np.pi * kk / FOURIER_GROUP_DIM
    cc, sc = np.cos(ang), -np.sin(ang)
    mw_lat = _four_w(jnp.asarray(cc * scale, F32), jnp.asarray(sc * scale, F32), w_four)
    cn_lat, sn_lat = _dft_tables(n, GRID_W)

    xt = x.reshape(t_lat, d)
    ct = ctx.reshape(t_ctx, d)
    for l in range(DEPTH):
        last = l == DEPTH - 1
        mods = mods_all[l].reshape(MOD_ROWS, N_MOD, 1, d)
        w_in_l = w_in[l].astype(BF16)
        w_out_l = w_out[l].astype(BF16)
        w_up_l = w_up[l].astype(BF16)
        w_down_l = w_down[l].astype(BF16)

        proj = _inproj(xt, g_pre_mix[l], mods, lat_row, w_in_l, tm_lat, 1024)
        if last:
            cproj = _inproj(ct, g_pre_mix[l], mods, ctx_row, w_in_l[:, NA_WIDTH:3 * NA_WIDTH], tm_ctx, 1024)
            kc_blk, vc_blk = 0, N_PAIRS
        else:
            cproj = _inproj(ct, g_pre_mix[l], mods, ctx_row, w_in_l, tm_ctx, 1024)
            kc_blk, vc_blk = N_PAIRS, 2 * N_PAIRS
        cproj3 = cproj.reshape(b, n_ctx, cproj.shape[1])

        attn = _attn(proj.reshape(b, n, PROJ_WIDTH), cproj3, _attn_bias_table(rpb[l]), kc_blk, vc_blk)
        yc, ys = _four_chan(proj, mw_lat[l], 1024)
        four = _four_pos(cn_lat, sn_lat, yc.reshape(b, n, FOURIER_WIDTH), ys.reshape(b, n, FOURIER_WIDTH),
                         512, 512)
        xt, h2 = _outproj(attn.reshape(t_lat, NA_WIDTH), four.reshape(t_lat, FOURIER_WIDTH), w_out_l, xt,
                          g_post_mix[l], g_pre_ffn[l], mods, lat_row, tm_lat)
        xt = _ffn(h2, w_up_l, conv_w[l], conv_b[l], w_down_l, xt, g_post_ffn[l], mods, lat_row, n,
                  tm_lat, 512)

        if not last:
            mw_ctx = _four_w(jnp.asarray(cc * scale_ctx, F32), jnp.asarray(sc * scale_ctx, F32),
                             w_four[l:l + 1])[0]
            cn_ctx, sn_ctx = _dft_tables(n_ctx, 16)
            attn_c = _ctx_attn(cproj3)
            yc, ys = _four_chan(cproj, mw_ctx, 512)
            four_c = _four_pos(cn_ctx, sn_ctx, yc.reshape(b, n_ctx, FOURIER_WIDTH),
                               ys.reshape(b, n_ctx, FOURIER_WIDTH), n_ctx, 512)
            ct, hc2 = _outproj(attn_c.reshape(t_ctx, NA_WIDTH), four_c.reshape(t_ctx, FOURIER_WIDTH),
                               w_out_l, ct, g_post_mix[l], g_pre_ffn[l], mods, ctx_row, tm_ctx)
            ct = _ffn(hc2, w_up_l, conv_w[l], conv_b[l], w_down_l, ct, g_post_ffn[l], mods, ctx_row,
                      n_ctx, tm_ctx, 512)
    return xt.reshape(b, n, d)
```

```python
import functools

import jax
import jax.numpy as jnp
import numpy as np
from jax import lax
from jax.experimental import pallas as pl
from jax.experimental.pallas import tpu as pltpu

D_MODEL = 2048
DEPTH = 2
GRID_W = 64
CTX_LEN = 256
NA_HEADS = 16
HEAD_DIM = 64
NA_WIDTH = NA_HEADS * HEAD_DIM
FOURIER_GROUPS = 8
FOURIER_GROUP_DIM = 128
FOURIER_WIDTH = FOURIER_GROUPS * FOURIER_GROUP_DIM
PROJ_WIDTH = 3 * NA_WIDTH + FOURIER_WIDTH
WIN_ROWS = 8
WIN_COLS = 16
D_FF = 5632
N_MOD = 6
EPS = 1e-6
ATTN_SCALE = HEAD_DIM ** -0.5
MOD_ROWS = 16
CTX_MOD_ROW = 8

LANES = 128
CONV_ROWS = 64
F32_SUBLANES = 8
BF16_SUBLANES = 16
HEAD_PAIR = LANES // HEAD_DIM
N_PAIRS = NA_HEADS // HEAD_PAIR
Q_ROWS = 4
K_ROWS = Q_ROWS + WIN_ROWS - 1
TQ = Q_ROWS * GRID_W
NK = K_ROWS * GRID_W
NEG = -1e30
VMEM_LIMIT = 56 * 1024 * 1024

F32 = jnp.float32
BF16 = jnp.bfloat16


def _params(*sem, flags=None):
    return pltpu.CompilerParams(dimension_semantics=sem, vmem_limit_bytes=VMEM_LIMIT, flags=flags)


def _dot(a, b):
    return jnp.dot(a, b, preferred_element_type=F32)


def _dot_nt(a, b):
    return lax.dot_general(a, b, (((1,), (1,)), ((), ())), preferred_element_type=F32)


def _rms(x, g):
    ms = jnp.mean(x * x, axis=-1, keepdims=True)
    return x * lax.rsqrt(ms + EPS) * g


def _norm_mod(x, g, shift, scale):
    return _rms(x, g) * (1.0 + scale) + shift


def _mods_kernel(cond_ref, w_ref, b_ref, o_ref):
    c = cond_ref[...]
    s = (c * jax.nn.sigmoid(c)).astype(BF16)
    o_ref[...] = _dot(s, w_ref[...].astype(BF16)) + b_ref[...]


def _mods(cond, w_ada, b_ada, tn=1024):
    depth, d, n = w_ada.shape
    return pl.pallas_call(
        _mods_kernel,
        grid=(depth, n // tn),
        in_specs=[
            pl.BlockSpec((MOD_ROWS, d), lambda l, j: (0, 0)),
            pl.BlockSpec((None, d, tn), lambda l, j: (l, 0, j)),
            pl.BlockSpec((None, 1, tn), lambda l, j: (l, 0, j)),
        ],
        out_specs=pl.BlockSpec((None, MOD_ROWS, tn), lambda l, j: (l, 0, j)),
        out_shape=jax.ShapeDtypeStruct((depth, MOD_ROWS, n), F32),
        compiler_params=_params("parallel", "parallel"),
        name="adaln_mods",
    )(cond, w_ada, b_ada.reshape(depth, 1, n))


def _mod_spec(which, row_fn):
    return pl.BlockSpec((None, None, 1, D_MODEL),
                        lambda i, *_: (row_fn(i), which, 0, 0))


def _inproj_kernel(x_ref, g_ref, sh_ref, sc_ref, w_ref, o_ref, h_ref):
    @pl.when(pl.program_id(1) == 0)
    def _():
        h_ref[...] = _norm_mod(x_ref[...], g_ref[...], sh_ref[...], sc_ref[...]).astype(BF16)

    o_ref[...] = _dot(h_ref[...], w_ref[...]).astype(o_ref.dtype)


def _inproj(x, g, mods, row_fn, w, tm, tn):
    t, d = x.shape
    n = w.shape[1]
    return pl.pallas_call(
        _inproj_kernel,
        grid=(t // tm, n // tn),
        in_specs=[
            pl.BlockSpec((tm, d), lambda i, j: (i, 0)),
            pl.BlockSpec((1, d), lambda i, j: (0, 0)),
            _mod_spec(0, row_fn),
            _mod_spec(1, row_fn),
            pl.BlockSpec((d, tn), lambda i, j: (0, j)),
        ],
        out_specs=pl.BlockSpec((tm, tn), lambda i, j: (i, j)),
        out_shape=jax.ShapeDtypeStruct((t, n), BF16),
        scratch_shapes=[pltpu.VMEM((tm, d), BF16)],
        compiler_params=_params("parallel", "arbitrary"),
        name="norm_inproj",
    )(x, g.reshape(1, d), mods, mods, w)


def _softmax_pv(parts):
    m = functools.reduce(jnp.maximum, [s.max(axis=-1, keepdims=True) for s, _ in parts])
    num = 0.0
    den = 0.0
    for s, v in parts:
        p = jnp.exp(s - m)
        den = den + p.sum(axis=-1, keepdims=True)
        num = num + _dot(p.astype(BF16), v)
    return num / den


def _head_masks():
    lane = lax.broadcasted_iota(jnp.int32, (1, LANES), 1)
    first = lane < HEAD_DIM
    return first, [first, jnp.logical_not(first)]


def _attn_kernel(q_ref, k_ref, v_ref, kc_ref, vc_ref, tab_ref, o_ref, *, n_tiles, rows):
    first, sels = _head_masks()
    kc = kc_ref[...]
    vc = vc_ref[...]

    def tile(t, carry):
        q0 = pl.multiple_of(t * TQ, TQ)
        k0 = pl.multiple_of(jnp.clip(t * Q_ROWS - WIN_ROWS // 2, 0, rows - K_ROWS) * GRID_W, GRID_W)
        pat = jnp.where(t == 0, 0, jnp.where(t == n_tiles - 1, 2, 1))
        q = q_ref[pl.ds(q0, TQ), :] * ATTN_SCALE
        kw = k_ref[pl.ds(k0, NK), :]
        vw = v_ref[pl.ds(k0, NK), :]
        outs = []
        for hh in range(HEAD_PAIR):
            qm = jnp.where(sels[hh], q, jnp.zeros_like(q))
            s_loc = _dot_nt(qm, kw) + tab_ref[pat, hh]
            s_ctx = _dot_nt(qm, kc)
            outs.append(_softmax_pv([(s_loc, vw), (s_ctx, vc)]))
        o_ref[pl.ds(q0, TQ), :] = jnp.where(first, outs[0], outs[1]).astype(o_ref.dtype)
        return carry

    lax.fori_loop(0, n_tiles, tile, 0, unroll=2)


def _attn(proj, cproj, table, kc_blk, vc_blk):
    b, n, _ = proj.shape
    rows = n // GRID_W
    n_tiles = rows // Q_ROWS
    lat = lambda off: pl.BlockSpec((None, n, LANES), lambda p, i: (i, 0, off + p))
    cx = lambda off: pl.BlockSpec((None, CTX_LEN, LANES), lambda p, i: (i, 0, off + p))
    return pl.pallas_call(
        functools.partial(_attn_kernel, n_tiles=n_tiles, rows=rows),
        grid=(N_PAIRS, b),
        in_specs=[
            lat(0), lat(N_PAIRS), lat(2 * N_PAIRS), cx(kc_blk), cx(vc_blk),
            pl.BlockSpec((3, HEAD_PAIR, TQ, NK), lambda p, i: (0, p, 0, 0)),
        ],
        out_specs=pl.BlockSpec((None, n, LANES), lambda p, i: (i, 0, p)),
        out_shape=jax.ShapeDtypeStruct((b, n, NA_WIDTH), BF16),
        compiler_params=_params("parallel", "parallel"),
        name="nbr_attn",
    )(proj, proj, proj, cproj, cproj, table)


def _ctx_attn_kernel(q_ref, k_ref, v_ref, o_ref):
    first, sels = _head_masks()
    q = q_ref[...] * ATTN_SCALE
    k = k_ref[...]
    v = v_ref[...]
    outs = []
    for hh in range(HEAD_PAIR):
        qm = jnp.where(sels[hh], q, jnp.zeros_like(q))
        outs.append(_softmax_pv([(_dot_nt(qm, k), v)]))
    o_ref[...] = jnp.where(first, outs[0], outs[1]).astype(o_ref.dtype)


def _ctx_attn(cproj):
    b, n, _ = cproj.shape
    spec = lambda off: pl.BlockSpec((None, n, LANES), lambda i, p: (i, 0, off + p))
    return pl.pallas_call(
        _ctx_attn_kernel,
        grid=(b, N_PAIRS),
        in_specs=[spec(0), spec(N_PAIRS), spec(2 * N_PAIRS)],
        out_specs=spec(0),
        out_shape=jax.ShapeDtypeStruct((b, n, NA_WIDTH), BF16),
        compiler_params=_params("parallel", "parallel"),
        name="ctx_attn",
    )(cproj, cproj, cproj)


def _attn_bias_table(rpb):
    rows = GRID_W
    qc = np.arange(GRID_W)[:, None]
    kc = np.arange(GRID_W)[None, :]
    cs = np.clip(qc - WIN_COLS // 2, 0, GRID_W - WIN_COLS)
    col_ok = (kc >= cs) & (kc < cs + WIN_COLS)
    dc = kc - qc + WIN_COLS - 1
    onehot = np.zeros((2 * WIN_COLS - 1, GRID_W, GRID_W), np.float32)
    qi, ki = np.nonzero(col_ok)
    onehot[dc[qi, ki], qi, ki] = 1.0
    t1 = jnp.einsum('hrd,dqk->hrqk', rpb, jnp.asarray(onehot), precision=lax.Precision.HIGHEST)
    t1 = jnp.where(jnp.asarray(col_ok)[None, None], t1, NEG)
    neg_slab = jnp.full((NA_HEADS, 1, GRID_W, GRID_W), NEG, F32)
    t1 = jnp.concatenate([t1, neg_slab], axis=1)
    n_tiles = rows // Q_ROWS
    idx = np.full((3, Q_ROWS, K_ROWS), 2 * WIN_ROWS - 1, np.int32)
    for p, t in enumerate((0, 1, n_tiles - 1)):
        k_start = int(np.clip(t * Q_ROWS - WIN_ROWS // 2, 0, rows - K_ROWS))
        for qr in range(Q_ROWS):
            r = t * Q_ROWS + qr
            rs = int(np.clip(r - WIN_ROWS // 2, 0, rows - WIN_ROWS))
            for kr in range(K_ROWS):
                krow = k_start + kr
                if rs <= krow < rs + WIN_ROWS:
                    idx[p, qr, kr] = krow - r + WIN_ROWS - 1
    g = jnp.take(t1, jnp.asarray(idx.reshape(-1)), axis=1)
    g = g.reshape(NA_HEADS, 3, Q_ROWS, K_ROWS, GRID_W, GRID_W)
    return g.transpose(1, 0, 2, 4, 3, 5).reshape(3, NA_HEADS, TQ, NK)


def _split_bf16(a):
    hi = a.astype(BF16)
    lo = (a - hi.astype(F32)).astype(BF16)
    return hi, lo


def _dot3(a, b):
    ah, al = _split_bf16(a)
    bh, bl = _split_bf16(b)
    return _dot(ah, bh) + (_dot(ah, bl) + _dot(al, bh))


def _four_w_kernel(cc_ref, sc_ref, w_ref, o_ref):
    w = w_ref[...]
    o_ref[:, :FOURIER_GROUP_DIM] = _dot3(cc_ref[...], w).astype(BF16)
    o_ref[:, FOURIER_GROUP_DIM:] = _dot3(sc_ref[...], w).astype(BF16)


def _four_w(cc, sc, w_four):
    depth, g, c, _ = w_four.shape
    tab = pl.BlockSpec((c, c), lambda l, i: (0, 0))
    return pl.pallas_call(
        _four_w_kernel,
        grid=(depth, g),
        in_specs=[tab, tab, pl.BlockSpec((None, None, c, c), lambda l, i: (l, i, 0, 0))],
        out_specs=pl.BlockSpec((None, None, c, 2 * c), lambda l, i: (l, i, 0, 0)),
        out_shape=jax.ShapeDtypeStruct((depth, g, c, 2 * c), BF16),
        compiler_params=_params("parallel", "parallel"),
        name="fourier_weights",
    )(cc, sc, w_four)


def _four_chan_kernel(f_ref, m_ref, yc_ref, ys_ref):
    c = FOURIER_GROUP_DIM
    for g in range(FOURIER_GROUPS):
        y = _dot(f_ref[:, g * c:(g + 1) * c], m_ref[g])
        yc_ref[:, g * c:(g + 1) * c] = y[:, :c].astype(BF16)
        ys_ref[:, g * c:(g + 1) * c] = y[:, c:].astype(BF16)


def _four_chan(proj, mw, tm):
    t = proj.shape[0]
    f_blk = proj.shape[1] // FOURIER_WIDTH - 1
    out = pl.BlockSpec((tm, FOURIER_WIDTH), lambda i: (i, 0))
    return pl.pallas_call(
        _four_chan_kernel,
        grid=(t // tm,),
        in_specs=[
            pl.BlockSpec((tm, FOURIER_WIDTH), lambda i: (i, f_blk)),
            pl.BlockSpec(mw.shape, lambda i: (0, 0, 0)),
        ],
        out_specs=[out, out],
        out_shape=[jax.ShapeDtypeStruct((t, FOURIER_WIDTH), BF16)] * 2,
        compiler_params=_params("parallel"),
        name="fourier_chan",
    )(proj, mw)


def _four_pos_kernel(ce_ref, se_ref, co_ref, so_ref, ae_ref, ao_ref, be_ref, bo_ref, o_ref):
    even = _dot(ce_ref[...], ae_ref[...]) + _dot(se_ref[...], be_ref[...])
    odd = _dot(co_ref[...], ao_ref[...]) + _dot(so_ref[...], bo_ref[...])
    o_ref[0] = (even + odd).astype(o_ref.dtype)
    o_ref[1] = (even - odd).astype(o_ref.dtype)


def _four_pos(tables, yc, ys, tm, tn):
    b, n, w = yc.shape
    half = n // 2
    nj = w // tn
    yc2 = yc.reshape(b, half, 2 * w)
    ys2 = ys.reshape(b, half, 2 * w)
    mat = pl.BlockSpec((tm, half), lambda i, bb, j: (i, 0))
    ev = pl.BlockSpec((None, half, tn), lambda i, bb, j: (bb, 0, j))
    od = pl.BlockSpec((None, half, tn), lambda i, bb, j: (bb, 0, nj + j))
    out = pl.pallas_call(
        _four_pos_kernel,
        grid=(half // tm, b, nj),
        in_specs=[mat, mat, mat, mat, ev, od, ev, od],
        out_specs=pl.BlockSpec((None, 2, tm, tn), lambda i, bb, j: (bb, 0, i, j)),
        out_shape=jax.ShapeDtypeStruct((b, 2, half, w), BF16),
        compiler_params=_params("parallel", "parallel", "parallel"),
        name="fourier_pos",
    )(*tables, yc2, yc2, ys2, ys2)
    return out.reshape(b, n, w)


def _dft_tables(n, split):
    half = n // 2
    j = jnp.arange(half, dtype=jnp.int32)[:, None]
    hi = jnp.arange(half // split, dtype=jnp.int32)[None, :] * split
    lo = jnp.arange(split, dtype=jnp.int32)[None, :]
    ang = lambda k: (2.0 * np.pi / half) * ((j * k) % half).astype(F32)
    ch, sh, cl, sl = jnp.cos(ang(hi)), jnp.sin(ang(hi)), jnp.cos(ang(lo)), jnp.sin(ang(lo))
    ce = (ch[:, :, None] * cl[:, None, :] - sh[:, :, None] * sl[:, None, :]).reshape(half, half)
    se = (sh[:, :, None] * cl[:, None, :] + ch[:, :, None] * sl[:, None, :]).reshape(half, half)
    phi = (2.0 * np.pi / n) * j.astype(F32)
    cp, sp = jnp.cos(phi), jnp.sin(phi)
    co = ce * cp - se * sp
    so = se * cp + ce * sp
    return tuple(t.astype(BF16) for t in (ce, se, co, so))


def _outproj_kernel(a_ref, f_ref, w_ref, x_ref, gpost_ref, gate_ref, gpre_ref, sh_ref, sc_ref,
                    xo_ref, h_ref):
    mix = _dot(a_ref[...], w_ref[:NA_WIDTH, :]) + _dot(f_ref[...], w_ref[NA_WIDTH:, :])
    x = x_ref[...] + gate_ref[...] * _rms(mix, gpost_ref[...])
    xo_ref[...] = x
    h_ref[...] = _norm_mod(x, gpre_ref[...], sh_ref[...], sc_ref[...]).astype(BF16)


def _outproj(attn, four, w, x, g_post, g_pre, mods, row_fn, tm):
    t, d = x.shape
    vec = pl.BlockSpec((1, d), lambda i: (0, 0))
    row = pl.BlockSpec((tm, d), lambda i: (i, 0))
    half = pl.BlockSpec((tm, NA_WIDTH), lambda i: (i, 0))
    return pl.pallas_call(
        _outproj_kernel,
        grid=(t // tm,),
        in_specs=[half, half, pl.BlockSpec(w.shape, lambda i: (0, 0)), row, vec,
                  _mod_spec(2, row_fn), vec, _mod_spec(3, row_fn), _mod_spec(4, row_fn)],
        out_specs=[row, row],
        out_shape=[jax.ShapeDtypeStruct((t, d), F32), jax.ShapeDtypeStruct((t, d), BF16)],
        compiler_params=_params("parallel"),
        name="outproj_residual",
    )(attn, four, w, x, g_post.reshape(1, d), mods, g_pre.reshape(1, d), mods, mods)


def _matmul_kernel(x_ref, w_ref, o_ref):
    o_ref[...] = _dot(x_ref[...], w_ref[...]).astype(o_ref.dtype)


def _matmul(x, w, tn, out_dtype, name):
    m, k = x.shape
    n = w.shape[1]
    return pl.pallas_call(
        _matmul_kernel,
        grid=(n // tn,),
        in_specs=[pl.BlockSpec((m, k), lambda j: (0, 0)), pl.BlockSpec((k, tn), lambda j: (0, j))],
        out_specs=pl.BlockSpec((m, tn), lambda j: (0, j)),
        out_shape=jax.ShapeDtypeStruct((m, n), out_dtype),
        compiler_params=_params("parallel"),
        name=name,
    )(x, w)


def _ffn_kernel(h_ref, wa_ref, wg_ref, pa_ref, pg_ref, na_ref, ng_ref, cwa_ref, cwg_ref, cba_ref, cbg_ref,
                wd_ref, x_ref, gate_ref, gpost_ref, o_ref, acc_ref, *act_refs, tm, n_ff_tiles):
    j = pl.program_id(1)
    nj = pl.num_programs(1) - 1

    pad = F32_SUBLANES
    tf = wa_ref.shape[1]

    def up_proj(w_ref, prev_ref, next_ref):
        u = _dot(h_ref[...], w_ref[...])
        before = jnp.broadcast_to(prev_ref[...], (pad, tf))
        after = jnp.broadcast_to(next_ref[...], (pad, tf))
        return jnp.concatenate([before, u, after], axis=0)

    def conv(ue, r0, cols, cw_ref, cb_ref):
        win = ue[r0:r0 + CONV_ROWS + 2 * pad, cols]
        mid = win[pad:pad + CONV_ROWS]
        down = pltpu.roll(win, 1, axis=0)[pad:pad + CONV_ROWS]
        up = pltpu.roll(win, CONV_ROWS + 2 * pad - 1, axis=0)[pad:pad + CONV_ROWS]
        return down * cw_ref[0:1, cols] + mid * cw_ref[1:2, cols] + up * cw_ref[2:3, cols] + cb_ref[:, cols]

    def gated_tile(dst_ref):
        ua = up_proj(wa_ref, pa_ref, na_ref)
        ug = up_proj(wg_ref, pg_ref, ng_ref)
        for r0 in range(0, tm, CONV_ROWS):
            for c0 in range(0, tf, LANES):
                cols = slice(c0, c0 + LANES)
                a = conv(ua, r0, cols, cwa_ref, cba_ref)
                g = conv(ug, r0, cols, cwg_ref, cbg_ref)
                dst_ref[r0:r0 + CONV_ROWS, cols] = (g * jax.nn.sigmoid(g) * a).astype(BF16)

    @pl.when(j == 0)
    def _():
        acc_ref[...] = jnp.zeros_like(acc_ref)
        gated_tile(act_refs[0])

    for parity in range(2):
        @pl.when((j > 0) & (j < nj) & (j % 2 == parity))
        def _():
            gated_tile(act_refs[parity])
            acc_ref[...] += _dot(act_refs[1 - parity][...], wd_ref[...])

    @pl.when(j == nj)
    def _():
        last = act_refs[(n_ff_tiles - 1) % 2]
        y = acc_ref[...] + _dot(last[...], wd_ref[...])
        o_ref[...] = x_ref[...] + gate_ref[...] * _rms(y, gpost_ref[...])


def _ffn(h, w_up, conv_w, conv_b, w_down, x, g_post, mods, row_fn, seq_len, tm, tf):
    t, d = x.shape
    ff = w_down.shape[0]
    nj = ff // tf
    n_tiles = t // tm
    assert seq_len % tm == 0

    h3 = h.reshape(n_tiles, tm, d)
    start = (np.arange(n_tiles) * tm) % seq_len == 0
    end = ((np.arange(n_tiles) + 1) * tm) % seq_len == 0
    zero = jnp.zeros((1, d), h.dtype)
    prev = jnp.concatenate([zero, h3[:-1, tm - 1]], axis=0)
    nxt = jnp.concatenate([h3[1:, 0], zero], axis=0)
    prev = jnp.where(jnp.asarray(start)[:, None], jnp.zeros_like(prev), prev)
    nxt = jnp.where(jnp.asarray(end)[:, None], jnp.zeros_like(nxt), nxt)
    edge_rows = jnp.concatenate([prev, nxt], axis=0)
    pad = -edge_rows.shape[0] % BF16_SUBLANES
    edge_rows = jnp.pad(edge_rows, ((0, pad), (0, 0)))
    u_edge = _matmul(edge_rows, w_up, 2 * ff // 8, F32, "conv_ffn_edges")
    u_edge = u_edge[:2 * n_tiles].reshape(2, n_tiles, 1, 2 * ff)

    vec = pl.BlockSpec((1, d), lambda i, j: (0, 0))
    row = pl.BlockSpec((tm, d), lambda i, j: (i, 0))
    up_j = lambda j: jnp.minimum(j, nj - 1)
    down_j = lambda j: jnp.maximum(j - 1, 0)
    edge = lambda side, off: pl.BlockSpec((None, None, 1, tf), lambda i, j: (side, i, 0, off + up_j(j)))
    col = lambda rows, off: pl.BlockSpec((rows, tf), lambda i, j: (0, off + up_j(j)))
    cb = conv_b.reshape(1, 2 * ff)
    return pl.pallas_call(
        functools.partial(_ffn_kernel, tm=tm, n_ff_tiles=nj),
        grid=(n_tiles, nj + 1),
        in_specs=[
            row,
            col(d, 0), col(d, nj),
            edge(0, 0), edge(0, nj), edge(1, 0), edge(1, nj),
            col(3, 0), col(3, nj), col(1, 0), col(1, nj),
            pl.BlockSpec((tf, d), lambda i, j: (down_j(j), 0)),
            row,
            _mod_spec(5, row_fn),
            vec,
        ],
        out_specs=row,
        out_shape=jax.ShapeDtypeStruct((t, d), F32),
        scratch_shapes=[pltpu.VMEM((tm, d), F32), pltpu.VMEM((tm, tf), BF16), pltpu.VMEM((tm, tf), BF16)],
        compiler_params=_params("parallel", "arbitrary"),
        name="conv_ffn",
    )(h, w_up, w_up, u_edge, u_edge, u_edge, u_edge, conv_w, conv_w, cb, cb, w_down, x, mods,
      g_post.reshape(1, d))


def kernel(x, c, ctx, c_ctx, w_ada, b_ada, g_pre_mix, w_in, rpb, w_four, w_out, g_post_mix,
           g_pre_ffn, w_up, conv_w, conv_b, w_down, g_post_ffn):
    b, n, d = x.shape
    n_ctx = ctx.shape[1]
    assert (d, n, n_ctx, b) == (D_MODEL, GRID_W * GRID_W, CTX_LEN, CTX_MOD_ROW)
    t_lat, t_ctx = b * n, b * n_ctx
    tm_lat, tm_ctx = 512, CTX_LEN
    lat_row = lambda i: (i * tm_lat) // n
    ctx_row = lambda i: CTX_MOD_ROW

    cond = jnp.zeros((MOD_ROWS, d), F32).at[:b].set(c).at[CTX_MOD_ROW].set(c_ctx)
    mods_all = _mods(cond, w_ada, b_ada)

    scale = (n * FOURIER_GROUP_DIM) ** -0.5
    scale_ctx = (n_ctx * FOURIER_GROUP_DIM) ** -0.5
    kk = np.outer(np.arange(FOURIER_GROUP_DIM), np.arange(FOURIER_GROUP_DIM)) % FOURIER_GROUP_DIM
    ang = 2.0 * np.pi * kk / FOURIER_GROUP_DIM
    cc, sc = np.cos(ang), -np.sin(ang)
    mw_lat = _four_w(jnp.asarray(cc * scale, F32), jnp.asarray(sc * scale, F32), w_four)
    dft_lat = _dft_tables(n, GRID_W)

    xt = x.reshape(t_lat, d)
    ct = ctx.reshape(t_ctx, d)
    for l in range(DEPTH):
        last = l == DEPTH - 1
        mods = mods_all[l].reshape(MOD_ROWS, N_MOD, 1, d)
        w_in_l = w_in[l].astype(BF16)
        w_out_l = w_out[l].astype(BF16)
        w_up_l = w_up[l].astype(BF16)
        w_down_l = w_down[l].astype(BF16)

        proj = _inproj(xt, g_pre_mix[l], mods, lat_row, w_in_l, tm_lat, 1024)
        if last:
            cproj = _inproj(ct, g_pre_mix[l], mods, ctx_row, w_in_l[:, NA_WIDTH:3 * NA_WIDTH], tm_ctx, 1024)
            kc_blk, vc_blk = 0, N_PAIRS
        else:
            cproj = _inproj(ct, g_pre_mix[l], mods, ctx_row, w_in_l, tm_ctx, 1024)
            kc_blk, vc_blk = N_PAIRS, 2 * N_PAIRS
        cproj3 = cproj.reshape(b, n_ctx, cproj.shape[1])

        attn = _attn(proj.reshape(b, n, PROJ_WIDTH), cproj3, _attn_bias_table(rpb[l]), kc_blk, vc_blk)
        yc, ys = _four_chan(proj, mw_lat[l], 1024)
        four = _four_pos(dft_lat, yc.reshape(b, n, FOURIER_WIDTH), ys.reshape(b, n, FOURIER_WIDTH),
                         512, 512)
        xt, h2 = _outproj(attn.reshape(t_lat, NA_WIDTH), four.reshape(t_lat, FOURIER_WIDTH), w_out_l, xt,
                          g_post_mix[l], g_pre_ffn[l], mods, lat_row, tm_lat)
        xt = _ffn(h2, w_up_l, conv_w[l], conv_b[l], w_down_l, xt, g_post_ffn[l], mods, lat_row, n,
                  tm_lat, 512)

        if not last:
            mw_ctx = _four_w(jnp.asarray(cc * scale_ctx, F32), jnp.asarray(sc * scale_ctx, F32),
                             w_four[l:l + 1])[0]
            dft_ctx = _dft_tables(n_ctx, 16)
            attn_c = _ctx_attn(cproj3)
            yc, ys = _four_chan(cproj, mw_ctx, 512)
            four_c = _four_pos(dft_ctx, yc.reshape(b, n_ctx, FOURIER_WIDTH),
                               ys.reshape(b, n_ctx, FOURIER_WIDTH), n_ctx // 2, 512)
            ct, hc2 = _outproj(attn_c.reshape(t_ctx, NA_WIDTH), four_c.reshape(t_ctx, FOURIER_WIDTH),
                               w_out_l, ct, g_post_mix[l], g_pre_ffn[l], mods, ctx_row, tm_ctx)
            ct = _ffn(hc2, w_up_l, conv_w[l], conv_b[l], w_down_l, ct, g_post_ffn[l], mods, ctx_row,
                      n_ctx, tm_ctx, 512)
    return xt.reshape(b, n, d)
```

```python
import functools

import jax
import jax.numpy as jnp
import numpy as np
from jax import lax
from jax.experimental import pallas as pl
from jax.experimental.pallas import tpu as pltpu

D_MODEL = 2048
DEPTH = 2
GRID_W = 64
CTX_LEN = 256
NA_HEADS = 16
HEAD_DIM = 64
NA_WIDTH = NA_HEADS * HEAD_DIM
FOURIER_GROUPS = 8
FOURIER_GROUP_DIM = 128
FOURIER_WIDTH = FOURIER_GROUPS * FOURIER_GROUP_DIM
PROJ_WIDTH = 3 * NA_WIDTH + FOURIER_WIDTH
WIN_ROWS = 8
WIN_COLS = 16
D_FF = 5632
N_MOD = 6
EPS = 1e-6
ATTN_SCALE = HEAD_DIM ** -0.5
MOD_ROWS = 16
CTX_MOD_ROW = 8

LANES = 128
UP_K = 512
DOWN_COLS = 512
CONV_ROWS = 64
F32_SUBLANES = 8
BF16_SUBLANES = 16
HEAD_PAIR = LANES // HEAD_DIM
N_PAIRS = NA_HEADS // HEAD_PAIR
Q_ROWS = 4
K_ROWS = Q_ROWS + WIN_ROWS - 1
TQ = Q_ROWS * GRID_W
NK = K_ROWS * GRID_W
NEG = -1e30
VMEM_LIMIT = 56 * 1024 * 1024

F32 = jnp.float32
BF16 = jnp.bfloat16


def _params(*sem, flags=None):
    return pltpu.CompilerParams(dimension_semantics=sem, vmem_limit_bytes=VMEM_LIMIT, flags=flags)


def _dot(a, b):
    return jnp.dot(a, b, preferred_element_type=F32)


def _dot_nt(a, b):
    return lax.dot_general(a, b, (((1,), (1,)), ((), ())), preferred_element_type=F32)


def _rms(x, g):
    ms = jnp.mean(x * x, axis=-1, keepdims=True)
    return x * lax.rsqrt(ms + EPS) * g


def _norm_mod(x, g, shift, scale):
    return _rms(x, g) * (1.0 + scale) + shift


def _mods_kernel(cond_ref, w_ref, b_ref, o_ref):
    c = cond_ref[...]
    s = (c * jax.nn.sigmoid(c)).astype(BF16)
    o_ref[...] = _dot(s, w_ref[...].astype(BF16)) + b_ref[...]


def _mods(cond, w_ada, b_ada, tn=1024):
    depth, d, n = w_ada.shape
    return pl.pallas_call(
        _mods_kernel,
        grid=(depth, n // tn),
        in_specs=[
            pl.BlockSpec((MOD_ROWS, d), lambda l, j: (0, 0)),
            pl.BlockSpec((None, d, tn), lambda l, j: (l, 0, j)),
            pl.BlockSpec((None, 1, tn), lambda l, j: (l, 0, j)),
        ],
        out_specs=pl.BlockSpec((None, MOD_ROWS, tn), lambda l, j: (l, 0, j)),
        out_shape=jax.ShapeDtypeStruct((depth, MOD_ROWS, n), F32),
        compiler_params=_params("parallel", "parallel"),
        name="adaln_mods",
    )(cond, w_ada, b_ada.reshape(depth, 1, n))


def _mod_spec(which, row_fn):
    return pl.BlockSpec((None, None, 1, D_MODEL),
                        lambda i, *_: (row_fn(i), which, 0, 0))


def _inproj_kernel(x_ref, g_ref, sh_ref, sc_ref, w_ref, o_ref, h_ref):
    @pl.when(pl.program_id(1) == 0)
    def _():
        h_ref[...] = _norm_mod(x_ref[...], g_ref[...], sh_ref[...], sc_ref[...]).astype(BF16)

    o_ref[...] = _dot(h_ref[...], w_ref[...]).astype(o_ref.dtype)


def _inproj(x, g, mods, row_fn, w, tm, tn):
    t, d = x.shape
    n = w.shape[1]
    return pl.pallas_call(
        _inproj_kernel,
        grid=(t // tm, n // tn),
        in_specs=[
            pl.BlockSpec((tm, d), lambda i, j: (i, 0)),
            pl.BlockSpec((1, d), lambda i, j: (0, 0)),
            _mod_spec(0, row_fn),
            _mod_spec(1, row_fn),
            pl.BlockSpec((d, tn), lambda i, j: (0, j)),
        ],
        out_specs=pl.BlockSpec((tm, tn), lambda i, j: (i, j)),
        out_shape=jax.ShapeDtypeStruct((t, n), BF16),
        scratch_shapes=[pltpu.VMEM((tm, d), BF16)],
        compiler_params=_params("parallel", "arbitrary"),
        name="norm_inproj",
    )(x, g.reshape(1, d), mods, mods, w)


def _softmax_pv(parts):
    m = functools.reduce(jnp.maximum, [s.max(axis=-1, keepdims=True) for s, _ in parts])
    num = 0.0
    den = 0.0
    for s, v in parts:
        p = jnp.exp(s - m)
        den = den + p.sum(axis=-1, keepdims=True)
        num = num + _dot(p.astype(BF16), v)
    return num / den


def _head_masks():
    lane = lax.broadcasted_iota(jnp.int32, (1, LANES), 1)
    first = lane < HEAD_DIM
    return first, [first, jnp.logical_not(first)]


def _attn_kernel(q_ref, k_ref, v_ref, kc_ref, vc_ref, tab_ref, o_ref, *, n_tiles, rows):
    first, sels = _head_masks()
    kc = kc_ref[...]
    vc = vc_ref[...]

    def tile(t, carry):
        q0 = pl.multiple_of(t * TQ, TQ)
        k0 = pl.multiple_of(jnp.clip(t * Q_ROWS - WIN_ROWS // 2, 0, rows - K_ROWS) * GRID_W, GRID_W)
        pat = jnp.where(t == 0, 0, jnp.where(t == n_tiles - 1, 2, 1))
        q = q_ref[pl.ds(q0, TQ), :] * ATTN_SCALE
        kw = k_ref[pl.ds(k0, NK), :]
        vw = v_ref[pl.ds(k0, NK), :]
        outs = []
        for hh in range(HEAD_PAIR):
            qm = jnp.where(sels[hh], q, jnp.zeros_like(q))
            s_loc = _dot_nt(qm, kw) + tab_ref[pat, hh]
            s_ctx = _dot_nt(qm, kc)
            outs.append(_softmax_pv([(s_loc, vw), (s_ctx, vc)]))
        o_ref[pl.ds(q0, TQ), :] = jnp.where(first, outs[0], outs[1]).astype(o_ref.dtype)
        return carry

    lax.fori_loop(0, n_tiles, tile, 0, unroll=2)


def _attn(proj, cproj, table, kc_blk, vc_blk):
    b, n, _ = proj.shape
    rows = n // GRID_W
    n_tiles = rows // Q_ROWS
    lat = lambda off: pl.BlockSpec((None, n, LANES), lambda p, i: (i, 0, off + p))
    cx = lambda off: pl.BlockSpec((None, CTX_LEN, LANES), lambda p, i: (i, 0, off + p))
    return pl.pallas_call(
        functools.partial(_attn_kernel, n_tiles=n_tiles, rows=rows),
        grid=(N_PAIRS, b),
        in_specs=[
            lat(0), lat(N_PAIRS), lat(2 * N_PAIRS), cx(kc_blk), cx(vc_blk),
            pl.BlockSpec((3, HEAD_PAIR, TQ, NK), lambda p, i: (0, p, 0, 0)),
        ],
        out_specs=pl.BlockSpec((None, n, LANES), lambda p, i: (i, 0, p)),
        out_shape=jax.ShapeDtypeStruct((b, n, NA_WIDTH), BF16),
        compiler_params=_params("parallel", "parallel"),
        name="nbr_attn",
    )(proj, proj, proj, cproj, cproj, table)


def _ctx_attn_kernel(q_ref, k_ref, v_ref, o_ref):
    first, sels = _head_masks()
    q = q_ref[...] * ATTN_SCALE
    k = k_ref[...]
    v = v_ref[...]
    outs = []
    for hh in range(HEAD_PAIR):
        qm = jnp.where(sels[hh], q, jnp.zeros_like(q))
        outs.append(_softmax_pv([(_dot_nt(qm, k), v)]))
    o_ref[...] = jnp.where(first, outs[0], outs[1]).astype(o_ref.dtype)


def _ctx_attn(cproj):
    b, n, _ = cproj.shape
    spec = lambda off: pl.BlockSpec((None, n, LANES), lambda i, p: (i, 0, off + p))
    return pl.pallas_call(
        _ctx_attn_kernel,
        grid=(b, N_PAIRS),
        in_specs=[spec(0), spec(N_PAIRS), spec(2 * N_PAIRS)],
        out_specs=spec(0),
        out_shape=jax.ShapeDtypeStruct((b, n, NA_WIDTH), BF16),
        compiler_params=_params("parallel", "parallel"),
        name="ctx_attn",
    )(cproj, cproj, cproj)


def _attn_bias_table(rpb):
    rows = GRID_W
    qc = np.arange(GRID_W)[:, None]
    kc = np.arange(GRID_W)[None, :]
    cs = np.clip(qc - WIN_COLS // 2, 0, GRID_W - WIN_COLS)
    col_ok = (kc >= cs) & (kc < cs + WIN_COLS)
    dc = kc - qc + WIN_COLS - 1
    onehot = np.zeros((2 * WIN_COLS - 1, GRID_W, GRID_W), np.float32)
    qi, ki = np.nonzero(col_ok)
    onehot[dc[qi, ki], qi, ki] = 1.0
    t1 = jnp.einsum('hrd,dqk->hrqk', rpb, jnp.asarray(onehot), precision=lax.Precision.HIGHEST)
    t1 = jnp.where(jnp.asarray(col_ok)[None, None], t1, NEG)
    neg_slab = jnp.full((NA_HEADS, 1, GRID_W, GRID_W), NEG, F32)
    t1 = jnp.concatenate([t1, neg_slab], axis=1)
    n_tiles = rows // Q_ROWS
    idx = np.full((3, Q_ROWS, K_ROWS), 2 * WIN_ROWS - 1, np.int32)
    for p, t in enumerate((0, 1, n_tiles - 1)):
        k_start = int(np.clip(t * Q_ROWS - WIN_ROWS // 2, 0, rows - K_ROWS))
        for qr in range(Q_ROWS):
            r = t * Q_ROWS + qr
            rs = int(np.clip(r - WIN_ROWS // 2, 0, rows - WIN_ROWS))
            for kr in range(K_ROWS):
                krow = k_start + kr
                if rs <= krow < rs + WIN_ROWS:
                    idx[p, qr, kr] = krow - r + WIN_ROWS - 1
    g = jnp.take(t1, jnp.asarray(idx.reshape(-1)), axis=1)
    g = g.reshape(NA_HEADS, 3, Q_ROWS, K_ROWS, GRID_W, GRID_W)
    return g.transpose(1, 0, 2, 4, 3, 5).reshape(3, NA_HEADS, TQ, NK)


def _split_bf16(a):
    hi = a.astype(BF16)
    lo = (a - hi.astype(F32)).astype(BF16)
    return hi, lo


def _dot3(a, b):
    ah, al = _split_bf16(a)
    bh, bl = _split_bf16(b)
    return _dot(ah, bh) + (_dot(ah, bl) + _dot(al, bh))


def _four_w_kernel(cc_ref, sc_ref, w_ref, o_ref):
    w = w_ref[...]
    o_ref[:, :FOURIER_GROUP_DIM] = _dot3(cc_ref[...], w).astype(BF16)
    o_ref[:, FOURIER_GROUP_DIM:] = _dot3(sc_ref[...], w).astype(BF16)


def _four_w(cc, sc, w_four):
    depth, g, c, _ = w_four.shape
    tab = pl.BlockSpec((c, c), lambda l, i: (0, 0))
    return pl.pallas_call(
        _four_w_kernel,
        grid=(depth, g),
        in_specs=[tab, tab, pl.BlockSpec((None, None, c, c), lambda l, i: (l, i, 0, 0))],
        out_specs=pl.BlockSpec((None, None, c, 2 * c), lambda l, i: (l, i, 0, 0)),
        out_shape=jax.ShapeDtypeStruct((depth, g, c, 2 * c), BF16),
        compiler_params=_params("parallel", "parallel"),
        name="fourier_weights",
    )(cc, sc, w_four)


def _four_chan_kernel(f_ref, m_ref, yce_ref, yco_ref, yse_ref, yso_ref, yc_scr, ys_scr):
    c = FOURIER_GROUP_DIM
    half = f_ref.shape[0] // 2
    for g in range(FOURIER_GROUPS):
        cols = slice(g * c, (g + 1) * c)
        y = _dot(f_ref[:, cols], m_ref[g])
        for scr, part, even_ref, odd_ref in ((yc_scr, y[:, :c], yce_ref, yco_ref),
                                             (ys_scr, y[:, c:], yse_ref, yso_ref)):
            scr[...] = part
            even_ref[:, cols] = scr[pl.ds(0, half, stride=2), :].astype(BF16)
            odd_ref[:, cols] = scr[pl.ds(1, half, stride=2), :].astype(BF16)


def _four_chan(proj, mw, tm):
    t = proj.shape[0]
    f_blk = proj.shape[1] // FOURIER_WIDTH - 1
    out = pl.BlockSpec((tm // 2, FOURIER_WIDTH), lambda i: (i, 0))
    return pl.pallas_call(
        _four_chan_kernel,
        grid=(t // tm,),
        in_specs=[
            pl.BlockSpec((tm, FOURIER_WIDTH), lambda i: (i, f_blk)),
            pl.BlockSpec(mw.shape, lambda i: (0, 0, 0)),
        ],
        out_specs=[out] * 4,
        out_shape=[jax.ShapeDtypeStruct((t // 2, FOURIER_WIDTH), BF16)] * 4,
        scratch_shapes=[pltpu.VMEM((tm, FOURIER_GROUP_DIM), F32)] * 2,
        compiler_params=_params("parallel"),
        name="fourier_chan",
    )(proj, mw)


def _four_pos_kernel(ce_ref, se_ref, co_ref, so_ref, ae_ref, ao_ref, be_ref, bo_ref, o_ref):
    even = _dot(ce_ref[...], ae_ref[...]) + _dot(se_ref[...], be_ref[...])
    odd = _dot(co_ref[...], ao_ref[...]) + _dot(so_ref[...], bo_ref[...])
    o_ref[0] = (even + odd).astype(o_ref.dtype)
    o_ref[1] = (even - odd).astype(o_ref.dtype)


def _four_pos(tables, y_parts, b, tm, tn):
    yce, yco, yse, yso = (y.reshape(b, y.shape[0] // b, y.shape[1]) for y in y_parts)
    _, half, w = yce.shape
    mat = pl.BlockSpec((tm, half), lambda i, bb, j: (i, 0))
    dat = pl.BlockSpec((None, half, tn), lambda i, bb, j: (bb, 0, j))
    out = pl.pallas_call(
        _four_pos_kernel,
        grid=(half // tm, b, w // tn),
        in_specs=[mat, mat, mat, mat, dat, dat, dat, dat],
        out_specs=pl.BlockSpec((None, 2, tm, tn), lambda i, bb, j: (bb, 0, i, j)),
        out_shape=jax.ShapeDtypeStruct((b, 2, half, w), BF16),
        compiler_params=_params("parallel", "parallel", "parallel"),
        name="fourier_pos",
    )(*tables, yce, yco, yse, yso)
    return out.reshape(b, 2 * half, w)


def _dft_tables(n, split):
    half = n // 2
    j = jnp.arange(half, dtype=jnp.int32)[:, None]
    hi = jnp.arange(half // split, dtype=jnp.int32)[None, :] * split
    lo = jnp.arange(split, dtype=jnp.int32)[None, :]
    ang = lambda k: (2.0 * np.pi / half) * ((j * k) % half).astype(F32)
    ch, sh, cl, sl = jnp.cos(ang(hi)), jnp.sin(ang(hi)), jnp.cos(ang(lo)), jnp.sin(ang(lo))
    ce = (ch[:, :, None] * cl[:, None, :] - sh[:, :, None] * sl[:, None, :]).reshape(half, half)
    se = (sh[:, :, None] * cl[:, None, :] + ch[:, :, None] * sl[:, None, :]).reshape(half, half)
    phi = (2.0 * np.pi / n) * j.astype(F32)
    cp, sp = jnp.cos(phi), jnp.sin(phi)
    co = ce * cp - se * sp
    so = se * cp + ce * sp
    return tuple(t.astype(BF16) for t in (ce, se, co, so))


def _outproj_kernel(a_ref, f_ref, w_ref, x_ref, gpost_ref, gate_ref, gpre_ref, sh_ref, sc_ref,
                    xo_ref, h_ref):
    mix = _dot(a_ref[...], w_ref[:NA_WIDTH, :]) + _dot(f_ref[...], w_ref[NA_WIDTH:, :])
    x = x_ref[...] + gate_ref[...] * _rms(mix, gpost_ref[...])
    xo_ref[...] = x
    h_ref[...] = _norm_mod(x, gpre_ref[...], sh_ref[...], sc_ref[...]).astype(BF16)


def _outproj(attn, four, w, x, g_post, g_pre, mods, row_fn, tm):
    t, d = x.shape
    vec = pl.BlockSpec((1, d), lambda i: (0, 0))
    row = pl.BlockSpec((tm, d), lambda i: (i, 0))
    half = pl.BlockSpec((tm, NA_WIDTH), lambda i: (i, 0))
    return pl.pallas_call(
        _outproj_kernel,
        grid=(t // tm,),
        in_specs=[half, half, pl.BlockSpec(w.shape, lambda i: (0, 0)), row, vec,
                  _mod_spec(2, row_fn), vec, _mod_spec(3, row_fn), _mod_spec(4, row_fn)],
        out_specs=[row, row],
        out_shape=[jax.ShapeDtypeStruct((t, d), F32), jax.ShapeDtypeStruct((t, d), BF16)],
        compiler_params=_params("parallel"),
        name="outproj_residual",
    )(attn, four, w, x, g_post.reshape(1, d), mods, g_pre.reshape(1, d), mods, mods)


def _matmul_kernel(x_ref, w_ref, o_ref):
    o_ref[...] = _dot(x_ref[...], w_ref[...]).astype(o_ref.dtype)


def _matmul(x, w, tn, out_dtype, name):
    m, k = x.shape
    n = w.shape[1]
    return pl.pallas_call(
        _matmul_kernel,
        grid=(n // tn,),
        in_specs=[pl.BlockSpec((m, k), lambda j: (0, 0)), pl.BlockSpec((k, tn), lambda j: (0, j))],
        out_specs=pl.BlockSpec((m, tn), lambda j: (0, j)),
        out_shape=jax.ShapeDtypeStruct((m, n), out_dtype),
        compiler_params=_params("parallel"),
        name=name,
    )(x, w)


def _ffn_kernel(h_ref, wa_ref, wg_ref, pa_ref, pg_ref, na_ref, ng_ref, cwa_ref, cwg_ref, cba_ref, cbg_ref,
                wd_ref, x_ref, gate_ref, gpost_ref, o_ref, acc_ref, ua0, ug0, ua1, ug1, act0, act1,
                *, tm, n_ff_tiles):
    j = pl.program_id(1)
    nj = n_ff_tiles
    pad = F32_SUBLANES
    tf = wa_ref.shape[1]
    u_refs = ((ua0, ug0), (ua1, ug1))
    act_refs = (act0, act1)

    def up_proj_part(half, k0):
        w_ref = (wa_ref, wg_ref)[half]
        return _dot(h_ref[:, k0:k0 + UP_K], w_ref[k0:k0 + UP_K, :])

    def up_proj_store(parity, half, u):
        u_ref = u_refs[parity][half]
        prev_ref, next_ref = ((pa_ref, na_ref), (pg_ref, ng_ref))[half]
        u_ref[0:pad, :] = jnp.broadcast_to(prev_ref[...], (pad, tf))
        u_ref[pad:pad + tm, :] = u
        u_ref[pad + tm:, :] = jnp.broadcast_to(next_ref[...], (pad, tf))

    def conv(u_ref, r0, cols, cw_ref, cb_ref):
        win = u_ref[r0:r0 + CONV_ROWS + 2 * pad, cols]
        mid = win[pad:pad + CONV_ROWS]
        down = pltpu.roll(win, 1, axis=0)[pad:pad + CONV_ROWS]
        up = pltpu.roll(win, CONV_ROWS + 2 * pad - 1, axis=0)[pad:pad + CONV_ROWS]
        return down * cw_ref[0:1, cols] + mid * cw_ref[1:2, cols] + up * cw_ref[2:3, cols] + cb_ref[:, cols]

    def conv_gate(parity, r0, c0):
        ua_ref, ug_ref = u_refs[parity]
        cols = slice(c0, c0 + LANES)
        a = conv(ua_ref, r0, cols, cwa_ref, cba_ref)
        g = conv(ug_ref, r0, cols, cwg_ref, cbg_ref)
        act_refs[parity][r0:r0 + CONV_ROWS, cols] = (g * jax.nn.sigmoid(g) * a).astype(BF16)

    def down_proj(parity, cols):
        return _dot(act_refs[parity][...], wd_ref[:, cols])

    chunks = [(r0, c0) for r0 in range(0, tm, CONV_ROWS) for c0 in range(0, tf, LANES)]
    d_out = wd_ref.shape[1]
    down_groups = [slice(c, c + DOWN_COLS) for c in range(0, d_out, DOWN_COLS)]

    def run_step(up=None, conv_of=None, down=None):
        pieces = []
        if down is not None:
            for cols in down_groups:
                def piece(cols=cols):
                    acc_ref[:, cols] += down_proj(down, cols)
                pieces.append((piece, tf * DOWN_COLS))
        if up is not None:
            d_in = h_ref.shape[1]
            for half in range(2):
                partial_u = []
                for k0 in range(0, d_in, UP_K):
                    def piece(half=half, k0=k0, partial_u=partial_u):
                        part = up_proj_part(half, k0)
                        partial_u[:] = [part if not partial_u else partial_u[0] + part]
                        if k0 + UP_K == d_in:
                            up_proj_store(up, half, partial_u[0])
                    pieces.append((piece, UP_K * tf))
        todo = list(chunks) if conv_of is not None else []
        total = sum(w for _, w in pieces)
        done = 0.0
        for n, (piece, w) in enumerate(pieces):
            piece()
            done += w
            upto = len(chunks) if n == len(pieces) - 1 else round(len(chunks) * done / total)
            while todo and len(chunks) - len(todo) < upto:
                conv_gate(conv_of, *todo.pop(0))

    @pl.when(j == 0)
    def _():
        acc_ref[...] = jnp.zeros_like(acc_ref)
        run_step(up=0)

    @pl.when(j == 1)
    def _():
        run_step(up=1, conv_of=0)

    for parity in range(2):
        @pl.when((j >= 2) & (j < nj) & (j % 2 == parity))
        def _():
            run_step(up=parity, conv_of=1 - parity, down=parity)

    @pl.when(j == nj)
    def _():
        run_step(conv_of=(nj - 1) % 2, down=nj % 2)

    @pl.when(j == nj + 1)
    def _():
        y = acc_ref[...] + down_proj((nj - 1) % 2, slice(None))
        o_ref[...] = x_ref[...] + gate_ref[...] * _rms(y, gpost_ref[...])


def _ffn(h, w_up, conv_w, conv_b, w_down, x, g_post, mods, row_fn, seq_len, tm, tf):
    t, d = x.shape
    ff = w_down.shape[0]
    nj = ff // tf
    n_tiles = t // tm
    assert seq_len % tm == 0

    h3 = h.reshape(n_tiles, tm, d)
    start = (np.arange(n_tiles) * tm) % seq_len == 0
    end = ((np.arange(n_tiles) + 1) * tm) % seq_len == 0
    zero = jnp.zeros((1, d), h.dtype)
    prev = jnp.concatenate([zero, h3[:-1, tm - 1]], axis=0)
    nxt = jnp.concatenate([h3[1:, 0], zero], axis=0)
    prev = jnp.where(jnp.asarray(start)[:, None], jnp.zeros_like(prev), prev)
    nxt = jnp.where(jnp.asarray(end)[:, None], jnp.zeros_like(nxt), nxt)
    edge_rows = jnp.concatenate([prev, nxt], axis=0)
    pad = -edge_rows.shape[0] % BF16_SUBLANES
    edge_rows = jnp.pad(edge_rows, ((0, pad), (0, 0)))
    u_edge = _matmul(edge_rows, w_up, 2 * ff // 8, F32, "conv_ffn_edges")
    u_edge = u_edge[:2 * n_tiles].reshape(2, n_tiles, 1, 2 * ff)

    vec = pl.BlockSpec((1, d), lambda i, j: (0, 0))
    row = pl.BlockSpec((tm, d), lambda i, j: (i, 0))
    assert nj >= 3
    clamp = lambda v: jnp.clip(v, 0, nj - 1)
    edge = lambda side, off: pl.BlockSpec((None, None, 1, tf), lambda i, j: (side, i, 0, off + clamp(j)))
    col = lambda rows, off, lag: pl.BlockSpec((rows, tf), lambda i, j: (0, off + clamp(j - lag)))
    cb = conv_b.reshape(1, 2 * ff)
    u_buf = pltpu.VMEM((tm + 2 * F32_SUBLANES, tf), F32)
    act_buf = pltpu.VMEM((tm, tf), BF16)
    return pl.pallas_call(
        functools.partial(_ffn_kernel, tm=tm, n_ff_tiles=nj),
        grid=(n_tiles, nj + 2),
        in_specs=[
            row,
            col(d, 0, 0), col(d, nj, 0),
            edge(0, 0), edge(0, nj), edge(1, 0), edge(1, nj),
            col(3, 0, 1), col(3, nj, 1), col(1, 0, 1), col(1, nj, 1),
            pl.BlockSpec((tf, d), lambda i, j: (jnp.where(j < 2, nj - 1, clamp(j - 2)), 0)),
            row,
            _mod_spec(5, row_fn),
            vec,
        ],
        out_specs=row,
        out_shape=jax.ShapeDtypeStruct((t, d), F32),
        scratch_shapes=[pltpu.VMEM((tm, d), F32), u_buf, u_buf, u_buf, u_buf, act_buf, act_buf],
        compiler_params=_params("parallel", "arbitrary"),
        name="conv_ffn",
    )(h, w_up, w_up, u_edge, u_edge, u_edge, u_edge, conv_w, conv_w, cb, cb, w_down, x, mods,
      g_post.reshape(1, d))


def kernel(x, c, ctx, c_ctx, w_ada, b_ada, g_pre_mix, w_in, rpb, w_four, w_out, g_post_mix,
           g_pre_ffn, w_up, conv_w, conv_b, w_down, g_post_ffn):
    b, n, d = x.shape
    n_ctx = ctx.shape[1]
    assert (d, n, n_ctx, b) == (D_MODEL, GRID_W * GRID_W, CTX_LEN, CTX_MOD_ROW)
    t_lat, t_ctx = b * n, b * n_ctx
    tm_lat, tm_ctx = 512, CTX_LEN
    lat_row = lambda i: (i * tm_lat) // n
    ctx_row = lambda i: CTX_MOD_ROW

    cond = jnp.zeros((MOD_ROWS, d), F32).at[:b].set(c).at[CTX_MOD_ROW].set(c_ctx)
    mods_all = _mods(cond, w_ada, b_ada)

    scale = (n * FOURIER_GROUP_DIM) ** -0.5
    scale_ctx = (n_ctx * FOURIER_GROUP_DIM) ** -0.5
    kk = np.outer(np.arange(FOURIER_GROUP_DIM), np.arange(FOURIER_GROUP_DIM)) % FOURIER_GROUP_DIM
    ang = 2.0 * np.pi * kk / FOURIER_GROUP_DIM
    cc, sc = np.cos(ang), -np.sin(ang)
    mw_lat = _four_w(jnp.asarray(cc * scale, F32), jnp.asarray(sc * scale, F32), w_four)
    dft_lat = _dft_tables(n, GRID_W)

    xt = x.reshape(t_lat, d)
    ct = ctx.reshape(t_ctx, d)
    for l in range(DEPTH):
        last = l == DEPTH - 1
        mods = mods_all[l].reshape(MOD_ROWS, N_MOD, 1, d)
        w_in_l = w_in[l].astype(BF16)
        w_out_l = w_out[l].astype(BF16)
        w_up_l = w_up[l].astype(BF16)
        w_down_l = w_down[l].astype(BF16)

        proj = _inproj(xt, g_pre_mix[l], mods, lat_row, w_in_l, tm_lat, 1024)
        if last:
            cproj = _inproj(ct, g_pre_mix[l], mods, ctx_row, w_in_l[:, NA_WIDTH:3 * NA_WIDTH], tm_ctx, 1024)
            kc_blk, vc_blk = 0, N_PAIRS
        else:
            cproj = _inproj(ct, g_pre_mix[l], mods, ctx_row, w_in_l, tm_ctx, 1024)
            kc_blk, vc_blk = N_PAIRS, 2 * N_PAIRS
        cproj3 = cproj.reshape(b, n_ctx, cproj.shape[1])

        attn = _attn(proj.reshape(b, n, PROJ_WIDTH), cproj3, _attn_bias_table(rpb[l]), kc_blk, vc_blk)
        four = _four_pos(dft_lat, _four_chan(proj, mw_lat[l], 1024), b, 512, 512)
        xt, h2 = _outproj(attn.reshape(t_lat, NA_WIDTH), four.reshape(t_lat, FOURIER_WIDTH), w_out_l, xt,
                          g_post_mix[l], g_pre_ffn[l], mods, lat_row, tm_lat)
        xt = _ffn(h2, w_up_l, conv_w[l], conv_b[l], w_down_l, xt, g_post_ffn[l], mods, lat_row, n,
                  tm_lat, 512)

        if not last:
            mw_ctx = _four_w(jnp.asarray(cc * scale_ctx, F32), jnp.asarray(sc * scale_ctx, F32),
                             w_four[l:l + 1])[0]
            dft_ctx = _dft_tables(n_ctx, 16)
            attn_c = _ctx_attn(cproj3)
            four_c = _four_pos(dft_ctx, _four_chan(cproj, mw_ctx, 512), b, n_ctx // 2, 512)
            ct, hc2 = _outproj(attn_c.reshape(t_ctx, NA_WIDTH), four_c.reshape(t_ctx, FOURIER_WIDTH),
                               w_out_l, ct, g_post_mix[l], g_pre_ffn[l], mods, ctx_row, tm_ctx)
            ct = _ffn(hc2, w_up_l, conv_w[l], conv_b[l], w_down_l, ct, g_post_ffn[l], mods, ctx_row,
                      n_ctx, tm_ctx, 512)
    return xt.reshape(b, n, d)
```

```python
import functools

import jax
import jax.numpy as jnp
import numpy as np
from jax import lax
from jax.experimental import pallas as pl
from jax.experimental.pallas import tpu as pltpu

D_MODEL = 2048
DEPTH = 2
GRID_W = 64
CTX_LEN = 256
NA_HEADS = 16
HEAD_DIM = 64
NA_WIDTH = NA_HEADS * HEAD_DIM
FOURIER_GROUPS = 8
FOURIER_GROUP_DIM = 128
FOURIER_WIDTH = FOURIER_GROUPS * FOURIER_GROUP_DIM
PROJ_WIDTH = 3 * NA_WIDTH + FOURIER_WIDTH
WIN_ROWS = 8
WIN_COLS = 16
D_FF = 5632
N_MOD = 6
EPS = 1e-6
ATTN_SCALE = HEAD_DIM ** -0.5
MOD_ROWS = 16
CTX_MOD_ROW = 8

LANES = 128
FF_TILE = 512
UP_K = 512
DOWN_COLS = 512
PROJ_COLS = 512
NORM_ROWS = 16
CONV_ROWS = 64
F32_SUBLANES = 8
BF16_SUBLANES = 16
HEAD_PAIR = LANES // HEAD_DIM
N_PAIRS = NA_HEADS // HEAD_PAIR
Q_ROWS = 4
K_ROWS = Q_ROWS + WIN_ROWS - 1
TQ = Q_ROWS * GRID_W
NK = K_ROWS * GRID_W
ATTN_UNROLL = 4
NEG = -1e30
VMEM_LIMIT = 56 * 1024 * 1024

F32 = jnp.float32
BF16 = jnp.bfloat16


def _params(*sem, flags=None):
    return pltpu.CompilerParams(dimension_semantics=sem, vmem_limit_bytes=VMEM_LIMIT, flags=flags)


def _dot(a, b):
    return jnp.dot(a, b, preferred_element_type=F32)


def _dot_nt(a, b):
    return lax.dot_general(a, b, (((1,), (1,)), ((), ())), preferred_element_type=F32)


def _rms(x, g):
    ms = jnp.mean(x * x, axis=-1, keepdims=True)
    return x * lax.rsqrt(ms + EPS) * g


def _norm_mod(x, g, shift, scale):
    return _rms(x, g) * (1.0 + scale) + shift


def _mods_kernel(cond_ref, w_ref, b_ref, o_ref):
    c = cond_ref[...]
    s = (c * jax.nn.sigmoid(c)).astype(BF16)
    o_ref[...] = _dot(s, w_ref[...].astype(BF16)) + b_ref[...]


def _mods(cond, w_ada, b_ada, tn=1024):
    depth, d, n = w_ada.shape
    return pl.pallas_call(
        _mods_kernel,
        grid=(depth, n // tn),
        in_specs=[
            pl.BlockSpec((MOD_ROWS, d), lambda l, j: (0, 0)),
            pl.BlockSpec((None, d, tn), lambda l, j: (l, 0, j)),
            pl.BlockSpec((None, 1, tn), lambda l, j: (l, 0, j)),
        ],
        out_specs=pl.BlockSpec((None, MOD_ROWS, tn), lambda l, j: (l, 0, j)),
        out_shape=jax.ShapeDtypeStruct((depth, MOD_ROWS, n), F32),
        compiler_params=_params("parallel", "parallel"),
        name="adaln_mods",
    )(cond, w_ada, b_ada.reshape(depth, 1, n))


def _mod_spec(which, row_fn):
    return pl.BlockSpec((None, None, 1, D_MODEL),
                        lambda i, *_: (row_fn(i), which, 0, 0))


def _interleave(pieces, chunks):
    if not pieces:
        for chunk in chunks:
            chunk()
        return
    per = -(-len(chunks) // len(pieces))
    for n, piece in enumerate(pieces):
        piece()
        for chunk in chunks[n * per:(n + 1) * per]:
            chunk()


def _two_stage(i, n_tiles, run):
    @pl.when(i == 0)
    def _():
        run(0, None)

    for parity in range(2):
        @pl.when((i >= 1) & (i < n_tiles) & (i % 2 == parity))
        def _():
            run(parity, 1 - parity)

    @pl.when(i == n_tiles)
    def _():
        run(None, (n_tiles - 1) % 2)


def _inproj_kernel(x_ref, g_ref, sh_ref, sc_ref, w_ref, o_ref, h0_ref, h1_ref, *, n_tiles):
    h_refs = (h0_ref, h1_ref)
    tm = x_ref.shape[0]

    def run(norm_to, dot_from):
        pieces, chunks = [], []
        if dot_from is not None:
            for c0 in range(0, w_ref.shape[1], PROJ_COLS):
                def piece(cols=slice(c0, c0 + PROJ_COLS)):
                    o_ref[:, cols] = _dot(h_refs[dot_from][...], w_ref[:, cols]).astype(o_ref.dtype)
                pieces.append(piece)
        if norm_to is not None:
            for r0 in range(0, tm, NORM_ROWS):
                def chunk(rows=slice(r0, r0 + NORM_ROWS)):
                    h_refs[norm_to][rows, :] = _norm_mod(
                        x_ref[rows, :], g_ref[...], sh_ref[...], sc_ref[...]).astype(BF16)
                chunks.append(chunk)
        _interleave(pieces, chunks)

    _two_stage(pl.program_id(0), n_tiles, run)


def _inproj(x, g, mods, row_fn, w, tm):
    t, d = x.shape
    n = w.shape[1]
    n_tiles = t // tm
    cur = lambda i: jnp.minimum(i, n_tiles - 1)
    prev = lambda i: jnp.maximum(i - 1, 0)
    return pl.pallas_call(
        functools.partial(_inproj_kernel, n_tiles=n_tiles),
        grid=(n_tiles + 1,),
        in_specs=[
            pl.BlockSpec((tm, d), lambda i: (cur(i), 0)),
            pl.BlockSpec((1, d), lambda i: (0, 0)),
            _mod_spec(0, lambda i: row_fn(cur(i))),
            _mod_spec(1, lambda i: row_fn(cur(i))),
            pl.BlockSpec((d, n), lambda i: (0, 0), pipeline_mode=pl.Buffered(1)),
        ],
        out_specs=pl.BlockSpec((tm, n), lambda i: (prev(i), 0)),
        out_shape=jax.ShapeDtypeStruct((t, n), BF16),
        scratch_shapes=[pltpu.VMEM((tm, d), BF16)] * 2,
        compiler_params=_params("arbitrary"),
        name="norm_inproj",
    )(x, g.reshape(1, d), mods, mods, w)


def _softmax_pv(parts):
    m = functools.reduce(jnp.maximum, [s.max(axis=-1, keepdims=True) for s, _ in parts])
    num = 0.0
    den = 0.0
    for s, v in parts:
        p = jnp.exp(s - m)
        den = den + p.sum(axis=-1, keepdims=True)
        num = num + _dot(p.astype(BF16), v)
    return num / den


def _head_masks():
    lane = lax.broadcasted_iota(jnp.int32, (1, LANES), 1)
    first = lane < HEAD_DIM
    return first, [first, jnp.logical_not(first)]


def _attn_kernel(q_ref, k_ref, v_ref, kc_ref, vc_ref, tab_ref, o_ref, *, n_tiles, rows):
    first, sels = _head_masks()
    kc = kc_ref[...]
    vc = vc_ref[...]

    def scores(t, hh):
        q0 = pl.multiple_of(t * TQ, TQ)
        k0 = pl.multiple_of(jnp.clip(t * Q_ROWS - WIN_ROWS // 2, 0, rows - K_ROWS) * GRID_W, GRID_W)
        pat = jnp.where(t == 0, 0, jnp.where(t == n_tiles - 1, 2, 1))
        q = q_ref[pl.ds(q0, TQ), :] * ATTN_SCALE
        qm = jnp.where(sels[hh], q, jnp.zeros_like(q))
        s_loc = _dot_nt(qm, k_ref[pl.ds(k0, NK), :]) + tab_ref[pat, hh]
        s_ctx = _dot_nt(qm, kc)
        return k0, s_loc, s_ctx

    def tile_group(tg, carry):
        units = [(tg * ATTN_UNROLL + ti, hh) for ti in range(ATTN_UNROLL) for hh in range(HEAD_PAIR)]
        pending = None
        outs = {}
        for unit in units + [None]:
            nxt = scores(*unit) if unit is not None else None
            if pending is not None:
                (t, hh), (k0, s_loc, s_ctx) = pending
                outs[hh] = _softmax_pv([(s_loc, v_ref[pl.ds(k0, NK), :]), (s_ctx, vc)])
                if hh == HEAD_PAIR - 1:
                    q0 = pl.multiple_of(t * TQ, TQ)
                    o_ref[pl.ds(q0, TQ), :] = jnp.where(first, outs[0], outs[1]).astype(o_ref.dtype)
            pending = (unit, nxt)
        return carry

    lax.fori_loop(0, n_tiles // ATTN_UNROLL, tile_group, 0)


def _attn(proj, cproj, table, kc_blk, vc_blk):
    b, n, _ = proj.shape
    rows = n // GRID_W
    n_tiles = rows // Q_ROWS
    lat = lambda off: pl.BlockSpec((None, n, LANES), lambda p, i: (i, 0, off + p))
    cx = lambda off: pl.BlockSpec((None, CTX_LEN, LANES), lambda p, i: (i, 0, off + p))
    return pl.pallas_call(
        functools.partial(_attn_kernel, n_tiles=n_tiles, rows=rows),
        grid=(N_PAIRS, b),
        in_specs=[
            lat(0), lat(N_PAIRS), lat(2 * N_PAIRS), cx(kc_blk), cx(vc_blk),
            pl.BlockSpec((3, HEAD_PAIR, TQ, NK), lambda p, i: (0, p, 0, 0)),
        ],
        out_specs=pl.BlockSpec((None, n, LANES), lambda p, i: (i, 0, p)),
        out_shape=jax.ShapeDtypeStruct((b, n, NA_WIDTH), BF16),
        compiler_params=_params("parallel", "parallel"),
        name="nbr_attn",
    )(proj, proj, proj, cproj, cproj, table)


def _ctx_attn_kernel(q_ref, k_ref, v_ref, o_ref):
    first, sels = _head_masks()
    q = q_ref[...] * ATTN_SCALE
    k = k_ref[...]
    v = v_ref[...]
    outs = []
    for hh in range(HEAD_PAIR):
        qm = jnp.where(sels[hh], q, jnp.zeros_like(q))
        outs.append(_softmax_pv([(_dot_nt(qm, k), v)]))
    o_ref[...] = jnp.where(first, outs[0], outs[1]).astype(o_ref.dtype)


def _ctx_attn(cproj):
    b, n, _ = cproj.shape
    spec = lambda off: pl.BlockSpec((None, n, LANES), lambda i, p: (i, 0, off + p))
    return pl.pallas_call(
        _ctx_attn_kernel,
        grid=(b, N_PAIRS),
        in_specs=[spec(0), spec(N_PAIRS), spec(2 * N_PAIRS)],
        out_specs=spec(0),
        out_shape=jax.ShapeDtypeStruct((b, n, NA_WIDTH), BF16),
        compiler_params=_params("parallel", "parallel"),
        name="ctx_attn",
    )(cproj, cproj, cproj)


def _attn_bias_table(rpb):
    rows = GRID_W
    qc = np.arange(GRID_W)[:, None]
    kc = np.arange(GRID_W)[None, :]
    cs = np.clip(qc - WIN_COLS // 2, 0, GRID_W - WIN_COLS)
    col_ok = (kc >= cs) & (kc < cs + WIN_COLS)
    dc = kc - qc + WIN_COLS - 1
    onehot = np.zeros((2 * WIN_COLS - 1, GRID_W, GRID_W), np.float32)
    qi, ki = np.nonzero(col_ok)
    onehot[dc[qi, ki], qi, ki] = 1.0
    t1 = jnp.einsum('hrd,dqk->hrqk', rpb, jnp.asarray(onehot), precision=lax.Precision.HIGHEST)
    t1 = jnp.where(jnp.asarray(col_ok)[None, None], t1, NEG)
    neg_slab = jnp.full((NA_HEADS, 1, GRID_W, GRID_W), NEG, F32)
    t1 = jnp.concatenate([t1, neg_slab], axis=1)
    n_tiles = rows // Q_ROWS
    idx = np.full((3, Q_ROWS, K_ROWS), 2 * WIN_ROWS - 1, np.int32)
    for p, t in enumerate((0, 1, n_tiles - 1)):
        k_start = int(np.clip(t * Q_ROWS - WIN_ROWS // 2, 0, rows - K_ROWS))
        for qr in range(Q_ROWS):
            r = t * Q_ROWS + qr
            rs = int(np.clip(r - WIN_ROWS // 2, 0, rows - WIN_ROWS))
            for kr in range(K_ROWS):
                krow = k_start + kr
                if rs <= krow < rs + WIN_ROWS:
                    idx[p, qr, kr] = krow - r + WIN_ROWS - 1
    g = jnp.take(t1, jnp.asarray(idx.reshape(-1)), axis=1)
    g = g.reshape(NA_HEADS, 3, Q_ROWS, K_ROWS, GRID_W, GRID_W)
    return g.transpose(1, 0, 2, 4, 3, 5).reshape(3, NA_HEADS, TQ, NK)


def _split_bf16(a):
    hi = a.astype(BF16)
    lo = (a - hi.astype(F32)).astype(BF16)
    return hi, lo


def _dot3(a, b):
    ah, al = _split_bf16(a)
    bh, bl = _split_bf16(b)
    return _dot(ah, bh) + (_dot(ah, bl) + _dot(al, bh))


def _four_w_kernel(cc_ref, sc_ref, w_ref, o_ref):
    w = w_ref[...]
    o_ref[:, :FOURIER_GROUP_DIM] = _dot3(cc_ref[...], w).astype(BF16)
    o_ref[:, FOURIER_GROUP_DIM:] = _dot3(sc_ref[...], w).astype(BF16)


def _four_w(cc, sc, w_four):
    depth, g, c, _ = w_four.shape
    tab = pl.BlockSpec((c, c), lambda l, i: (0, 0))
    return pl.pallas_call(
        _four_w_kernel,
        grid=(depth, g),
        in_specs=[tab, tab, pl.BlockSpec((None, None, c, c), lambda l, i: (l, i, 0, 0))],
        out_specs=pl.BlockSpec((None, None, c, 2 * c), lambda l, i: (l, i, 0, 0)),
        out_shape=jax.ShapeDtypeStruct((depth, g, c, 2 * c), BF16),
        compiler_params=_params("parallel", "parallel"),
        name="fourier_weights",
    )(cc, sc, w_four)


def _four_chan_kernel(f_ref, m_ref, yce_ref, yco_ref, yse_ref, yso_ref, yc_scr, ys_scr):
    c = FOURIER_GROUP_DIM
    half = f_ref.shape[0] // 2
    for g in range(FOURIER_GROUPS):
        cols = slice(g * c, (g + 1) * c)
        y = _dot(f_ref[:, cols], m_ref[g])
        for scr, part, even_ref, odd_ref in ((yc_scr, y[:, :c], yce_ref, yco_ref),
                                             (ys_scr, y[:, c:], yse_ref, yso_ref)):
            scr[...] = part
            even_ref[:, cols] = scr[pl.ds(0, half, stride=2), :].astype(BF16)
            odd_ref[:, cols] = scr[pl.ds(1, half, stride=2), :].astype(BF16)


def _four_chan(proj, mw, tm):
    t = proj.shape[0]
    f_blk = proj.shape[1] // FOURIER_WIDTH - 1
    out = pl.BlockSpec((tm // 2, FOURIER_WIDTH), lambda i: (i, 0))
    return pl.pallas_call(
        _four_chan_kernel,
        grid=(t // tm,),
        in_specs=[
            pl.BlockSpec((tm, FOURIER_WIDTH), lambda i: (i, f_blk)),
            pl.BlockSpec(mw.shape, lambda i: (0, 0, 0)),
        ],
        out_specs=[out] * 4,
        out_shape=[jax.ShapeDtypeStruct((t // 2, FOURIER_WIDTH), BF16)] * 4,
        scratch_shapes=[pltpu.VMEM((tm, FOURIER_GROUP_DIM), F32)] * 2,
        compiler_params=_params("parallel"),
        name="fourier_chan",
    )(proj, mw)


def _four_pos_kernel(ce_ref, se_ref, co_ref, so_ref, ae_ref, ao_ref, be_ref, bo_ref, o_ref):
    even = _dot(ce_ref[...], ae_ref[...]) + _dot(se_ref[...], be_ref[...])
    odd = _dot(co_ref[...], ao_ref[...]) + _dot(so_ref[...], bo_ref[...])
    o_ref[0] = (even + odd).astype(o_ref.dtype)
    o_ref[1] = (even - odd).astype(o_ref.dtype)


def _four_pos(tables, y_parts, b, tm, tn):
    yce, yco, yse, yso = (y.reshape(b, y.shape[0] // b, y.shape[1]) for y in y_parts)
    _, half, w = yce.shape
    mat = pl.BlockSpec((tm, half), lambda i, bb, j: (i, 0))
    dat = pl.BlockSpec((None, half, tn), lambda i, bb, j: (bb, 0, j))
    out = pl.pallas_call(
        _four_pos_kernel,
        grid=(half // tm, b, w // tn),
        in_specs=[mat, mat, mat, mat, dat, dat, dat, dat],
        out_specs=pl.BlockSpec((None, 2, tm, tn), lambda i, bb, j: (bb, 0, i, j)),
        out_shape=jax.ShapeDtypeStruct((b, 2, half, w), BF16),
        compiler_params=_params("parallel", "parallel", "parallel"),
        name="fourier_pos",
    )(*tables, yce, yco, yse, yso)
    return out.reshape(b, 2 * half, w)


def _dft_tables(n, split):
    half = n // 2
    j = jnp.arange(half, dtype=jnp.int32)[:, None]
    hi = jnp.arange(half // split, dtype=jnp.int32)[None, :] * split
    lo = jnp.arange(split, dtype=jnp.int32)[None, :]
    ang = lambda k: (2.0 * np.pi / half) * ((j * k) % half).astype(F32)
    ch, sh, cl, sl = jnp.cos(ang(hi)), jnp.sin(ang(hi)), jnp.cos(ang(lo)), jnp.sin(ang(lo))
    ce = (ch[:, :, None] * cl[:, None, :] - sh[:, :, None] * sl[:, None, :]).reshape(half, half)
    se = (sh[:, :, None] * cl[:, None, :] + ch[:, :, None] * sl[:, None, :]).reshape(half, half)
    phi = (2.0 * np.pi / n) * j.astype(F32)
    cp, sp = jnp.cos(phi), jnp.sin(phi)
    co = ce * cp - se * sp
    so = se * cp + ce * sp
    return tuple(t.astype(BF16) for t in (ce, se, co, so))


def _outproj_kernel(a_ref, f_ref, w_ref, x_ref, gpost_ref, gate_ref, gpre_ref, sh_ref, sc_ref,
                    xo_ref, h_ref):
    mix = _dot(a_ref[...], w_ref[:NA_WIDTH, :]) + _dot(f_ref[...], w_ref[NA_WIDTH:, :])
    x = x_ref[...] + gate_ref[...] * _rms(mix, gpost_ref[...])
    xo_ref[...] = x
    h_ref[...] = _norm_mod(x, gpre_ref[...], sh_ref[...], sc_ref[...]).astype(BF16)


def _outproj(attn, four, w, x, g_post, g_pre, mods, row_fn, tm):
    t, d = x.shape
    vec = pl.BlockSpec((1, d), lambda i: (0, 0))
    row = pl.BlockSpec((tm, d), lambda i: (i, 0))
    half = pl.BlockSpec((tm, NA_WIDTH), lambda i: (i, 0))
    return pl.pallas_call(
        _outproj_kernel,
        grid=(t // tm,),
        in_specs=[half, half, pl.BlockSpec(w.shape, lambda i: (0, 0), pipeline_mode=pl.Buffered(1)), row, vec,
                  _mod_spec(2, row_fn), vec, _mod_spec(3, row_fn), _mod_spec(4, row_fn)],
        out_specs=[row, row],
        out_shape=[jax.ShapeDtypeStruct((t, d), F32), jax.ShapeDtypeStruct((t, d), BF16)],
        compiler_params=_params("parallel"),
        name="outproj_residual",
    )(attn, four, w, x, g_post.reshape(1, d), mods, g_pre.reshape(1, d), mods, mods)


def _matmul_kernel(x_ref, w_ref, o_ref):
    o_ref[...] = _dot(x_ref[...], w_ref[...]).astype(o_ref.dtype)


def _matmul_tiled(x, w_tiles, out_dtype, name):
    m, k = x.shape
    nt, _, tn = w_tiles.shape
    return pl.pallas_call(
        _matmul_kernel,
        grid=(nt,),
        in_specs=[pl.BlockSpec((m, k), lambda j: (0, 0)), pl.BlockSpec((None, k, tn), lambda j: (j, 0, 0))],
        out_specs=pl.BlockSpec((m, tn), lambda j: (0, j)),
        out_shape=jax.ShapeDtypeStruct((m, nt * tn), out_dtype),
        compiler_params=_params("parallel"),
        name=name,
    )(x, w_tiles)


def _ffn_kernel(h_ref, wa_ref, wg_ref, pa_ref, pg_ref, na_ref, ng_ref, cwa_ref, cwg_ref, cba_ref, cbg_ref,
                wd_ref, x_ref, gate_ref, gpost_ref, o_ref, acc_ref, ua0, ug0, ua1, ug1, act0, act1,
                *, tm, n_ff_tiles):
    j = pl.program_id(1)
    nj = n_ff_tiles
    pad = F32_SUBLANES
    tf = wa_ref.shape[1]
    u_refs = ((ua0, ug0), (ua1, ug1))
    act_refs = (act0, act1)

    def up_proj_part(half, k0):
        w_ref = (wa_ref, wg_ref)[half]
        return _dot(h_ref[:, k0:k0 + UP_K], w_ref[k0:k0 + UP_K, :])

    def up_proj_store(parity, half, u):
        u_ref = u_refs[parity][half]
        prev_ref, next_ref = ((pa_ref, na_ref), (pg_ref, ng_ref))[half]
        u_ref[0:pad, :] = jnp.broadcast_to(prev_ref[...], (pad, tf))
        u_ref[pad:pad + tm, :] = u
        u_ref[pad + tm:, :] = jnp.broadcast_to(next_ref[...], (pad, tf))

    def conv(u_ref, r0, cols, cw_ref, cb_ref):
        win = u_ref[r0:r0 + CONV_ROWS + 2 * pad, cols]
        mid = win[pad:pad + CONV_ROWS]
        down = pltpu.roll(win, 1, axis=0)[pad:pad + CONV_ROWS]
        up = pltpu.roll(win, CONV_ROWS + 2 * pad - 1, axis=0)[pad:pad + CONV_ROWS]
        return down * cw_ref[0:1, cols] + mid * cw_ref[1:2, cols] + up * cw_ref[2:3, cols] + cb_ref[:, cols]

    def conv_gate(parity, r0, c0):
        ua_ref, ug_ref = u_refs[parity]
        cols = slice(c0, c0 + LANES)
        a = conv(ua_ref, r0, cols, cwa_ref, cba_ref)
        g = conv(ug_ref, r0, cols, cwg_ref, cbg_ref)
        act_refs[parity][r0:r0 + CONV_ROWS, cols] = (g * jax.nn.sigmoid(g) * a).astype(BF16)

    def down_proj(parity, cols):
        return _dot(act_refs[parity][...], wd_ref[:, cols])

    chunks = [(r0, c0) for r0 in range(0, tm, CONV_ROWS) for c0 in range(0, tf, LANES)]
    d_out = wd_ref.shape[1]
    down_groups = [slice(c, c + DOWN_COLS) for c in range(0, d_out, DOWN_COLS)]

    def run_step(up=None, conv_of=None, down=None):
        pieces = []
        if down is not None:
            for cols in down_groups:
                def piece(cols=cols):
                    acc_ref[:, cols] += down_proj(down, cols)
                pieces.append((piece, tf * DOWN_COLS))
        if up is not None:
            d_in = h_ref.shape[1]
            for half in range(2):
                partial_u = []
                for k0 in range(0, d_in, UP_K):
                    def piece(half=half, k0=k0, partial_u=partial_u):
                        part = up_proj_part(half, k0)
                        partial_u[:] = [part if not partial_u else partial_u[0] + part]
                        if k0 + UP_K == d_in:
                            up_proj_store(up, half, partial_u[0])
                    pieces.append((piece, UP_K * tf))
        todo = list(chunks) if conv_of is not None else []
        total = sum(w for _, w in pieces)
        done = 0.0
        for n, (piece, w) in enumerate(pieces):
            piece()
            done += w
            upto = len(chunks) if n == len(pieces) - 1 else round(len(chunks) * done / total)
            while todo and len(chunks) - len(todo) < upto:
                conv_gate(conv_of, *todo.pop(0))

    @pl.when(j == 0)
    def _():
        acc_ref[...] = jnp.zeros_like(acc_ref)
        run_step(up=0)

    @pl.when(j == 1)
    def _():
        run_step(up=1, conv_of=0)

    for parity in range(2):
        @pl.when((j >= 2) & (j < nj) & (j % 2 == parity))
        def _():
            run_step(up=parity, conv_of=1 - parity, down=parity)

    @pl.when(j == nj)
    def _():
        run_step(conv_of=(nj - 1) % 2, down=nj % 2)

    @pl.when(j == nj + 1)
    def _():
        y = acc_ref[...] + down_proj((nj - 1) % 2, slice(None))
        o_ref[...] = x_ref[...] + gate_ref[...] * _rms(y, gpost_ref[...])


def _ffn(h, w_up_tiles, conv_w, conv_b, w_down, x, g_post, mods, row_fn, seq_len, tm):
    t, d = x.shape
    ff = w_down.shape[0]
    tf = w_up_tiles.shape[2]
    nj = ff // tf
    assert w_up_tiles.shape == (2 * nj, d, tf)
    n_tiles = t // tm
    assert seq_len % tm == 0

    h3 = h.reshape(n_tiles, tm, d)
    start = (np.arange(n_tiles) * tm) % seq_len == 0
    end = ((np.arange(n_tiles) + 1) * tm) % seq_len == 0
    zero = jnp.zeros((1, d), h.dtype)
    prev = jnp.concatenate([zero, h3[:-1, tm - 1]], axis=0)
    nxt = jnp.concatenate([h3[1:, 0], zero], axis=0)
    prev = jnp.where(jnp.asarray(start)[:, None], jnp.zeros_like(prev), prev)
    nxt = jnp.where(jnp.asarray(end)[:, None], jnp.zeros_like(nxt), nxt)
    edge_rows = jnp.concatenate([prev, nxt], axis=0)
    pad = -edge_rows.shape[0] % BF16_SUBLANES
    edge_rows = jnp.pad(edge_rows, ((0, pad), (0, 0)))
    u_edge = _matmul_tiled(edge_rows, w_up_tiles, F32, "conv_ffn_edges")
    u_edge = u_edge[:2 * n_tiles].reshape(2, n_tiles, 1, 2 * ff)

    vec = pl.BlockSpec((1, d), lambda i, j: (0, 0))
    row = pl.BlockSpec((tm, d), lambda i, j: (i, 0))
    assert nj >= 3
    clamp = lambda v: jnp.clip(v, 0, nj - 1)
    edge = lambda side, off: pl.BlockSpec((None, None, 1, tf), lambda i, j: (side, i, 0, off + clamp(j)))
    col = lambda rows, off, lag: pl.BlockSpec((rows, tf), lambda i, j: (0, off + clamp(j - lag)))
    cb = conv_b.reshape(1, 2 * ff)
    u_buf = pltpu.VMEM((tm + 2 * F32_SUBLANES, tf), F32)
    act_buf = pltpu.VMEM((tm, tf), BF16)
    return pl.pallas_call(
        functools.partial(_ffn_kernel, tm=tm, n_ff_tiles=nj),
        grid=(n_tiles, nj + 2),
        in_specs=[
            row,
            pl.BlockSpec((None, d, tf), lambda i, j: (clamp(j), 0, 0)),
            pl.BlockSpec((None, d, tf), lambda i, j: (nj + clamp(j), 0, 0)),
            edge(0, 0), edge(0, nj), edge(1, 0), edge(1, nj),
            col(3, 0, 1), col(3, nj, 1), col(1, 0, 1), col(1, nj, 1),
            pl.BlockSpec((tf, d), lambda i, j: (jnp.where(j < 2, nj - 1, clamp(j - 2)), 0)),
            row,
            _mod_spec(5, row_fn),
            vec,
        ],
        out_specs=row,
        out_shape=jax.ShapeDtypeStruct((t, d), F32),
        scratch_shapes=[pltpu.VMEM((tm, d), F32), u_buf, u_buf, u_buf, u_buf, act_buf, act_buf],
        compiler_params=_params("parallel", "arbitrary"),
        name="conv_ffn",
    )(h, w_up_tiles, w_up_tiles, u_edge, u_edge, u_edge, u_edge, conv_w, conv_w, cb, cb, w_down, x, mods,
      g_post.reshape(1, d))


def kernel(x, c, ctx, c_ctx, w_ada, b_ada, g_pre_mix, w_in, rpb, w_four, w_out, g_post_mix,
           g_pre_ffn, w_up, conv_w, conv_b, w_down, g_post_ffn):
    b, n, d = x.shape
    n_ctx = ctx.shape[1]
    assert (d, n, n_ctx, b) == (D_MODEL, GRID_W * GRID_W, CTX_LEN, CTX_MOD_ROW)
    t_lat, t_ctx = b * n, b * n_ctx
    tm_lat, tm_ctx = 512, CTX_LEN
    lat_row = lambda i: (i * tm_lat) // n
    ctx_row = lambda i: CTX_MOD_ROW

    cond = jnp.zeros((MOD_ROWS, d), F32).at[:b].set(c).at[CTX_MOD_ROW].set(c_ctx)
    mods_all = _mods(cond, w_ada, b_ada)

    scale = (n * FOURIER_GROUP_DIM) ** -0.5
    scale_ctx = (n_ctx * FOURIER_GROUP_DIM) ** -0.5
    kk = np.outer(np.arange(FOURIER_GROUP_DIM), np.arange(FOURIER_GROUP_DIM)) % FOURIER_GROUP_DIM
    ang = 2.0 * np.pi * kk / FOURIER_GROUP_DIM
    cc, sc = np.cos(ang), -np.sin(ang)
    mw_lat = _four_w(jnp.asarray(cc * scale, F32), jnp.asarray(sc * scale, F32), w_four)
    dft_lat = _dft_tables(n, GRID_W)

    xt = x.reshape(t_lat, d)
    ct = ctx.reshape(t_ctx, d)
    for l in range(DEPTH):
        last = l == DEPTH - 1
        mods = mods_all[l].reshape(MOD_ROWS, N_MOD, 1, d)
        w_in_l = w_in[l].astype(BF16)
        w_out_l = w_out[l].astype(BF16)
        w_up_l = w_up[l].reshape(d, 2 * D_FF // FF_TILE, FF_TILE).transpose(1, 0, 2).astype(BF16)
        w_down_l = w_down[l].astype(BF16)

        proj = _inproj(xt, g_pre_mix[l], mods, lat_row, w_in_l, tm_lat)
        if last:
            cproj = _inproj(ct, g_pre_mix[l], mods, ctx_row, w_in_l[:, NA_WIDTH:3 * NA_WIDTH], tm_ctx)
            kc_blk, vc_blk = 0, N_PAIRS
        else:
            cproj = _inproj(ct, g_pre_mix[l], mods, ctx_row, w_in_l, tm_ctx)
            kc_blk, vc_blk = N_PAIRS, 2 * N_PAIRS
        cproj3 = cproj.reshape(b, n_ctx, cproj.shape[1])

        attn = _attn(proj.reshape(b, n, PROJ_WIDTH), cproj3, _attn_bias_table(rpb[l]), kc_blk, vc_blk)
        four = _four_pos(dft_lat, _four_chan(proj, mw_lat[l], 1024), b, 512, 512)
        xt, h2 = _outproj(attn.reshape(t_lat, NA_WIDTH), four.reshape(t_lat, FOURIER_WIDTH), w_out_l, xt,
                          g_post_mix[l], g_pre_ffn[l], mods, lat_row, tm_lat)
        xt = _ffn(h2, w_up_l, conv_w[l], conv_b[l], w_down_l, xt, g_post_ffn[l], mods, lat_row, n, tm_lat)

        if not last:
            mw_ctx = _four_w(jnp.asarray(cc * scale_ctx, F32), jnp.asarray(sc * scale_ctx, F32),
                             w_four[l:l + 1])[0]
            dft_ctx = _dft_tables(n_ctx, 16)
            attn_c = _ctx_attn(cproj3)
            four_c = _four_pos(dft_ctx, _four_chan(cproj, mw_ctx, 512), b, n_ctx // 2, 512)
            ct, hc2 = _outproj(attn_c.reshape(t_ctx, NA_WIDTH), four_c.reshape(t_ctx, FOURIER_WIDTH),
                               w_out_l, ct, g_post_mix[l], g_pre_ffn[l], mods, ctx_row, tm_ctx)
            ct = _ffn(hc2, w_up_l, conv_w[l], conv_b[l], w_down_l, ct, g_post_ffn[l], mods, ctx_row,
                      n_ctx, tm_ctx)
    return xt.reshape(b, n, d)
```

```python
import functools

import jax
import jax.numpy as jnp
import numpy as np
from jax import lax
from jax.experimental import pallas as pl
from jax.experimental.pallas import tpu as pltpu

D_MODEL = 2048
DEPTH = 2
GRID_W = 64
CTX_LEN = 256
NA_HEADS = 16
HEAD_DIM = 64
NA_WIDTH = NA_HEADS * HEAD_DIM
FOURIER_GROUPS = 8
FOURIER_GROUP_DIM = 128
FOURIER_WIDTH = FOURIER_GROUPS * FOURIER_GROUP_DIM
PROJ_WIDTH = 3 * NA_WIDTH + FOURIER_WIDTH
WIN_ROWS = 8
WIN_COLS = 16
D_FF = 5632
N_MOD = 6
EPS = 1e-6
ATTN_SCALE = HEAD_DIM ** -0.5
MOD_ROWS = 16
CTX_MOD_ROW = 8

LANES = 128
DFT_ROWS = 512
FF_TILE = 512
UP_K = 512
DOWN_COLS = 512
PROJ_COLS = 512
NORM_ROWS = 16
CONV_ROWS = 64
F32_SUBLANES = 8
BF16_SUBLANES = 16
HEAD_PAIR = LANES // HEAD_DIM
N_PAIRS = NA_HEADS // HEAD_PAIR
Q_ROWS = 4
K_ROWS = Q_ROWS + WIN_ROWS - 1
TQ = Q_ROWS * GRID_W
NK = K_ROWS * GRID_W
ATTN_UNROLL = 4
NEG = -1e30
VMEM_LIMIT = 56 * 1024 * 1024

F32 = jnp.float32
BF16 = jnp.bfloat16


def _params(*sem, flags=None):
    return pltpu.CompilerParams(dimension_semantics=sem, vmem_limit_bytes=VMEM_LIMIT, flags=flags)


def _dot(a, b):
    return jnp.dot(a, b, preferred_element_type=F32)


def _dot_nt(a, b):
    return lax.dot_general(a, b, (((1,), (1,)), ((), ())), preferred_element_type=F32)


def _rms(x, g):
    ms = jnp.mean(x * x, axis=-1, keepdims=True)
    return x * lax.rsqrt(ms + EPS) * g


def _norm_mod(x, g, shift, scale):
    return _rms(x, g) * (1.0 + scale) + shift


def _mods_kernel(cond_ref, w_ref, b_ref, o_ref):
    c = cond_ref[...]
    s = (c * jax.nn.sigmoid(c)).astype(BF16)
    o_ref[...] = _dot(s, w_ref[...].astype(BF16)) + b_ref[...]


def _mods(cond, w_ada, b_ada, tn=1024):
    depth, d, n = w_ada.shape
    return pl.pallas_call(
        _mods_kernel,
        grid=(depth, n // tn),
        in_specs=[
            pl.BlockSpec((MOD_ROWS, d), lambda l, j: (0, 0)),
            pl.BlockSpec((None, d, tn), lambda l, j: (l, 0, j)),
            pl.BlockSpec((None, 1, tn), lambda l, j: (l, 0, j)),
        ],
        out_specs=pl.BlockSpec((None, MOD_ROWS, tn), lambda l, j: (l, 0, j)),
        out_shape=jax.ShapeDtypeStruct((depth, MOD_ROWS, n), F32),
        compiler_params=_params("parallel", "parallel"),
        name="adaln_mods",
    )(cond, w_ada, b_ada.reshape(depth, 1, n))


def _mod_spec(which, row_fn):
    return pl.BlockSpec((None, None, 1, D_MODEL),
                        lambda i, *_: (row_fn(i), which, 0, 0))


def _interleave(pieces, chunks):
    if not pieces:
        for chunk in chunks:
            chunk()
        return
    per = -(-len(chunks) // len(pieces))
    for n, piece in enumerate(pieces):
        piece()
        for chunk in chunks[n * per:(n + 1) * per]:
            chunk()


def _two_stage(i, n_tiles, run):
    @pl.when(i == 0)
    def _():
        run(0, None)

    for parity in range(2):
        @pl.when((i >= 1) & (i < n_tiles) & (i % 2 == parity))
        def _():
            run(parity, 1 - parity)

    @pl.when(i == n_tiles)
    def _():
        run(None, (n_tiles - 1) % 2)


def _inproj_kernel(x_ref, g_ref, sh_ref, sc_ref, w_ref, o_ref, h0_ref, h1_ref, *, n_tiles):
    h_refs = (h0_ref, h1_ref)
    tm = x_ref.shape[0]

    def run(norm_to, dot_from):
        pieces, chunks = [], []
        if dot_from is not None:
            for c0 in range(0, w_ref.shape[1], PROJ_COLS):
                def piece(cols=slice(c0, c0 + PROJ_COLS)):
                    o_ref[:, cols] = _dot(h_refs[dot_from][...], w_ref[:, cols]).astype(o_ref.dtype)
                pieces.append(piece)
        if norm_to is not None:
            for r0 in range(0, tm, NORM_ROWS):
                def chunk(rows=slice(r0, r0 + NORM_ROWS)):
                    h_refs[norm_to][rows, :] = _norm_mod(
                        x_ref[rows, :], g_ref[...], sh_ref[...], sc_ref[...]).astype(BF16)
                chunks.append(chunk)
        _interleave(pieces, chunks)

    _two_stage(pl.program_id(0), n_tiles, run)


def _inproj(x, g, mods, row_fn, w, tm):
    t, d = x.shape
    n = w.shape[1]
    n_tiles = t // tm
    cur = lambda i: jnp.minimum(i, n_tiles - 1)
    prev = lambda i: jnp.maximum(i - 1, 0)
    return pl.pallas_call(
        functools.partial(_inproj_kernel, n_tiles=n_tiles),
        grid=(n_tiles + 1,),
        in_specs=[
            pl.BlockSpec((tm, d), lambda i: (cur(i), 0)),
            pl.BlockSpec((1, d), lambda i: (0, 0)),
            _mod_spec(0, lambda i: row_fn(cur(i))),
            _mod_spec(1, lambda i: row_fn(cur(i))),
            pl.BlockSpec((d, n), lambda i: (0, 0), pipeline_mode=pl.Buffered(1)),
        ],
        out_specs=pl.BlockSpec((tm, n), lambda i: (prev(i), 0)),
        out_shape=jax.ShapeDtypeStruct((t, n), BF16),
        scratch_shapes=[pltpu.VMEM((tm, d), BF16)] * 2,
        compiler_params=_params("arbitrary"),
        name="norm_inproj",
    )(x, g.reshape(1, d), mods, mods, w)


def _softmax_pv(parts):
    m = functools.reduce(jnp.maximum, [s.max(axis=-1, keepdims=True) for s, _ in parts])
    num = 0.0
    den = 0.0
    for s, v in parts:
        p = jnp.exp(s - m)
        den = den + p.sum(axis=-1, keepdims=True)
        num = num + _dot(p.astype(BF16), v)
    return num / den


def _head_masks():
    lane = lax.broadcasted_iota(jnp.int32, (1, LANES), 1)
    first = lane < HEAD_DIM
    return first, [first, jnp.logical_not(first)]


def _attn_kernel(q_ref, k_ref, v_ref, kc_ref, vc_ref, tab_ref, o_ref, *, n_tiles, rows):
    first, sels = _head_masks()
    kc = kc_ref[...]
    vc = vc_ref[...]

    def scores(t, hh):
        q0 = pl.multiple_of(t * TQ, TQ)
        k0 = pl.multiple_of(jnp.clip(t * Q_ROWS - WIN_ROWS // 2, 0, rows - K_ROWS) * GRID_W, GRID_W)
        pat = jnp.where(t == 0, 0, jnp.where(t == n_tiles - 1, 2, 1))
        q = q_ref[pl.ds(q0, TQ), :] * ATTN_SCALE
        qm = jnp.where(sels[hh], q, jnp.zeros_like(q))
        s_loc = _dot_nt(qm, k_ref[pl.ds(k0, NK), :]) + tab_ref[pat, hh]
        s_ctx = _dot_nt(qm, kc)
        return k0, s_loc, s_ctx

    def tile_group(tg, carry):
        units = [(tg * ATTN_UNROLL + ti, hh) for ti in range(ATTN_UNROLL) for hh in range(HEAD_PAIR)]
        pending = None
        outs = {}
        for unit in units + [None]:
            nxt = scores(*unit) if unit is not None else None
            if pending is not None:
                (t, hh), (k0, s_loc, s_ctx) = pending
                outs[hh] = _softmax_pv([(s_loc, v_ref[pl.ds(k0, NK), :]), (s_ctx, vc)])
                if hh == HEAD_PAIR - 1:
                    q0 = pl.multiple_of(t * TQ, TQ)
                    o_ref[pl.ds(q0, TQ), :] = jnp.where(first, outs[0], outs[1]).astype(o_ref.dtype)
            pending = (unit, nxt)
        return carry

    lax.fori_loop(0, n_tiles // ATTN_UNROLL, tile_group, 0)


def _attn(proj, cproj, table, kc_blk, vc_blk):
    b, n, _ = proj.shape
    rows = n // GRID_W
    n_tiles = rows // Q_ROWS
    lat = lambda off: pl.BlockSpec((None, n, LANES), lambda p, i: (i, 0, off + p))
    cx = lambda off: pl.BlockSpec((None, CTX_LEN, LANES), lambda p, i: (i, 0, off + p))
    return pl.pallas_call(
        functools.partial(_attn_kernel, n_tiles=n_tiles, rows=rows),
        grid=(N_PAIRS, b),
        in_specs=[
            lat(0), lat(N_PAIRS), lat(2 * N_PAIRS), cx(kc_blk), cx(vc_blk),
            pl.BlockSpec((3, HEAD_PAIR, TQ, NK), lambda p, i: (0, p, 0, 0)),
        ],
        out_specs=pl.BlockSpec((None, n, LANES), lambda p, i: (i, 0, p)),
        out_shape=jax.ShapeDtypeStruct((b, n, NA_WIDTH), BF16),
        compiler_params=_params("parallel", "parallel"),
        name="nbr_attn",
    )(proj, proj, proj, cproj, cproj, table)


def _ctx_attn_kernel(q_ref, k_ref, v_ref, o_ref):
    first, sels = _head_masks()
    q = q_ref[...] * ATTN_SCALE
    k = k_ref[...]
    v = v_ref[...]
    outs = []
    for hh in range(HEAD_PAIR):
        qm = jnp.where(sels[hh], q, jnp.zeros_like(q))
        outs.append(_softmax_pv([(_dot_nt(qm, k), v)]))
    o_ref[...] = jnp.where(first, outs[0], outs[1]).astype(o_ref.dtype)


def _ctx_attn(cproj):
    b, n, _ = cproj.shape
    spec = lambda off: pl.BlockSpec((None, n, LANES), lambda i, p: (i, 0, off + p))
    return pl.pallas_call(
        _ctx_attn_kernel,
        grid=(b, N_PAIRS),
        in_specs=[spec(0), spec(N_PAIRS), spec(2 * N_PAIRS)],
        out_specs=spec(0),
        out_shape=jax.ShapeDtypeStruct((b, n, NA_WIDTH), BF16),
        compiler_params=_params("parallel", "parallel"),
        name="ctx_attn",
    )(cproj, cproj, cproj)


def _attn_bias_table(rpb):
    rows = GRID_W
    qc = np.arange(GRID_W)[:, None]
    kc = np.arange(GRID_W)[None, :]
    cs = np.clip(qc - WIN_COLS // 2, 0, GRID_W - WIN_COLS)
    col_ok = (kc >= cs) & (kc < cs + WIN_COLS)
    dc = kc - qc + WIN_COLS - 1
    onehot = np.zeros((2 * WIN_COLS - 1, GRID_W, GRID_W), np.float32)
    qi, ki = np.nonzero(col_ok)
    onehot[dc[qi, ki], qi, ki] = 1.0
    t1 = jnp.einsum('hrd,dqk->hrqk', rpb, jnp.asarray(onehot), precision=lax.Precision.HIGHEST)
    t1 = jnp.where(jnp.asarray(col_ok)[None, None], t1, NEG)
    neg_slab = jnp.full((NA_HEADS, 1, GRID_W, GRID_W), NEG, F32)
    t1 = jnp.concatenate([t1, neg_slab], axis=1)
    n_tiles = rows // Q_ROWS
    idx = np.full((3, Q_ROWS, K_ROWS), 2 * WIN_ROWS - 1, np.int32)
    for p, t in enumerate((0, 1, n_tiles - 1)):
        k_start = int(np.clip(t * Q_ROWS - WIN_ROWS // 2, 0, rows - K_ROWS))
        for qr in range(Q_ROWS):
            r = t * Q_ROWS + qr
            rs = int(np.clip(r - WIN_ROWS // 2, 0, rows - WIN_ROWS))
            for kr in range(K_ROWS):
                krow = k_start + kr
                if rs <= krow < rs + WIN_ROWS:
                    idx[p, qr, kr] = krow - r + WIN_ROWS - 1
    g = jnp.take(t1, jnp.asarray(idx.reshape(-1)), axis=1)
    g = g.reshape(NA_HEADS, 3, Q_ROWS, K_ROWS, GRID_W, GRID_W)
    return g.transpose(1, 0, 2, 4, 3, 5).reshape(3, NA_HEADS, TQ, NK)


def _split_bf16(a):
    hi = a.astype(BF16)
    lo = (a - hi.astype(F32)).astype(BF16)
    return hi, lo


def _dot3(a, b):
    ah, al = _split_bf16(a)
    bh, bl = _split_bf16(b)
    return _dot(ah, bh) + (_dot(ah, bl) + _dot(al, bh))


def _four_w_kernel(cc_ref, sc_ref, w_ref, o_ref):
    w = w_ref[...]
    o_ref[:, :FOURIER_GROUP_DIM] = _dot3(cc_ref[...], w).astype(BF16)
    o_ref[:, FOURIER_GROUP_DIM:] = _dot3(sc_ref[...], w).astype(BF16)


def _four_w(cc, sc, w_four):
    depth, g, c, _ = w_four.shape
    tab = pl.BlockSpec((c, c), lambda l, i: (0, 0))
    return pl.pallas_call(
        _four_w_kernel,
        grid=(depth, g),
        in_specs=[tab, tab, pl.BlockSpec((None, None, c, c), lambda l, i: (l, i, 0, 0))],
        out_specs=pl.BlockSpec((None, None, c, 2 * c), lambda l, i: (l, i, 0, 0)),
        out_shape=jax.ShapeDtypeStruct((depth, g, c, 2 * c), BF16),
        compiler_params=_params("parallel", "parallel"),
        name="fourier_weights",
    )(cc, sc, w_four)


def _four_chan_kernel(f_ref, m_ref, yce_ref, yco_ref, yse_ref, yso_ref, yc_scr, ys_scr):
    c = FOURIER_GROUP_DIM
    half = f_ref.shape[0] // 2
    for g in range(FOURIER_GROUPS):
        cols = slice(g * c, (g + 1) * c)
        y = _dot(f_ref[:, cols], m_ref[g])
        for scr, part, even_ref, odd_ref in ((yc_scr, y[:, :c], yce_ref, yco_ref),
                                             (ys_scr, y[:, c:], yse_ref, yso_ref)):
            scr[...] = part
            even_ref[:, cols] = scr[pl.ds(0, half, stride=2), :].astype(BF16)
            odd_ref[:, cols] = scr[pl.ds(1, half, stride=2), :].astype(BF16)


def _four_chan(proj, mw, tm):
    t = proj.shape[0]
    f_blk = proj.shape[1] // FOURIER_WIDTH - 1
    out = pl.BlockSpec((tm // 2, FOURIER_WIDTH), lambda i: (i, 0))
    return pl.pallas_call(
        _four_chan_kernel,
        grid=(t // tm,),
        in_specs=[
            pl.BlockSpec((tm, FOURIER_WIDTH), lambda i: (i, f_blk)),
            pl.BlockSpec(mw.shape, lambda i: (0, 0, 0)),
        ],
        out_specs=[out] * 4,
        out_shape=[jax.ShapeDtypeStruct((t // 2, FOURIER_WIDTH), BF16)] * 4,
        scratch_shapes=[pltpu.VMEM((tm, FOURIER_GROUP_DIM), F32)] * 2,
        compiler_params=_params("parallel"),
        name="fourier_chan",
    )(proj, mw)


def _four_pos_kernel(ce_ref, se_ref, co_ref, so_ref, ae_ref, ao_ref, be_ref, bo_ref, o_ref):
    step = min(DFT_ROWS, ce_ref.shape[0])
    for r0 in range(0, ce_ref.shape[0], step):
        rows = slice(r0, r0 + step)
        even = _dot(ce_ref[rows, :], ae_ref[...]) + _dot(se_ref[rows, :], be_ref[...])
        odd = _dot(co_ref[rows, :], ao_ref[...]) + _dot(so_ref[rows, :], bo_ref[...])
        o_ref[0, rows, :] = (even + odd).astype(o_ref.dtype)
        o_ref[1, rows, :] = (even - odd).astype(o_ref.dtype)


def _four_pos(tables, y_parts, b, tm, tn):
    yce, yco, yse, yso = (y.reshape(b, y.shape[0] // b, y.shape[1]) for y in y_parts)
    _, half, w = yce.shape
    mat = pl.BlockSpec((tm, half), lambda i, bb, j: (i, 0), pipeline_mode=pl.Buffered(1))
    dat = pl.BlockSpec((None, half, tn), lambda i, bb, j: (bb, 0, j))
    out = pl.pallas_call(
        _four_pos_kernel,
        grid=(half // tm, b, w // tn),
        in_specs=[mat, mat, mat, mat, dat, dat, dat, dat],
        out_specs=pl.BlockSpec((None, 2, tm, tn), lambda i, bb, j: (bb, 0, i, j)),
        out_shape=jax.ShapeDtypeStruct((b, 2, half, w), BF16),
        compiler_params=_params("parallel", "parallel", "parallel"),
        name="fourier_pos",
    )(*tables, yce, yco, yse, yso)
    return out.reshape(b, 2 * half, w)


def _dft_tables(n, split):
    half = n // 2
    j = jnp.arange(half, dtype=jnp.int32)[:, None]
    hi = jnp.arange(half // split, dtype=jnp.int32)[None, :] * split
    lo = jnp.arange(split, dtype=jnp.int32)[None, :]
    ang = lambda k: (2.0 * np.pi / half) * ((j * k) % half).astype(F32)
    ch, sh, cl, sl = jnp.cos(ang(hi)), jnp.sin(ang(hi)), jnp.cos(ang(lo)), jnp.sin(ang(lo))
    ce = (ch[:, :, None] * cl[:, None, :] - sh[:, :, None] * sl[:, None, :]).reshape(half, half)
    se = (sh[:, :, None] * cl[:, None, :] + ch[:, :, None] * sl[:, None, :]).reshape(half, half)
    phi = (2.0 * np.pi / n) * j.astype(F32)
    cp, sp = jnp.cos(phi), jnp.sin(phi)
    co = ce * cp - se * sp
    so = se * cp + ce * sp
    return tuple(t.astype(BF16) for t in (ce, se, co, so))


def _outproj_kernel(a_ref, f_ref, w_ref, x_ref, gpost_ref, gate_ref, gpre_ref, sh_ref, sc_ref,
                    xo_ref, h_ref):
    mix = _dot(a_ref[...], w_ref[:NA_WIDTH, :]) + _dot(f_ref[...], w_ref[NA_WIDTH:, :])
    x = x_ref[...] + gate_ref[...] * _rms(mix, gpost_ref[...])
    xo_ref[...] = x
    h_ref[...] = _norm_mod(x, gpre_ref[...], sh_ref[...], sc_ref[...]).astype(BF16)


def _outproj(attn, four, w, x, g_post, g_pre, mods, row_fn, tm):
    t, d = x.shape
    vec = pl.BlockSpec((1, d), lambda i: (0, 0))
    row = pl.BlockSpec((tm, d), lambda i: (i, 0))
    half = pl.BlockSpec((tm, NA_WIDTH), lambda i: (i, 0))
    return pl.pallas_call(
        _outproj_kernel,
        grid=(t // tm,),
        in_specs=[half, half, pl.BlockSpec(w.shape, lambda i: (0, 0), pipeline_mode=pl.Buffered(1)), row, vec,
                  _mod_spec(2, row_fn), vec, _mod_spec(3, row_fn), _mod_spec(4, row_fn)],
        out_specs=[row, row],
        out_shape=[jax.ShapeDtypeStruct((t, d), F32), jax.ShapeDtypeStruct((t, d), BF16)],
        compiler_params=_params("parallel"),
        name="outproj_residual",
    )(attn, four, w, x, g_post.reshape(1, d), mods, g_pre.reshape(1, d), mods, mods)


def _matmul_kernel(x_ref, w_ref, o_ref):
    o_ref[...] = _dot(x_ref[...], w_ref[...]).astype(o_ref.dtype)


def _matmul(x, w, tn, out_dtype, name):
    m, k = x.shape
    n = w.shape[1]
    return pl.pallas_call(
        _matmul_kernel,
        grid=(n // tn,),
        in_specs=[pl.BlockSpec((m, k), lambda j: (0, 0)), pl.BlockSpec((k, tn), lambda j: (0, j))],
        out_specs=pl.BlockSpec((m, tn), lambda j: (0, j)),
        out_shape=jax.ShapeDtypeStruct((m, n), out_dtype),
        compiler_params=_params("parallel"),
        name=name,
    )(x, w)


def _ffn_kernel(h_ref, wa_ref, wg_ref, pa_ref, pg_ref, na_ref, ng_ref, cwa_ref, cwg_ref, cba_ref, cbg_ref,
                wd_ref, x_ref, gate_ref, gpost_ref, o_ref, acc_ref, ua0, ug0, ua1, ug1, act0, act1,
                *, tm, n_ff_tiles, n_row_tiles):
    i = pl.program_id(0)
    j = pl.program_id(1)
    nj = n_ff_tiles
    pad = F32_SUBLANES
    tf = wa_ref.shape[1]
    u_refs = ((ua0, ug0), (ua1, ug1))
    act_refs = (act0, act1)

    def up_proj_part(half, k0):
        w_ref = (wa_ref, wg_ref)[half]
        return _dot(h_ref[:, k0:k0 + UP_K], w_ref[k0:k0 + UP_K, :])

    def up_proj_store(parity, half, u):
        u_ref = u_refs[parity][half]
        prev_ref, next_ref = ((pa_ref, na_ref), (pg_ref, ng_ref))[half]
        u_ref[0:pad, :] = jnp.broadcast_to(prev_ref[...], (pad, tf))
        u_ref[pad:pad + tm, :] = u
        u_ref[pad + tm:, :] = jnp.broadcast_to(next_ref[...], (pad, tf))

    def conv(u_ref, r0, cols, cw_ref, cb_ref):
        win = u_ref[r0:r0 + CONV_ROWS + 2 * pad, cols]
        mid = win[pad:pad + CONV_ROWS]
        down = pltpu.roll(win, 1, axis=0)[pad:pad + CONV_ROWS]
        up = pltpu.roll(win, CONV_ROWS + 2 * pad - 1, axis=0)[pad:pad + CONV_ROWS]
        return down * cw_ref[0:1, cols] + mid * cw_ref[1:2, cols] + up * cw_ref[2:3, cols] + cb_ref[:, cols]

    def conv_gate(parity, r0, c0):
        ua_ref, ug_ref = u_refs[parity]
        cols = slice(c0, c0 + LANES)
        a = conv(ua_ref, r0, cols, cwa_ref, cba_ref)
        g = conv(ug_ref, r0, cols, cwg_ref, cbg_ref)
        act_refs[parity][r0:r0 + CONV_ROWS, cols] = (g * jax.nn.sigmoid(g) * a).astype(BF16)

    def down_proj(parity, cols):
        return _dot(act_refs[parity][...], wd_ref[:, cols])

    chunks = [(r0, c0) for r0 in range(0, tm, CONV_ROWS) for c0 in range(0, tf, LANES)]
    d_out = wd_ref.shape[1]
    down_groups = [slice(c, c + DOWN_COLS) for c in range(0, d_out, DOWN_COLS)]

    def run_step(up=None, conv_of=None, down=None):
        pieces = []
        if down is not None:
            for cols in down_groups:
                def piece(cols=cols):
                    acc_ref[:, cols] += down_proj(down, cols)
                pieces.append((piece, tf * DOWN_COLS))
        if up is not None:
            d_in = h_ref.shape[1]
            for half in range(2):
                partial_u = []
                for k0 in range(0, d_in, UP_K):
                    def piece(half=half, k0=k0, partial_u=partial_u):
                        part = up_proj_part(half, k0)
                        partial_u[:] = [part if not partial_u else partial_u[0] + part]
                        if k0 + UP_K == d_in:
                            up_proj_store(up, half, partial_u[0])
                    pieces.append((piece, UP_K * tf))
        todo = list(chunks) if conv_of is not None else []
        total = sum(w for _, w in pieces)
        done = 0.0
        for n, (piece, w) in enumerate(pieces):
            piece()
            done += w
            upto = len(chunks) if n == len(pieces) - 1 else round(len(chunks) * done / total)
            while todo and len(chunks) - len(todo) < upto:
                conv_gate(conv_of, *todo.pop(0))

    s = i * nj + j
    last = n_row_tiles * nj

    @pl.when(s == 0)
    def _():
        acc_ref[...] = jnp.zeros_like(acc_ref)
        run_step(up=0)

    @pl.when(s == 1)
    def _():
        run_step(up=1, conv_of=0)

    for parity in range(2):
        @pl.when((s >= 2) & (s < last) & (s % 2 == parity))
        def _():
            run_step(up=parity, conv_of=1 - parity, down=parity)

    @pl.when(s == last)
    def _():
        run_step(conv_of=(last - 1) % 2, down=last % 2)

    @pl.when(s == last + 1)
    def _():
        run_step(down=(last + 1) % 2)

    @pl.when((j == 1) & (i >= 1))
    def _():
        o_ref[...] = x_ref[...] + gate_ref[...] * _rms(acc_ref[...], gpost_ref[...])
        acc_ref[...] = jnp.zeros_like(acc_ref)


def _ffn(h, w_up, conv_w, conv_b, w_down, x, g_post, mods, row_fn, seq_len, tm):
    t, d = x.shape
    ff = w_down.shape[0]
    tf = FF_TILE
    nj = ff // tf
    n_tiles = t // tm
    assert seq_len % tm == 0

    h3 = h.reshape(n_tiles, tm, d)
    start = (np.arange(n_tiles) * tm) % seq_len == 0
    end = ((np.arange(n_tiles) + 1) * tm) % seq_len == 0
    zero = jnp.zeros((1, d), h.dtype)
    prev = jnp.concatenate([zero, h3[:-1, tm - 1]], axis=0)
    nxt = jnp.concatenate([h3[1:, 0], zero], axis=0)
    prev = jnp.where(jnp.asarray(start)[:, None], jnp.zeros_like(prev), prev)
    nxt = jnp.where(jnp.asarray(end)[:, None], jnp.zeros_like(nxt), nxt)
    edge_rows = jnp.concatenate([prev, nxt], axis=0)
    pad = -edge_rows.shape[0] % BF16_SUBLANES
    edge_rows = jnp.pad(edge_rows, ((0, pad), (0, 0)))
    u_edge = _matmul(edge_rows, w_up, 2 * ff // 8, F32, "conv_ffn_edges")
    u_edge = u_edge[:2 * n_tiles].reshape(2, n_tiles, 1, 2 * ff)

    assert nj >= 3
    vec = pl.BlockSpec((1, d), lambda i, j: (0, 0))
    cur = lambda i: jnp.minimum(i, n_tiles - 1)
    fin = lambda i, j, switch: jnp.where(j <= switch, jnp.maximum(i - 1, 0), cur(i))
    lag = lambda j, k: (j + nj - k) % nj
    edge = lambda side, off: pl.BlockSpec((None, None, 1, tf), lambda i, j: (side, cur(i), 0, off + j))
    col = lambda rows, off, k: pl.BlockSpec((rows, tf), lambda i, j: (0, off + lag(j, k)))
    cb = conv_b.reshape(1, 2 * ff)
    u_buf = pltpu.VMEM((tm + 2 * F32_SUBLANES, tf), F32)
    act_buf = pltpu.VMEM((tm, tf), BF16)
    return pl.pallas_call(
        functools.partial(_ffn_kernel, tm=tm, n_ff_tiles=nj, n_row_tiles=n_tiles),
        grid=(n_tiles + 1, nj),
        in_specs=[
            pl.BlockSpec((tm, d), lambda i, j: (cur(i), 0)),
            col(d, 0, 0), col(d, nj, 0),
            edge(0, 0), edge(0, nj), edge(1, 0), edge(1, nj),
            col(3, 0, 1), col(3, nj, 1), col(1, 0, 1), col(1, nj, 1),
            pl.BlockSpec((tf, d), lambda i, j: (lag(j, 2), 0)),
            pl.BlockSpec((tm, d), lambda i, j: (fin(i, j, 1), 0)),
            pl.BlockSpec((None, None, 1, d), lambda i, j: (row_fn(fin(i, j, 1)), 5, 0, 0)),
            vec,
        ],
        out_specs=pl.BlockSpec((tm, d), lambda i, j: (fin(i, j, 3), 0)),
        out_shape=jax.ShapeDtypeStruct((t, d), F32),
        scratch_shapes=[pltpu.VMEM((tm, d), F32), u_buf, u_buf, u_buf, u_buf, act_buf, act_buf],
        compiler_params=_params("arbitrary", "arbitrary"),
        name="conv_ffn",
    )(h, w_up, w_up, u_edge, u_edge, u_edge, u_edge, conv_w, conv_w, cb, cb, w_down, x, mods,
      g_post.reshape(1, d))


def kernel(x, c, ctx, c_ctx, w_ada, b_ada, g_pre_mix, w_in, rpb, w_four, w_out, g_post_mix,
           g_pre_ffn, w_up, conv_w, conv_b, w_down, g_post_ffn):
    b, n, d = x.shape
    n_ctx = ctx.shape[1]
    assert (d, n, n_ctx, b) == (D_MODEL, GRID_W * GRID_W, CTX_LEN, CTX_MOD_ROW)
    t_lat, t_ctx = b * n, b * n_ctx
    tm_lat, tm_ctx = 512, CTX_LEN
    lat_row = lambda i: (i * tm_lat) // n
    ctx_row = lambda i: CTX_MOD_ROW

    cond = jnp.zeros((MOD_ROWS, d), F32).at[:b].set(c).at[CTX_MOD_ROW].set(c_ctx)
    mods_all = _mods(cond, w_ada, b_ada)

    scale = (n * FOURIER_GROUP_DIM) ** -0.5
    scale_ctx = (n_ctx * FOURIER_GROUP_DIM) ** -0.5
    kk = np.outer(np.arange(FOURIER_GROUP_DIM), np.arange(FOURIER_GROUP_DIM)) % FOURIER_GROUP_DIM
    ang = 2.0 * np.pi * kk / FOURIER_GROUP_DIM
    cc, sc = np.cos(ang), -np.sin(ang)
    mw_lat = _four_w(jnp.asarray(cc * scale, F32), jnp.asarray(sc * scale, F32), w_four)
    dft_lat = _dft_tables(n, GRID_W)

    xt = x.reshape(t_lat, d)
    ct = ctx.reshape(t_ctx, d)
    for l in range(DEPTH):
        last = l == DEPTH - 1
        mods = mods_all[l].reshape(MOD_ROWS, N_MOD, 1, d)
        w_in_l = w_in[l].astype(BF16)
        w_out_l = w_out[l].astype(BF16)
        w_up_l = w_up[l].astype(BF16)
        w_down_l = w_down[l].astype(BF16)

        proj = _inproj(xt, g_pre_mix[l], mods, lat_row, w_in_l, tm_lat)
        if last:
            cproj = _inproj(ct, g_pre_mix[l], mods, ctx_row, w_in_l[:, NA_WIDTH:3 * NA_WIDTH], tm_ctx)
            kc_blk, vc_blk = 0, N_PAIRS
        else:
            cproj = _inproj(ct, g_pre_mix[l], mods, ctx_row, w_in_l, tm_ctx)
            kc_blk, vc_blk = N_PAIRS, 2 * N_PAIRS
        cproj3 = cproj.reshape(b, n_ctx, cproj.shape[1])

        attn = _attn(proj.reshape(b, n, PROJ_WIDTH), cproj3, _attn_bias_table(rpb[l]), kc_blk, vc_blk)
        four = _four_pos(dft_lat, _four_chan(proj, mw_lat[l], 1024), b, 1024, 512)
        xt, h2 = _outproj(attn.reshape(t_lat, NA_WIDTH), four.reshape(t_lat, FOURIER_WIDTH), w_out_l, xt,
                          g_post_mix[l], g_pre_ffn[l], mods, lat_row, tm_lat)
        xt = _ffn(h2, w_up_l, conv_w[l], conv_b[l], w_down_l, xt, g_post_ffn[l], mods, lat_row, n, tm_lat)

        if not last:
            mw_ctx = _four_w(jnp.asarray(cc * scale_ctx, F32), jnp.asarray(sc * scale_ctx, F32),
                             w_four[l:l + 1])[0]
            dft_ctx = _dft_tables(n_ctx, 16)
            attn_c = _ctx_attn(cproj3)
            four_c = _four_pos(dft_ctx, _four_chan(cproj, mw_ctx, 512), b, n_ctx // 2, 512)
            ct, hc2 = _outproj(attn_c.reshape(t_ctx, NA_WIDTH), four_c.reshape(t_ctx, FOURIER_WIDTH),
                               w_out_l, ct, g_post_mix[l], g_pre_ffn[l], mods, ctx_row, tm_ctx)
            ct = _ffn(hc2, w_up_l, conv_w[l], conv_b[l], w_down_l, ct, g_post_ffn[l], mods, ctx_row,
                      n_ctx, tm_ctx)
    return xt.reshape(b, n, d)
```

```python
import functools

import jax
import jax.numpy as jnp
import numpy as np
from jax import lax
from jax.experimental import pallas as pl
from jax.experimental.pallas import tpu as pltpu

D_MODEL = 2048
DEPTH = 2
GRID_W = 64
CTX_LEN = 256
NA_HEADS = 16
HEAD_DIM = 64
NA_WIDTH = NA_HEADS * HEAD_DIM
FOURIER_GROUPS = 8
FOURIER_GROUP_DIM = 128
FOURIER_WIDTH = FOURIER_GROUPS * FOURIER_GROUP_DIM
PROJ_WIDTH = 3 * NA_WIDTH + FOURIER_WIDTH
WIN_ROWS = 8
WIN_COLS = 16
D_FF = 5632
N_MOD = 6
EPS = 1e-6
ATTN_SCALE = HEAD_DIM ** -0.5
MOD_ROWS = 16
CTX_MOD_ROW = 8

LANES = 128
DFT_ROWS = 512
FF_TILE = 512
CONV_TAPS = 3
FINISH_ROWS = 64
UP_K = 512
DOWN_COLS = 512
PROJ_COLS = 512
NORM_ROWS = 16
CONV_ROWS = 64
F32_SUBLANES = 8
BF16_SUBLANES = 16
HEAD_PAIR = LANES // HEAD_DIM
N_PAIRS = NA_HEADS // HEAD_PAIR
Q_ROWS = 4
K_ROWS = Q_ROWS + WIN_ROWS - 1
TQ = Q_ROWS * GRID_W
NK = K_ROWS * GRID_W
ATTN_UNROLL = 4
NEG = -1e30
VMEM_LIMIT = 56 * 1024 * 1024

F32 = jnp.float32
BF16 = jnp.bfloat16


def _params(*sem, flags=None):
    return pltpu.CompilerParams(dimension_semantics=sem, vmem_limit_bytes=VMEM_LIMIT, flags=flags)


def _dot(a, b):
    return jnp.dot(a, b, preferred_element_type=F32)


def _dot_nt(a, b):
    return lax.dot_general(a, b, (((1,), (1,)), ((), ())), preferred_element_type=F32)


def _rms(x, g):
    ms = jnp.mean(x * x, axis=-1, keepdims=True)
    return x * lax.rsqrt(ms + EPS) * g


def _norm_mod(x, g, shift, scale):
    return _rms(x, g) * (1.0 + scale) + shift


def _mods_kernel(cond_ref, w_ref, b_ref, o_ref):
    c = cond_ref[...]
    s = (c * jax.nn.sigmoid(c)).astype(BF16)
    o_ref[...] = _dot(s, w_ref[...].astype(BF16)) + b_ref[...]


def _mods(cond, w_ada, b_ada, tn=1024):
    depth, d, n = w_ada.shape
    return pl.pallas_call(
        _mods_kernel,
        grid=(depth, n // tn),
        in_specs=[
            pl.BlockSpec((MOD_ROWS, d), lambda l, j: (0, 0)),
            pl.BlockSpec((None, d, tn), lambda l, j: (l, 0, j)),
            pl.BlockSpec((None, 1, tn), lambda l, j: (l, 0, j)),
        ],
        out_specs=pl.BlockSpec((None, MOD_ROWS, tn), lambda l, j: (l, 0, j)),
        out_shape=jax.ShapeDtypeStruct((depth, MOD_ROWS, n), F32),
        compiler_params=_params("parallel", "parallel"),
        name="adaln_mods",
    )(cond, w_ada, b_ada.reshape(depth, 1, n))


def _mod_spec(which, row_fn):
    return pl.BlockSpec((None, None, 1, D_MODEL),
                        lambda i, *_: (row_fn(i), which, 0, 0))


def _interleave(pieces, chunks):
    if not pieces:
        for chunk in chunks:
            chunk()
        return
    per = -(-len(chunks) // len(pieces))
    for n, piece in enumerate(pieces):
        piece()
        for chunk in chunks[n * per:(n + 1) * per]:
            chunk()


def _two_stage(i, n_tiles, run):
    @pl.when(i == 0)
    def _():
        run(0, None)

    for parity in range(2):
        @pl.when((i >= 1) & (i < n_tiles) & (i % 2 == parity))
        def _():
            run(parity, 1 - parity)

    @pl.when(i == n_tiles)
    def _():
        run(None, (n_tiles - 1) % 2)


def _inproj_kernel(x_ref, g_ref, sh_ref, sc_ref, w_ref, o_ref, h0_ref, h1_ref, *, n_tiles):
    h_refs = (h0_ref, h1_ref)
    tm = x_ref.shape[0]

    def run(norm_to, dot_from):
        pieces, chunks = [], []
        if dot_from is not None:
            for c0 in range(0, w_ref.shape[1], PROJ_COLS):
                def piece(cols=slice(c0, c0 + PROJ_COLS)):
                    o_ref[:, cols] = _dot(h_refs[dot_from][...], w_ref[:, cols]).astype(o_ref.dtype)
                pieces.append(piece)
        if norm_to is not None:
            for r0 in range(0, tm, NORM_ROWS):
                def chunk(rows=slice(r0, r0 + NORM_ROWS)):
                    h_refs[norm_to][rows, :] = _norm_mod(
                        x_ref[rows, :], g_ref[...], sh_ref[...], sc_ref[...]).astype(BF16)
                chunks.append(chunk)
        _interleave(pieces, chunks)

    _two_stage(pl.program_id(0), n_tiles, run)


def _inproj(x, g, mods, row_fn, w, tm):
    t, d = x.shape
    n = w.shape[1]
    n_tiles = t // tm
    cur = lambda i: jnp.minimum(i, n_tiles - 1)
    prev = lambda i: jnp.maximum(i - 1, 0)
    return pl.pallas_call(
        functools.partial(_inproj_kernel, n_tiles=n_tiles),
        grid=(n_tiles + 1,),
        in_specs=[
            pl.BlockSpec((tm, d), lambda i: (cur(i), 0)),
            pl.BlockSpec((1, d), lambda i: (0, 0)),
            _mod_spec(0, lambda i: row_fn(cur(i))),
            _mod_spec(1, lambda i: row_fn(cur(i))),
            pl.BlockSpec((d, n), lambda i: (0, 0), pipeline_mode=pl.Buffered(1)),
        ],
        out_specs=pl.BlockSpec((tm, n), lambda i: (prev(i), 0)),
        out_shape=jax.ShapeDtypeStruct((t, n), BF16),
        scratch_shapes=[pltpu.VMEM((tm, d), BF16)] * 2,
        compiler_params=_params("arbitrary"),
        name="norm_inproj",
    )(x, g.reshape(1, d), mods, mods, w)


def _softmax_pv(parts):
    m = functools.reduce(jnp.maximum, [s.max(axis=-1, keepdims=True) for s, _ in parts])
    num = 0.0
    den = 0.0
    for s, v in parts:
        p = jnp.exp(s - m)
        den = den + p.sum(axis=-1, keepdims=True)
        num = num + _dot(p.astype(BF16), v)
    return num / den


def _head_masks():
    lane = lax.broadcasted_iota(jnp.int32, (1, LANES), 1)
    first = lane < HEAD_DIM
    return first, [first, jnp.logical_not(first)]


def _attn_kernel(q_ref, k_ref, v_ref, kc_ref, vc_ref, slab_ref, o_ref, tab_ref, *, n_tiles, rows):
    first, sels = _head_masks()
    kc = kc_ref[...]
    vc = vc_ref[...]

    @pl.when(pl.program_id(1) == 0)
    def _():
        pattern = _attn_slab_pattern(rows)
        for p in range(pattern.shape[0]):
            for hh in range(HEAD_PAIR):
                for qr in range(Q_ROWS):
                    for kr in range(K_ROWS):
                        tab_ref[p, hh, qr * GRID_W:(qr + 1) * GRID_W, kr * GRID_W:(kr + 1) * GRID_W] = (
                            slab_ref[hh, int(pattern[p, qr, kr])])

    def scores(t, hh):
        q0 = pl.multiple_of(t * TQ, TQ)
        k0 = pl.multiple_of(jnp.clip(t * Q_ROWS - WIN_ROWS // 2, 0, rows - K_ROWS) * GRID_W, GRID_W)
        pat = jnp.where(t == 0, 0, jnp.where(t == n_tiles - 1, 2, 1))
        q = q_ref[pl.ds(q0, TQ), :] * ATTN_SCALE
        qm = jnp.where(sels[hh], q, jnp.zeros_like(q))
        s_loc = _dot_nt(qm, k_ref[pl.ds(k0, NK), :]) + tab_ref[pat, hh]
        s_ctx = _dot_nt(qm, kc)
        return k0, s_loc, s_ctx

    def tile_group(tg, carry):
        units = [(tg * ATTN_UNROLL + ti, hh) for ti in range(ATTN_UNROLL) for hh in range(HEAD_PAIR)]
        pending = None
        outs = {}
        for unit in units + [None]:
            nxt = scores(*unit) if unit is not None else None
            if pending is not None:
                (t, hh), (k0, s_loc, s_ctx) = pending
                outs[hh] = _softmax_pv([(s_loc, v_ref[pl.ds(k0, NK), :]), (s_ctx, vc)])
                if hh == HEAD_PAIR - 1:
                    q0 = pl.multiple_of(t * TQ, TQ)
                    o_ref[pl.ds(q0, TQ), :] = jnp.where(first, outs[0], outs[1]).astype(o_ref.dtype)
            pending = (unit, nxt)
        return carry

    lax.fori_loop(0, n_tiles // ATTN_UNROLL, tile_group, 0)


def _attn(proj, cproj, slabs, kc_blk, vc_blk):
    b, n, _ = proj.shape
    rows = n // GRID_W
    n_tiles = rows // Q_ROWS
    lat = lambda off: pl.BlockSpec((None, n, LANES), lambda p, i: (i, 0, off + p))
    cx = lambda off: pl.BlockSpec((None, CTX_LEN, LANES), lambda p, i: (i, 0, off + p))
    return pl.pallas_call(
        functools.partial(_attn_kernel, n_tiles=n_tiles, rows=rows),
        grid=(N_PAIRS, b),
        in_specs=[
            lat(0), lat(N_PAIRS), lat(2 * N_PAIRS), cx(kc_blk), cx(vc_blk),
            pl.BlockSpec((HEAD_PAIR,) + slabs.shape[1:], lambda p, i: (p, 0, 0, 0)),
        ],
        out_specs=pl.BlockSpec((None, n, LANES), lambda p, i: (i, 0, p)),
        out_shape=jax.ShapeDtypeStruct((b, n, NA_WIDTH), BF16),
        scratch_shapes=[pltpu.VMEM((3, HEAD_PAIR, TQ, NK), F32)],
        compiler_params=_params("arbitrary", "arbitrary"),
        name="nbr_attn",
    )(proj, proj, proj, cproj, cproj, slabs)


def _ctx_attn_kernel(q_ref, k_ref, v_ref, o_ref):
    first, sels = _head_masks()
    q = q_ref[...] * ATTN_SCALE
    k = k_ref[...]
    v = v_ref[...]
    outs = []
    for hh in range(HEAD_PAIR):
        qm = jnp.where(sels[hh], q, jnp.zeros_like(q))
        outs.append(_softmax_pv([(_dot_nt(qm, k), v)]))
    o_ref[...] = jnp.where(first, outs[0], outs[1]).astype(o_ref.dtype)


def _ctx_attn(cproj):
    b, n, _ = cproj.shape
    spec = lambda off: pl.BlockSpec((None, n, LANES), lambda i, p: (i, 0, off + p))
    return pl.pallas_call(
        _ctx_attn_kernel,
        grid=(b, N_PAIRS),
        in_specs=[spec(0), spec(N_PAIRS), spec(2 * N_PAIRS)],
        out_specs=spec(0),
        out_shape=jax.ShapeDtypeStruct((b, n, NA_WIDTH), BF16),
        compiler_params=_params("parallel", "parallel"),
        name="ctx_attn",
    )(cproj, cproj, cproj)


MASKED_SLAB = 2 * WIN_ROWS - 1


def _attn_bias_slabs(rpb):
    qc = np.arange(GRID_W)[:, None]
    kc = np.arange(GRID_W)[None, :]
    cs = np.clip(qc - WIN_COLS // 2, 0, GRID_W - WIN_COLS)
    col_ok = (kc >= cs) & (kc < cs + WIN_COLS)
    dc = kc - qc + WIN_COLS - 1
    onehot = np.zeros((2 * WIN_COLS - 1, GRID_W, GRID_W), np.float32)
    qi, ki = np.nonzero(col_ok)
    onehot[dc[qi, ki], qi, ki] = 1.0
    t1 = jnp.einsum('hrd,dqk->hrqk', rpb, jnp.asarray(onehot), precision=lax.Precision.HIGHEST)
    t1 = jnp.where(jnp.asarray(col_ok)[None, None], t1, NEG)
    neg_slab = jnp.full((NA_HEADS, 1, GRID_W, GRID_W), NEG, F32)
    return jnp.concatenate([t1, neg_slab], axis=1)


def _attn_slab_pattern(rows):
    n_tiles = rows // Q_ROWS
    idx = np.full((3, Q_ROWS, K_ROWS), MASKED_SLAB, np.int32)
    for p, t in enumerate((0, 1, n_tiles - 1)):
        k_start = int(np.clip(t * Q_ROWS - WIN_ROWS // 2, 0, rows - K_ROWS))
        for qr in range(Q_ROWS):
            r = t * Q_ROWS + qr
            rs = int(np.clip(r - WIN_ROWS // 2, 0, rows - WIN_ROWS))
            for kr in range(K_ROWS):
                krow = k_start + kr
                if rs <= krow < rs + WIN_ROWS:
                    idx[p, qr, kr] = krow - r + WIN_ROWS - 1
    return idx


def _split_bf16(a):
    hi = a.astype(BF16)
    lo = (a - hi.astype(F32)).astype(BF16)
    return hi, lo


def _dot3(a, b):
    ah, al = _split_bf16(a)
    bh, bl = _split_bf16(b)
    return _dot(ah, bh) + (_dot(ah, bl) + _dot(al, bh))


def _four_w_kernel(cc_ref, sc_ref, w_ref, o_ref):
    w = w_ref[...]
    o_ref[:, :FOURIER_GROUP_DIM] = _dot3(cc_ref[...], w).astype(BF16)
    o_ref[:, FOURIER_GROUP_DIM:] = _dot3(sc_ref[...], w).astype(BF16)


def _four_w(cc, sc, w_four):
    depth, g, c, _ = w_four.shape
    tab = pl.BlockSpec((c, c), lambda l, i: (0, 0))
    return pl.pallas_call(
        _four_w_kernel,
        grid=(depth, g),
        in_specs=[tab, tab, pl.BlockSpec((None, None, c, c), lambda l, i: (l, i, 0, 0))],
        out_specs=pl.BlockSpec((None, None, c, 2 * c), lambda l, i: (l, i, 0, 0)),
        out_shape=jax.ShapeDtypeStruct((depth, g, c, 2 * c), BF16),
        compiler_params=_params("parallel", "parallel"),
        name="fourier_weights",
    )(cc, sc, w_four)


def _four_chan_kernel(f_ref, m_ref, yce_ref, yco_ref, yse_ref, yso_ref, yc_scr, ys_scr):
    c = FOURIER_GROUP_DIM
    half = f_ref.shape[0] // 2
    for g in range(FOURIER_GROUPS):
        cols = slice(g * c, (g + 1) * c)
        y = _dot(f_ref[:, cols], m_ref[g])
        for scr, part, even_ref, odd_ref in ((yc_scr, y[:, :c], yce_ref, yco_ref),
                                             (ys_scr, y[:, c:], yse_ref, yso_ref)):
            scr[...] = part
            even_ref[:, cols] = scr[pl.ds(0, half, stride=2), :].astype(BF16)
            odd_ref[:, cols] = scr[pl.ds(1, half, stride=2), :].astype(BF16)


def _four_chan(proj, mw, tm):
    t = proj.shape[0]
    f_blk = proj.shape[1] // FOURIER_WIDTH - 1
    out = pl.BlockSpec((tm // 2, FOURIER_WIDTH), lambda i: (i, 0))
    return pl.pallas_call(
        _four_chan_kernel,
        grid=(t // tm,),
        in_specs=[
            pl.BlockSpec((tm, FOURIER_WIDTH), lambda i: (i, f_blk)),
            pl.BlockSpec(mw.shape, lambda i: (0, 0, 0)),
        ],
        out_specs=[out] * 4,
        out_shape=[jax.ShapeDtypeStruct((t // 2, FOURIER_WIDTH), BF16)] * 4,
        scratch_shapes=[pltpu.VMEM((tm, FOURIER_GROUP_DIM), F32)] * 2,
        compiler_params=_params("parallel"),
        name="fourier_chan",
    )(proj, mw)


def _four_pos_kernel(ce_ref, se_ref, co_ref, so_ref, ae_ref, ao_ref, be_ref, bo_ref, o_ref):
    step = min(DFT_ROWS, ce_ref.shape[0])
    for r0 in range(0, ce_ref.shape[0], step):
        rows = slice(r0, r0 + step)
        even = _dot(ce_ref[rows, :], ae_ref[...]) + _dot(se_ref[rows, :], be_ref[...])
        odd = _dot(co_ref[rows, :], ao_ref[...]) + _dot(so_ref[rows, :], bo_ref[...])
        o_ref[0, rows, :] = (even + odd).astype(o_ref.dtype)
        o_ref[1, rows, :] = (even - odd).astype(o_ref.dtype)


def _four_pos(tables, y_parts, b, tm, tn):
    yce, yco, yse, yso = (y.reshape(b, y.shape[0] // b, y.shape[1]) for y in y_parts)
    _, half, w = yce.shape
    mat = pl.BlockSpec((tm, half), lambda i, bb, j: (i, 0), pipeline_mode=pl.Buffered(1))
    dat = pl.BlockSpec((None, half, tn), lambda i, bb, j: (bb, 0, j))
    out = pl.pallas_call(
        _four_pos_kernel,
        grid=(half // tm, b, w // tn),
        in_specs=[mat, mat, mat, mat, dat, dat, dat, dat],
        out_specs=pl.BlockSpec((None, 2, tm, tn), lambda i, bb, j: (bb, 0, i, j)),
        out_shape=jax.ShapeDtypeStruct((b, 2, half, w), BF16),
        compiler_params=_params("parallel", "parallel", "parallel"),
        name="fourier_pos",
    )(*tables, yce, yco, yse, yso)
    return out.reshape(b, 2 * half, w)


def _dft_tables(n, split):
    half = n // 2
    j = jnp.arange(half, dtype=jnp.int32)[:, None]
    hi = jnp.arange(half // split, dtype=jnp.int32)[None, :] * split
    lo = jnp.arange(split, dtype=jnp.int32)[None, :]
    ang = lambda k: (2.0 * np.pi / half) * ((j * k) % half).astype(F32)
    ch, sh, cl, sl = jnp.cos(ang(hi)), jnp.sin(ang(hi)), jnp.cos(ang(lo)), jnp.sin(ang(lo))
    ce = (ch[:, :, None] * cl[:, None, :] - sh[:, :, None] * sl[:, None, :]).reshape(half, half)
    se = (sh[:, :, None] * cl[:, None, :] + ch[:, :, None] * sl[:, None, :]).reshape(half, half)
    phi = (2.0 * np.pi / n) * j.astype(F32)
    cp, sp = jnp.cos(phi), jnp.sin(phi)
    co = ce * cp - se * sp
    so = se * cp + ce * sp
    return tuple(t.astype(BF16) for t in (ce, se, co, so))


def _outproj_kernel(a_ref, f_ref, w_ref, x_ref, gpost_ref, gate_ref, gpre_ref, sh_ref, sc_ref,
                    xo_ref, h_ref):
    mix = _dot(a_ref[...], w_ref[:NA_WIDTH, :]) + _dot(f_ref[...], w_ref[NA_WIDTH:, :])
    x = x_ref[...] + gate_ref[...] * _rms(mix, gpost_ref[...])
    xo_ref[...] = x
    h_ref[...] = _norm_mod(x, gpre_ref[...], sh_ref[...], sc_ref[...]).astype(BF16)


def _outproj(attn, four, w, x, g_post, g_pre, mods, row_fn, tm):
    t, d = x.shape
    vec = pl.BlockSpec((1, d), lambda i: (0, 0))
    row = pl.BlockSpec((tm, d), lambda i: (i, 0))
    half = pl.BlockSpec((tm, NA_WIDTH), lambda i: (i, 0))
    return pl.pallas_call(
        _outproj_kernel,
        grid=(t // tm,),
        in_specs=[half, half, pl.BlockSpec(w.shape, lambda i: (0, 0), pipeline_mode=pl.Buffered(1)), row, vec,
                  _mod_spec(2, row_fn), vec, _mod_spec(3, row_fn), _mod_spec(4, row_fn)],
        out_specs=[row, row],
        out_shape=[jax.ShapeDtypeStruct((t, d), F32), jax.ShapeDtypeStruct((t, d), BF16)],
        compiler_params=_params("parallel"),
        name="outproj_residual",
    )(attn, four, w, x, g_post.reshape(1, d), mods, g_pre.reshape(1, d), mods, mods)


def _matmul_kernel(x_ref, w_ref, o_ref):
    o_ref[...] = _dot(x_ref[...], w_ref[...]).astype(o_ref.dtype)


def _matmul(x, w, tn, out_dtype, name):
    m, k = x.shape
    n = w.shape[1]
    return pl.pallas_call(
        _matmul_kernel,
        grid=(n // tn,),
        in_specs=[pl.BlockSpec((m, k), lambda j: (0, 0)), pl.BlockSpec((k, tn), lambda j: (0, j))],
        out_specs=pl.BlockSpec((m, tn), lambda j: (0, j)),
        out_shape=jax.ShapeDtypeStruct((m, n), out_dtype),
        compiler_params=_params("parallel"),
        name=name,
    )(x, w)


def _ffn_kernel(h_ref, wa_ref, wg_ref, edge_ref, cp_ref, wd_ref, x_ref, gate_ref, gpost_ref, o_ref,
                acc_ref, ua0, ug0, ua1, ug1, act0, act1,
                *, tm, n_ff_tiles, n_row_tiles):
    i = pl.program_id(0)
    j = pl.program_id(1)
    nj = n_ff_tiles
    pad = F32_SUBLANES
    tf = wa_ref.shape[1]
    u_refs = ((ua0, ug0), (ua1, ug1))
    act_refs = (act0, act1)

    def up_proj_part(half, k0):
        w_ref = (wa_ref, wg_ref)[half]
        return _dot(h_ref[:, k0:k0 + UP_K], w_ref[k0:k0 + UP_K, :])

    def up_proj_store(parity, half, u):
        u_ref = u_refs[parity][half]
        u_ref[0:pad, :] = jnp.broadcast_to(edge_ref[half:half + 1, :], (pad, tf))
        u_ref[pad:pad + tm, :] = u
        u_ref[pad + tm:, :] = jnp.broadcast_to(edge_ref[2 + half:3 + half, :], (pad, tf))

    def conv(u_ref, r0, cols, p0):
        win = u_ref[r0:r0 + CONV_ROWS + 2 * pad, cols]
        mid = win[pad:pad + CONV_ROWS]
        down = pltpu.roll(win, 1, axis=0)[pad:pad + CONV_ROWS]
        up = pltpu.roll(win, CONV_ROWS + 2 * pad - 1, axis=0)[pad:pad + CONV_ROWS]
        return (down * cp_ref[p0:p0 + 1, cols] + mid * cp_ref[p0 + 1:p0 + 2, cols]
                + up * cp_ref[p0 + 2:p0 + 3, cols] + cp_ref[p0 + 3:p0 + 4, cols])

    def conv_gate(parity, r0, c0):
        ua_ref, ug_ref = u_refs[parity]
        cols = slice(c0, c0 + LANES)
        a = conv(ua_ref, r0, cols, 0)
        g = conv(ug_ref, r0, cols, CONV_TAPS + 1)
        act_refs[parity][r0:r0 + CONV_ROWS, cols] = (g * jax.nn.sigmoid(g) * a).astype(BF16)

    def down_proj(parity, cols):
        return _dot(act_refs[parity][...], wd_ref[:, cols])

    chunks = [(r0, c0) for r0 in range(0, tm, CONV_ROWS) for c0 in range(0, tf, LANES)]
    d_out = wd_ref.shape[1]
    down_groups = [slice(c, c + DOWN_COLS) for c in range(0, d_out, DOWN_COLS)]

    def finish_rows(r0):
        rows = slice(r0, r0 + FINISH_ROWS)
        o_ref[rows, :] = x_ref[rows, :] + gate_ref[...] * _rms(acc_ref[rows, :], gpost_ref[...])
        acc_ref[rows, :] = jnp.zeros((FINISH_ROWS, d_out), F32)

    def run_step(up=None, conv_of=None, down=None, finish=False):
        pieces = []
        if down is not None:
            for cols in down_groups:
                def piece(cols=cols):
                    acc_ref[:, cols] += down_proj(down, cols)
                pieces.append((piece, tf * DOWN_COLS))
        n_down = len(pieces)
        if up is not None:
            d_in = h_ref.shape[1]
            for half in range(2):
                partial_u = []
                for k0 in range(0, d_in, UP_K):
                    def piece(half=half, k0=k0, partial_u=partial_u):
                        part = up_proj_part(half, k0)
                        partial_u[:] = [part if not partial_u else partial_u[0] + part]
                        if k0 + UP_K == d_in:
                            up_proj_store(up, half, partial_u[0])
                    pieces.append((piece, UP_K * tf))
        conv_todo = [functools.partial(conv_gate, conv_of, *c) for c in chunks] if conv_of is not None else []
        fin_todo = [functools.partial(finish_rows, r0) for r0 in range(0, tm, FINISH_ROWS)] if finish else []
        n_conv, n_fin = len(conv_todo), len(fin_todo)
        total = sum(w for _, w in pieces)
        total_up = sum(w for _, w in pieces[n_down:])
        done = done_up = 0.0
        for n, (piece, w) in enumerate(pieces):
            piece()
            done += w
            is_last = n == len(pieces) - 1
            conv_upto = n_conv if is_last else round(n_conv * done / total)
            while n_conv - len(conv_todo) < conv_upto:
                conv_todo.pop(0)()
            if n >= n_down:
                done_up += w
                fin_upto = n_fin if is_last else round(n_fin * done_up / total_up)
                while n_fin - len(fin_todo) < fin_upto:
                    fin_todo.pop(0)()
        for chunk in conv_todo + fin_todo:
            chunk()

    s = i * nj + j
    last = n_row_tiles * nj
    finishing = (j == 1) & (i >= 1)

    @pl.when(s == 0)
    def _():
        acc_ref[...] = jnp.zeros_like(acc_ref)
        run_step(up=0)

    @pl.when(s == 1)
    def _():
        run_step(up=1, conv_of=0)

    for parity in range(2):
        steady = (s >= 2) & (s < last) & (s % 2 == parity)

        @pl.when(steady & jnp.logical_not(finishing))
        def _():
            run_step(up=parity, conv_of=1 - parity, down=parity)

        @pl.when(steady & finishing)
        def _():
            run_step(up=parity, conv_of=1 - parity, down=parity, finish=True)

    @pl.when(s == last)
    def _():
        run_step(conv_of=(last - 1) % 2, down=last % 2)

    @pl.when(s == last + 1)
    def _():
        run_step(down=(last + 1) % 2, finish=True)


def _ffn(h, w_up, conv_w, conv_b, w_down, x, g_post, mods, row_fn, seq_len, tm):
    t, d = x.shape
    ff = w_down.shape[0]
    tf = FF_TILE
    nj = ff // tf
    n_tiles = t // tm
    assert seq_len % tm == 0

    h3 = h.reshape(n_tiles, tm, d)
    start = (np.arange(n_tiles) * tm) % seq_len == 0
    end = ((np.arange(n_tiles) + 1) * tm) % seq_len == 0
    zero = jnp.zeros((1, d), h.dtype)
    prev = jnp.concatenate([zero, h3[:-1, tm - 1]], axis=0)
    nxt = jnp.concatenate([h3[1:, 0], zero], axis=0)
    prev = jnp.where(jnp.asarray(start)[:, None], jnp.zeros_like(prev), prev)
    nxt = jnp.where(jnp.asarray(end)[:, None], jnp.zeros_like(nxt), nxt)
    edge_rows = jnp.concatenate([prev, nxt], axis=0)
    pad = -edge_rows.shape[0] % BF16_SUBLANES
    edge_rows = jnp.pad(edge_rows, ((0, pad), (0, 0)))
    u_edge = _matmul(edge_rows, w_up, 2 * ff // 8, F32, "conv_ffn_edges")
    u_prev, u_next = u_edge[:n_tiles], u_edge[n_tiles:2 * n_tiles]
    edges = jnp.stack([u_prev[:, :ff], u_prev[:, ff:], u_next[:, :ff], u_next[:, ff:]], axis=1)
    edges = edges.reshape(n_tiles, 4, nj, tf).transpose(0, 2, 1, 3)
    conv_p = jnp.concatenate([conv_w[:, :ff], conv_b[None, :ff], conv_w[:, ff:], conv_b[None, ff:]], axis=0)
    conv_p = conv_p.reshape(2 * (CONV_TAPS + 1), nj, tf).transpose(1, 0, 2)

    assert nj >= 3
    vec = pl.BlockSpec((1, d), lambda i, j: (0, 0))
    cur = lambda i: jnp.minimum(i, n_tiles - 1)
    fin = lambda i, j, switch: jnp.where(j <= switch, jnp.maximum(i - 1, 0), cur(i))
    lag = lambda j, k: (j + nj - k) % nj
    col = lambda rows, off, k: pl.BlockSpec((rows, tf), lambda i, j: (0, off + lag(j, k)))
    u_buf = pltpu.VMEM((tm + 2 * F32_SUBLANES, tf), F32)
    act_buf = pltpu.VMEM((tm, tf), BF16)
    return pl.pallas_call(
        functools.partial(_ffn_kernel, tm=tm, n_ff_tiles=nj, n_row_tiles=n_tiles),
        grid=(n_tiles + 1, nj),
        in_specs=[
            pl.BlockSpec((tm, d), lambda i, j: (cur(i), 0)),
            col(d, 0, 0), col(d, nj, 0),
            pl.BlockSpec((None, None, 4, tf), lambda i, j: (cur(i), j, 0, 0)),
            pl.BlockSpec((None, 2 * (CONV_TAPS + 1), tf), lambda i, j: (lag(j, 1), 0, 0)),
            pl.BlockSpec((tf, d), lambda i, j: (lag(j, 2), 0)),
            pl.BlockSpec((tm, d), lambda i, j: (fin(i, j, 1), 0)),
            pl.BlockSpec((None, None, 1, d), lambda i, j: (row_fn(fin(i, j, 1)), 5, 0, 0)),
            vec,
        ],
        out_specs=pl.BlockSpec((tm, d), lambda i, j: (fin(i, j, 3), 0)),
        out_shape=jax.ShapeDtypeStruct((t, d), F32),
        scratch_shapes=[pltpu.VMEM((tm, d), F32), u_buf, u_buf, u_buf, u_buf, act_buf, act_buf],
        compiler_params=_params("arbitrary", "arbitrary"),
        name="conv_ffn",
    )(h, w_up, w_up, edges, conv_p, w_down, x, mods,
      g_post.reshape(1, d))


def kernel(x, c, ctx, c_ctx, w_ada, b_ada, g_pre_mix, w_in, rpb, w_four, w_out, g_post_mix,
           g_pre_ffn, w_up, conv_w, conv_b, w_down, g_post_ffn):
    b, n, d = x.shape
    n_ctx = ctx.shape[1]
    assert (d, n, n_ctx, b) == (D_MODEL, GRID_W * GRID_W, CTX_LEN, CTX_MOD_ROW)
    t_lat, t_ctx = b * n, b * n_ctx
    tm_lat, tm_ctx = 512, CTX_LEN
    lat_row = lambda i: (i * tm_lat) // n
    ctx_row = lambda i: CTX_MOD_ROW

    cond = jnp.zeros((MOD_ROWS, d), F32).at[:b].set(c).at[CTX_MOD_ROW].set(c_ctx)
    mods_all = _mods(cond, w_ada, b_ada)

    scale = (n * FOURIER_GROUP_DIM) ** -0.5
    scale_ctx = (n_ctx * FOURIER_GROUP_DIM) ** -0.5
    kk = np.outer(np.arange(FOURIER_GROUP_DIM), np.arange(FOURIER_GROUP_DIM)) % FOURIER_GROUP_DIM
    ang = 2.0 * np.pi * kk / FOURIER_GROUP_DIM
    cc, sc = np.cos(ang), -np.sin(ang)
    mw_lat = _four_w(jnp.asarray(cc * scale, F32), jnp.asarray(sc * scale, F32), w_four)
    dft_lat = _dft_tables(n, GRID_W)

    xt = x.reshape(t_lat, d)
    ct = ctx.reshape(t_ctx, d)
    for l in range(DEPTH):
        last = l == DEPTH - 1
        mods = mods_all[l].reshape(MOD_ROWS, N_MOD, 1, d)
        w_in_l = w_in[l].astype(BF16)
        w_out_l = w_out[l].astype(BF16)
        w_up_l = w_up[l].astype(BF16)
        w_down_l = w_down[l].astype(BF16)

        proj = _inproj(xt, g_pre_mix[l], mods, lat_row, w_in_l, tm_lat)
        if last:
            cproj = _inproj(ct, g_pre_mix[l], mods, ctx_row, w_in_l[:, NA_WIDTH:3 * NA_WIDTH], tm_ctx)
            kc_blk, vc_blk = 0, N_PAIRS
        else:
            cproj = _inproj(ct, g_pre_mix[l], mods, ctx_row, w_in_l, tm_ctx)
            kc_blk, vc_blk = N_PAIRS, 2 * N_PAIRS
        cproj3 = cproj.reshape(b, n_ctx, cproj.shape[1])

        attn = _attn(proj.reshape(b, n, PROJ_WIDTH), cproj3, _attn_bias_slabs(rpb[l]), kc_blk, vc_blk)
        four = _four_pos(dft_lat, _four_chan(proj, mw_lat[l], 1024), b, 1024, 512)
        xt, h2 = _outproj(attn.reshape(t_lat, NA_WIDTH), four.reshape(t_lat, FOURIER_WIDTH), w_out_l, xt,
                          g_post_mix[l], g_pre_ffn[l], mods, lat_row, tm_lat)
        xt = _ffn(h2, w_up_l, conv_w[l], conv_b[l], w_down_l, xt, g_post_ffn[l], mods, lat_row, n, tm_lat)

        if not last:
            mw_ctx = _four_w(jnp.asarray(cc * scale_ctx, F32), jnp.asarray(sc * scale_ctx, F32),
                             w_four[l:l + 1])[0]
            dft_ctx = _dft_tables(n_ctx, 16)
            attn_c = _ctx_attn(cproj3)
            four_c = _four_pos(dft_ctx, _four_chan(cproj, mw_ctx, 512), b, n_ctx // 2, 512)
            ct, hc2 = _outproj(attn_c.reshape(t_ctx, NA_WIDTH), four_c.reshape(t_ctx, FOURIER_WIDTH),
                               w_out_l, ct, g_post_mix[l], g_pre_ffn[l], mods, ctx_row, tm_ctx)
            ct = _ffn(hc2, w_up_l, conv_w[l], conv_b[l], w_down_l, ct, g_post_ffn[l], mods, ctx_row,
                      n_ctx, tm_ctx)
    return xt.reshape(b, n, d)
```

```python
import functools

import jax
import jax.numpy as jnp
import numpy as np
from jax import lax
from jax.experimental import pallas as pl
from jax.experimental.pallas import tpu as pltpu

D_MODEL = 2048
DEPTH = 2
GRID_W = 64
CTX_LEN = 256
NA_HEADS = 16
HEAD_DIM = 64
NA_WIDTH = NA_HEADS * HEAD_DIM
FOURIER_GROUPS = 8
FOURIER_GROUP_DIM = 128
FOURIER_WIDTH = FOURIER_GROUPS * FOURIER_GROUP_DIM
PROJ_WIDTH = 3 * NA_WIDTH + FOURIER_WIDTH
WIN_ROWS = 8
WIN_COLS = 16
D_FF = 5632
N_MOD = 6
EPS = 1e-6
ATTN_SCALE = HEAD_DIM ** -0.5
MOD_ROWS = 16
CTX_MOD_ROW = 8

LANES = 128
DFT_RADIX = 4
DFT_ROWS = 512
FF_TILE = 512
CONV_TAPS = 3
FINISH_ROWS = 64
UP_K = 512
DOWN_COLS = 512
PROJ_COLS = 512
NORM_ROWS = 16
CONV_ROWS = 64
F32_SUBLANES = 8
BF16_SUBLANES = 16
HEAD_PAIR = LANES // HEAD_DIM
N_PAIRS = NA_HEADS // HEAD_PAIR
Q_ROWS = 4
K_ROWS = Q_ROWS + WIN_ROWS - 1
TQ = Q_ROWS * GRID_W
NK = K_ROWS * GRID_W
ATTN_UNROLL = 4
NEG = -1e30
VMEM_LIMIT = 56 * 1024 * 1024

F32 = jnp.float32
BF16 = jnp.bfloat16


def _params(*sem, flags=None):
    return pltpu.CompilerParams(dimension_semantics=sem, vmem_limit_bytes=VMEM_LIMIT, flags=flags)


def _dot(a, b):
    return jnp.dot(a, b, preferred_element_type=F32)


def _dot_nt(a, b):
    return lax.dot_general(a, b, (((1,), (1,)), ((), ())), preferred_element_type=F32)


def _rms(x, g):
    ms = jnp.mean(x * x, axis=-1, keepdims=True)
    return x * lax.rsqrt(ms + EPS) * g


def _norm_mod(x, g, shift, scale):
    return _rms(x, g) * (1.0 + scale) + shift


def _mods_kernel(cond_ref, w_ref, b_ref, o_ref):
    c = cond_ref[...]
    s = (c * jax.nn.sigmoid(c)).astype(BF16)
    o_ref[...] = _dot(s, w_ref[...].astype(BF16)) + b_ref[...]


def _mods(cond, w_ada, b_ada, tn=1024):
    depth, d, n = w_ada.shape
    return pl.pallas_call(
        _mods_kernel,
        grid=(depth, n // tn),
        in_specs=[
            pl.BlockSpec((MOD_ROWS, d), lambda l, j: (0, 0)),
            pl.BlockSpec((None, d, tn), lambda l, j: (l, 0, j)),
            pl.BlockSpec((None, 1, tn), lambda l, j: (l, 0, j)),
        ],
        out_specs=pl.BlockSpec((None, MOD_ROWS, tn), lambda l, j: (l, 0, j)),
        out_shape=jax.ShapeDtypeStruct((depth, MOD_ROWS, n), F32),
        compiler_params=_params("parallel", "parallel"),
        name="adaln_mods",
    )(cond, w_ada, b_ada.reshape(depth, 1, n))


def _mod_spec(which, row_fn):
    return pl.BlockSpec((None, None, 1, D_MODEL),
                        lambda i, *_: (row_fn(i), which, 0, 0))


def _interleave(pieces, chunks):
    if not pieces:
        for chunk in chunks:
            chunk()
        return
    per = -(-len(chunks) // len(pieces))
    for n, piece in enumerate(pieces):
        piece()
        for chunk in chunks[n * per:(n + 1) * per]:
            chunk()


def _two_stage(i, n_tiles, run):
    @pl.when(i == 0)
    def _():
        run(0, None)

    for parity in range(2):
        @pl.when((i >= 1) & (i < n_tiles) & (i % 2 == parity))
        def _():
            run(parity, 1 - parity)

    @pl.when(i == n_tiles)
    def _():
        run(None, (n_tiles - 1) % 2)


def _inproj_kernel(x_ref, g_ref, sh_ref, sc_ref, w_ref, o_ref, h0_ref, h1_ref, *, n_tiles):
    h_refs = (h0_ref, h1_ref)
    tm = x_ref.shape[0]

    def run(norm_to, dot_from):
        pieces, chunks = [], []
        if dot_from is not None:
            for c0 in range(0, w_ref.shape[1], PROJ_COLS):
                def piece(cols=slice(c0, c0 + PROJ_COLS)):
                    o_ref[:, cols] = _dot(h_refs[dot_from][...], w_ref[:, cols]).astype(o_ref.dtype)
                pieces.append(piece)
        if norm_to is not None:
            for r0 in range(0, tm, NORM_ROWS):
                def chunk(rows=slice(r0, r0 + NORM_ROWS)):
                    h_refs[norm_to][rows, :] = _norm_mod(
                        x_ref[rows, :], g_ref[...], sh_ref[...], sc_ref[...]).astype(BF16)
                chunks.append(chunk)
        _interleave(pieces, chunks)

    _two_stage(pl.program_id(0), n_tiles, run)


def _inproj(x, g, mods, row_fn, w, tm):
    t, d = x.shape
    n = w.shape[1]
    n_tiles = t // tm
    cur = lambda i: jnp.minimum(i, n_tiles - 1)
    prev = lambda i: jnp.maximum(i - 1, 0)
    return pl.pallas_call(
        functools.partial(_inproj_kernel, n_tiles=n_tiles),
        grid=(n_tiles + 1,),
        in_specs=[
            pl.BlockSpec((tm, d), lambda i: (cur(i), 0)),
            pl.BlockSpec((1, d), lambda i: (0, 0)),
            _mod_spec(0, lambda i: row_fn(cur(i))),
            _mod_spec(1, lambda i: row_fn(cur(i))),
            pl.BlockSpec((d, n), lambda i: (0, 0), pipeline_mode=pl.Buffered(1)),
        ],
        out_specs=pl.BlockSpec((tm, n), lambda i: (prev(i), 0)),
        out_shape=jax.ShapeDtypeStruct((t, n), BF16),
        scratch_shapes=[pltpu.VMEM((tm, d), BF16)] * 2,
        compiler_params=_params("arbitrary"),
        name="norm_inproj",
    )(x, g.reshape(1, d), mods, mods, w)


def _softmax_pv(parts):
    m = functools.reduce(jnp.maximum, [s.max(axis=-1, keepdims=True) for s, _ in parts])
    num = 0.0
    den = 0.0
    for s, v in parts:
        p = jnp.exp(s - m)
        den = den + p.sum(axis=-1, keepdims=True)
        num = num + _dot(p.astype(BF16), v)
    return num / den


def _head_masks():
    lane = lax.broadcasted_iota(jnp.int32, (1, LANES), 1)
    first = lane < HEAD_DIM
    return first, [first, jnp.logical_not(first)]


def _attn_kernel(q_ref, k_ref, v_ref, kc_ref, vc_ref, slab_ref, o_ref, tab_ref, *, n_tiles, rows):
    first, sels = _head_masks()
    kc = kc_ref[...]
    vc = vc_ref[...]

    @pl.when(pl.program_id(1) == 0)
    def _():
        pattern = _attn_slab_pattern(rows)
        for p in range(pattern.shape[0]):
            for hh in range(HEAD_PAIR):
                for qr in range(Q_ROWS):
                    for kr in range(K_ROWS):
                        tab_ref[p, hh, qr * GRID_W:(qr + 1) * GRID_W, kr * GRID_W:(kr + 1) * GRID_W] = (
                            slab_ref[hh, int(pattern[p, qr, kr])])

    def scores(t, hh):
        q0 = pl.multiple_of(t * TQ, TQ)
        k0 = pl.multiple_of(jnp.clip(t * Q_ROWS - WIN_ROWS // 2, 0, rows - K_ROWS) * GRID_W, GRID_W)
        pat = jnp.where(t == 0, 0, jnp.where(t == n_tiles - 1, 2, 1))
        q = q_ref[pl.ds(q0, TQ), :] * ATTN_SCALE
        qm = jnp.where(sels[hh], q, jnp.zeros_like(q))
        s_loc = _dot_nt(qm, k_ref[pl.ds(k0, NK), :]) + tab_ref[pat, hh]
        s_ctx = _dot_nt(qm, kc)
        return k0, s_loc, s_ctx

    def tile_group(tg, carry):
        units = [(tg * ATTN_UNROLL + ti, hh) for ti in range(ATTN_UNROLL) for hh in range(HEAD_PAIR)]
        pending = None
        outs = {}
        for unit in units + [None]:
            nxt = scores(*unit) if unit is not None else None
            if pending is not None:
                (t, hh), (k0, s_loc, s_ctx) = pending
                outs[hh] = _softmax_pv([(s_loc, v_ref[pl.ds(k0, NK), :]), (s_ctx, vc)])
                if hh == HEAD_PAIR - 1:
                    q0 = pl.multiple_of(t * TQ, TQ)
                    o_ref[pl.ds(q0, TQ), :] = jnp.where(first, outs[0], outs[1]).astype(o_ref.dtype)
            pending = (unit, nxt)
        return carry

    lax.fori_loop(0, n_tiles // ATTN_UNROLL, tile_group, 0)


def _attn(proj, cproj, slabs, kc_blk, vc_blk):
    b, n, _ = proj.shape
    rows = n // GRID_W
    n_tiles = rows // Q_ROWS
    lat = lambda off: pl.BlockSpec((None, n, LANES), lambda p, i: (i, 0, off + p))
    cx = lambda off: pl.BlockSpec((None, CTX_LEN, LANES), lambda p, i: (i, 0, off + p))
    return pl.pallas_call(
        functools.partial(_attn_kernel, n_tiles=n_tiles, rows=rows),
        grid=(N_PAIRS, b),
        in_specs=[
            lat(0), lat(N_PAIRS), lat(2 * N_PAIRS), cx(kc_blk), cx(vc_blk),
            pl.BlockSpec((HEAD_PAIR,) + slabs.shape[1:], lambda p, i: (p, 0, 0, 0)),
        ],
        out_specs=pl.BlockSpec((None, n, LANES), lambda p, i: (i, 0, p)),
        out_shape=jax.ShapeDtypeStruct((b, n, NA_WIDTH), BF16),
        scratch_shapes=[pltpu.VMEM((3, HEAD_PAIR, TQ, NK), F32)],
        compiler_params=_params("arbitrary", "arbitrary"),
        name="nbr_attn",
    )(proj, proj, proj, cproj, cproj, slabs)


def _ctx_attn_kernel(q_ref, k_ref, v_ref, o_ref):
    first, sels = _head_masks()
    q = q_ref[...] * ATTN_SCALE
    k = k_ref[...]
    v = v_ref[...]
    outs = []
    for hh in range(HEAD_PAIR):
        qm = jnp.where(sels[hh], q, jnp.zeros_like(q))
        outs.append(_softmax_pv([(_dot_nt(qm, k), v)]))
    o_ref[...] = jnp.where(first, outs[0], outs[1]).astype(o_ref.dtype)


def _ctx_attn(cproj):
    b, n, _ = cproj.shape
    spec = lambda off: pl.BlockSpec((None, n, LANES), lambda i, p: (i, 0, off + p))
    return pl.pallas_call(
        _ctx_attn_kernel,
        grid=(b, N_PAIRS),
        in_specs=[spec(0), spec(N_PAIRS), spec(2 * N_PAIRS)],
        out_specs=spec(0),
        out_shape=jax.ShapeDtypeStruct((b, n, NA_WIDTH), BF16),
        compiler_params=_params("parallel", "parallel"),
        name="ctx_attn",
    )(cproj, cproj, cproj)


MASKED_SLAB = 2 * WIN_ROWS - 1


def _attn_bias_slabs(rpb):
    qc = np.arange(GRID_W)[:, None]
    kc = np.arange(GRID_W)[None, :]
    cs = np.clip(qc - WIN_COLS // 2, 0, GRID_W - WIN_COLS)
    col_ok = (kc >= cs) & (kc < cs + WIN_COLS)
    dc = kc - qc + WIN_COLS - 1
    onehot = np.zeros((2 * WIN_COLS - 1, GRID_W, GRID_W), np.float32)
    qi, ki = np.nonzero(col_ok)
    onehot[dc[qi, ki], qi, ki] = 1.0
    t1 = jnp.einsum('hrd,dqk->hrqk', rpb, jnp.asarray(onehot), precision=lax.Precision.HIGHEST)
    t1 = jnp.where(jnp.asarray(col_ok)[None, None], t1, NEG)
    neg_slab = jnp.full((NA_HEADS, 1, GRID_W, GRID_W), NEG, F32)
    return jnp.concatenate([t1, neg_slab], axis=1)


def _attn_slab_pattern(rows):
    n_tiles = rows // Q_ROWS
    idx = np.full((3, Q_ROWS, K_ROWS), MASKED_SLAB, np.int32)
    for p, t in enumerate((0, 1, n_tiles - 1)):
        k_start = int(np.clip(t * Q_ROWS - WIN_ROWS // 2, 0, rows - K_ROWS))
        for qr in range(Q_ROWS):
            r = t * Q_ROWS + qr
            rs = int(np.clip(r - WIN_ROWS // 2, 0, rows - WIN_ROWS))
            for kr in range(K_ROWS):
                krow = k_start + kr
                if rs <= krow < rs + WIN_ROWS:
                    idx[p, qr, kr] = krow - r + WIN_ROWS - 1
    return idx


def _split_bf16(a):
    hi = a.astype(BF16)
    lo = (a - hi.astype(F32)).astype(BF16)
    return hi, lo


def _dot3(a, b):
    ah, al = _split_bf16(a)
    bh, bl = _split_bf16(b)
    return _dot(ah, bh) + (_dot(ah, bl) + _dot(al, bh))


def _four_w_kernel(cc_ref, sc_ref, w_ref, o_ref):
    w = w_ref[...]
    o_ref[:, :FOURIER_GROUP_DIM] = _dot3(cc_ref[...], w).astype(BF16)
    o_ref[:, FOURIER_GROUP_DIM:] = _dot3(sc_ref[...], w).astype(BF16)


def _four_w(cc, sc, w_four):
    depth, g, c, _ = w_four.shape
    tab = pl.BlockSpec((c, c), lambda l, i: (0, 0))
    return pl.pallas_call(
        _four_w_kernel,
        grid=(depth, g),
        in_specs=[tab, tab, pl.BlockSpec((None, None, c, c), lambda l, i: (l, i, 0, 0))],
        out_specs=pl.BlockSpec((None, None, c, 2 * c), lambda l, i: (l, i, 0, 0)),
        out_shape=jax.ShapeDtypeStruct((depth, g, c, 2 * c), BF16),
        compiler_params=_params("parallel", "parallel"),
        name="fourier_weights",
    )(cc, sc, w_four)


def _four_chan_kernel(f_ref, m_ref, *refs):
    out_refs, (yc_scr, ys_scr) = refs[:2 * DFT_RADIX], refs[2 * DFT_RADIX:]
    c = FOURIER_GROUP_DIM
    part_rows = f_ref.shape[0] // DFT_RADIX
    for g in range(FOURIER_GROUPS):
        cols = slice(g * c, (g + 1) * c)
        y = _dot(f_ref[:, cols], m_ref[g])
        yc_scr[...] = y[:, :c]
        ys_scr[...] = y[:, c:]
        for p in range(DFT_RADIX):
            out_refs[2 * p][:, cols] = yc_scr[pl.ds(p, part_rows, stride=DFT_RADIX), :].astype(BF16)
            out_refs[2 * p + 1][:, cols] = ys_scr[pl.ds(p, part_rows, stride=DFT_RADIX), :].astype(BF16)


def _four_chan(proj, mw, tm):
    t = proj.shape[0]
    f_blk = proj.shape[1] // FOURIER_WIDTH - 1
    out = pl.BlockSpec((tm // DFT_RADIX, FOURIER_WIDTH), lambda i: (i, 0))
    return pl.pallas_call(
        _four_chan_kernel,
        grid=(t // tm,),
        in_specs=[
            pl.BlockSpec((tm, FOURIER_WIDTH), lambda i: (i, f_blk)),
            pl.BlockSpec(mw.shape, lambda i: (0, 0, 0)),
        ],
        out_specs=[out] * (2 * DFT_RADIX),
        out_shape=[jax.ShapeDtypeStruct((t // DFT_RADIX, FOURIER_WIDTH), BF16)] * (2 * DFT_RADIX),
        scratch_shapes=[pltpu.VMEM((tm, FOURIER_GROUP_DIM), F32)] * 2,
        compiler_params=_params("parallel"),
        name="fourier_chan",
    )(proj, mw)


def _four_pos_kernel(c0, s0, c1, s1, c2, s2, c3, s3, ns1, nc3, a0, b0, a1, b1, a2, b2, a3, b3, o_ref):
    step = min(DFT_ROWS, c0.shape[0])
    for r0 in range(0, c0.shape[0], step):
        rows = slice(r0, r0 + step)
        dot = lambda table, data: _dot(table[rows, :], data[...])
        re0 = dot(c0, a0) + dot(s0, b0)
        re2 = dot(c2, a2) + dot(s2, b2)
        re13 = dot(c1, a1) + dot(s1, b1) + dot(c3, a3) + dot(s3, b3)
        im13 = dot(c1, b1) + dot(ns1, a1) + dot(nc3, b3) + dot(s3, a3)
        plus, minus = re0 + re2, re0 - re2
        o_ref[0, rows, :] = (plus + re13).astype(o_ref.dtype)
        o_ref[1, rows, :] = (minus + im13).astype(o_ref.dtype)
        o_ref[2, rows, :] = (plus - re13).astype(o_ref.dtype)
        o_ref[3, rows, :] = (minus - im13).astype(o_ref.dtype)


def _four_pos(tables, y_parts, b, tm, tn):
    data = [y.reshape(b, y.shape[0] // b, y.shape[1]) for y in y_parts]
    _, part, w = data[0].shape
    mat = pl.BlockSpec((tm, part), lambda i, bb, j: (i, 0), pipeline_mode=pl.Buffered(1))
    dat = pl.BlockSpec((None, part, tn), lambda i, bb, j: (bb, 0, j))
    out = pl.pallas_call(
        _four_pos_kernel,
        grid=(part // tm, b, w // tn),
        in_specs=[mat] * len(tables) + [dat] * len(data),
        out_specs=pl.BlockSpec((None, DFT_RADIX, tm, tn), lambda i, bb, j: (bb, 0, i, j)),
        out_shape=jax.ShapeDtypeStruct((b, DFT_RADIX, part, w), BF16),
        compiler_params=_params("parallel", "parallel", "parallel"),
        name="fourier_pos",
    )(*tables, *data)
    return out.reshape(b, DFT_RADIX * part, w)


def _dft_tables(n, split):
    part = n // DFT_RADIX
    j = jnp.arange(part, dtype=jnp.int32)[:, None]
    hi = jnp.arange(part // split, dtype=jnp.int32)[None, :] * split
    lo = jnp.arange(split, dtype=jnp.int32)[None, :]
    ang = lambda k: (2.0 * np.pi / part) * ((j * k) % part).astype(F32)
    ch, sh, cl, sl = jnp.cos(ang(hi)), jnp.sin(ang(hi)), jnp.cos(ang(lo)), jnp.sin(ang(lo))
    c0 = (ch[:, :, None] * cl[:, None, :] - sh[:, :, None] * sl[:, None, :]).reshape(part, part)
    s0 = (sh[:, :, None] * cl[:, None, :] + ch[:, :, None] * sl[:, None, :]).reshape(part, part)
    tables = [c0, s0]
    for p in range(1, DFT_RADIX):
        phi = (2.0 * np.pi * p / n) * j.astype(F32)
        cp, sp = jnp.cos(phi), jnp.sin(phi)
        tables += [c0 * cp - s0 * sp, s0 * cp + c0 * sp]
    tables += [-tables[3], -tables[6]]
    return tuple(t.astype(BF16) for t in tables)


def _outproj_kernel(a_ref, f_ref, w_ref, x_ref, gpost_ref, gate_ref, gpre_ref, sh_ref, sc_ref,
                    xo_ref, h_ref):
    mix = _dot(a_ref[...], w_ref[:NA_WIDTH, :]) + _dot(f_ref[...], w_ref[NA_WIDTH:, :])
    x = x_ref[...] + gate_ref[...] * _rms(mix, gpost_ref[...])
    xo_ref[...] = x
    h_ref[...] = _norm_mod(x, gpre_ref[...], sh_ref[...], sc_ref[...]).astype(BF16)


def _outproj(attn, four, w, x, g_post, g_pre, mods, row_fn, tm):
    t, d = x.shape
    vec = pl.BlockSpec((1, d), lambda i: (0, 0))
    row = pl.BlockSpec((tm, d), lambda i: (i, 0))
    half = pl.BlockSpec((tm, NA_WIDTH), lambda i: (i, 0))
    return pl.pallas_call(
        _outproj_kernel,
        grid=(t // tm,),
        in_specs=[half, half, pl.BlockSpec(w.shape, lambda i: (0, 0), pipeline_mode=pl.Buffered(1)), row, vec,
                  _mod_spec(2, row_fn), vec, _mod_spec(3, row_fn), _mod_spec(4, row_fn)],
        out_specs=[row, row],
        out_shape=[jax.ShapeDtypeStruct((t, d), F32), jax.ShapeDtypeStruct((t, d), BF16)],
        compiler_params=_params("parallel"),
        name="outproj_residual",
    )(attn, four, w, x, g_post.reshape(1, d), mods, g_pre.reshape(1, d), mods, mods)


def _matmul_kernel(x_ref, w_ref, o_ref):
    o_ref[...] = _dot(x_ref[...], w_ref[...]).astype(o_ref.dtype)


def _matmul(x, w, tn, out_dtype, name):
    m, k = x.shape
    n = w.shape[1]
    return pl.pallas_call(
        _matmul_kernel,
        grid=(n // tn,),
        in_specs=[pl.BlockSpec((m, k), lambda j: (0, 0)), pl.BlockSpec((k, tn), lambda j: (0, j))],
        out_specs=pl.BlockSpec((m, tn), lambda j: (0, j)),
        out_shape=jax.ShapeDtypeStruct((m, n), out_dtype),
        compiler_params=_params("parallel"),
        name=name,
    )(x, w)


def _ffn_kernel(h_ref, wa_ref, wg_ref, edge_ref, cp_ref, wd_ref, x_ref, gate_ref, gpost_ref, o_ref,
                acc_ref, ua0, ug0, ua1, ug1, act0, act1,
                *, tm, n_ff_tiles, n_row_tiles):
    i = pl.program_id(0)
    j = pl.program_id(1)
    nj = n_ff_tiles
    pad = F32_SUBLANES
    tf = wa_ref.shape[1]
    u_refs = ((ua0, ug0), (ua1, ug1))
    act_refs = (act0, act1)

    def up_proj_part(half, k0):
        w_ref = (wa_ref, wg_ref)[half]
        return _dot(h_ref[:, k0:k0 + UP_K], w_ref[k0:k0 + UP_K, :])

    def up_proj_store(parity, half, u):
        u_ref = u_refs[parity][half]
        u_ref[0:pad, :] = jnp.broadcast_to(edge_ref[half:half + 1, :], (pad, tf))
        u_ref[pad:pad + tm, :] = u
        u_ref[pad + tm:, :] = jnp.broadcast_to(edge_ref[2 + half:3 + half, :], (pad, tf))

    def conv(u_ref, r0, cols, p0):
        win = u_ref[r0:r0 + CONV_ROWS + 2 * pad, cols]
        mid = win[pad:pad + CONV_ROWS]
        down = pltpu.roll(win, 1, axis=0)[pad:pad + CONV_ROWS]
        up = pltpu.roll(win, CONV_ROWS + 2 * pad - 1, axis=0)[pad:pad + CONV_ROWS]
        return (down * cp_ref[p0:p0 + 1, cols] + mid * cp_ref[p0 + 1:p0 + 2, cols]
                + up * cp_ref[p0 + 2:p0 + 3, cols] + cp_ref[p0 + 3:p0 + 4, cols])

    def conv_gate(parity, r0, c0):
        ua_ref, ug_ref = u_refs[parity]
        cols = slice(c0, c0 + LANES)
        a = conv(ua_ref, r0, cols, 0)
        g = conv(ug_ref, r0, cols, CONV_TAPS + 1)
        act_refs[parity][r0:r0 + CONV_ROWS, cols] = (g * jax.nn.sigmoid(g) * a).astype(BF16)

    def down_proj(parity, cols):
        return _dot(act_refs[parity][...], wd_ref[:, cols])

    chunks = [(r0, c0) for r0 in range(0, tm, CONV_ROWS) for c0 in range(0, tf, LANES)]
    d_out = wd_ref.shape[1]
    down_groups = [slice(c, c + DOWN_COLS) for c in range(0, d_out, DOWN_COLS)]

    def finish_rows(r0):
        rows = slice(r0, r0 + FINISH_ROWS)
        o_ref[rows, :] = x_ref[rows, :] + gate_ref[...] * _rms(acc_ref[rows, :], gpost_ref[...])
        acc_ref[rows, :] = jnp.zeros((FINISH_ROWS, d_out), F32)

    def run_step(up=None, conv_of=None, down=None, finish=False):
        pieces = []
        if down is not None:
            for cols in down_groups:
                def piece(cols=cols):
                    acc_ref[:, cols] += down_proj(down, cols)
                pieces.append((piece, tf * DOWN_COLS))
        n_down = len(pieces)
        if up is not None:
            d_in = h_ref.shape[1]
            for half in range(2):
                partial_u = []
                for k0 in range(0, d_in, UP_K):
                    def piece(half=half, k0=k0, partial_u=partial_u):
                        part = up_proj_part(half, k0)
                        partial_u[:] = [part if not partial_u else partial_u[0] + part]
                        if k0 + UP_K == d_in:
                            up_proj_store(up, half, partial_u[0])
                    pieces.append((piece, UP_K * tf))
        conv_todo = [functools.partial(conv_gate, conv_of, *c) for c in chunks] if conv_of is not None else []
        fin_todo = [functools.partial(finish_rows, r0) for r0 in range(0, tm, FINISH_ROWS)] if finish else []
        n_conv, n_fin = len(conv_todo), len(fin_todo)
        total = sum(w for _, w in pieces)
        total_up = sum(w for _, w in pieces[n_down:])
        done = done_up = 0.0
        for n, (piece, w) in enumerate(pieces):
            piece()
            done += w
            is_last = n == len(pieces) - 1
            conv_upto = n_conv if is_last else round(n_conv * done / total)
            while n_conv - len(conv_todo) < conv_upto:
                conv_todo.pop(0)()
            if n >= n_down:
                done_up += w
                fin_upto = n_fin if is_last else round(n_fin * done_up / total_up)
                while n_fin - len(fin_todo) < fin_upto:
                    fin_todo.pop(0)()
        for chunk in conv_todo + fin_todo:
            chunk()

    s = i * nj + j
    last = n_row_tiles * nj
    finishing = (j == 1) & (i >= 1)

    @pl.when(s == 0)
    def _():
        acc_ref[...] = jnp.zeros_like(acc_ref)
        run_step(up=0)

    @pl.when(s == 1)
    def _():
        run_step(up=1, conv_of=0)

    for parity in range(2):
        steady = (s >= 2) & (s < last) & (s % 2 == parity)

        @pl.when(steady & jnp.logical_not(finishing))
        def _():
            run_step(up=parity, conv_of=1 - parity, down=parity)

        @pl.when(steady & finishing)
        def _():
            run_step(up=parity, conv_of=1 - parity, down=parity, finish=True)

    @pl.when(s == last)
    def _():
        run_step(conv_of=(last - 1) % 2, down=last % 2)

    @pl.when(s == last + 1)
    def _():
        run_step(down=(last + 1) % 2, finish=True)


def _ffn(h, w_up, conv_w, conv_b, w_down, x, g_post, mods, row_fn, seq_len, tm):
    t, d = x.shape
    ff = w_down.shape[0]
    tf = FF_TILE
    nj = ff // tf
    n_tiles = t // tm
    assert seq_len % tm == 0

    tile0 = np.arange(n_tiles) * tm
    start = tile0 % seq_len == 0
    end = (tile0 + tm) % seq_len == 0
    prev = jnp.take(h, jnp.asarray(np.maximum(tile0 - 1, 0)), axis=0)
    nxt = jnp.take(h, jnp.asarray(np.minimum(tile0 + tm, t - 1)), axis=0)
    prev = jnp.where(jnp.asarray(start)[:, None], jnp.zeros_like(prev), prev)
    nxt = jnp.where(jnp.asarray(end)[:, None], jnp.zeros_like(nxt), nxt)
    edge_rows = jnp.concatenate([prev, nxt], axis=0)
    pad = -edge_rows.shape[0] % BF16_SUBLANES
    edge_rows = jnp.pad(edge_rows, ((0, pad), (0, 0)))
    u_edge = _matmul(edge_rows, w_up, 2 * ff // 8, F32, "conv_ffn_edges")
    u_prev, u_next = u_edge[:n_tiles], u_edge[n_tiles:2 * n_tiles]
    edges = jnp.stack([u_prev[:, :ff], u_prev[:, ff:], u_next[:, :ff], u_next[:, ff:]], axis=1)
    edges = edges.reshape(n_tiles, 4, nj, tf).transpose(0, 2, 1, 3)
    conv_p = jnp.concatenate([conv_w[:, :ff], conv_b[None, :ff], conv_w[:, ff:], conv_b[None, ff:]], axis=0)
    conv_p = conv_p.reshape(2 * (CONV_TAPS + 1), nj, tf).transpose(1, 0, 2)

    assert nj >= 3
    vec = pl.BlockSpec((1, d), lambda i, j: (0, 0))
    cur = lambda i: jnp.minimum(i, n_tiles - 1)
    fin = lambda i, j, switch: jnp.where(j <= switch, jnp.maximum(i - 1, 0), cur(i))
    lag = lambda j, k: (j + nj - k) % nj
    col = lambda rows, off, k: pl.BlockSpec((rows, tf), lambda i, j: (0, off + lag(j, k)))
    u_buf = pltpu.VMEM((tm + 2 * F32_SUBLANES, tf), F32)
    act_buf = pltpu.VMEM((tm, tf), BF16)
    return pl.pallas_call(
        functools.partial(_ffn_kernel, tm=tm, n_ff_tiles=nj, n_row_tiles=n_tiles),
        grid=(n_tiles + 1, nj),
        in_specs=[
            pl.BlockSpec((tm, d), lambda i, j: (cur(i), 0)),
            col(d, 0, 0), col(d, nj, 0),
            pl.BlockSpec((None, None, 4, tf), lambda i, j: (cur(i), j, 0, 0)),
            pl.BlockSpec((None, 2 * (CONV_TAPS + 1), tf), lambda i, j: (lag(j, 1), 0, 0)),
            pl.BlockSpec((tf, d), lambda i, j: (lag(j, 2), 0)),
            pl.BlockSpec((tm, d), lambda i, j: (fin(i, j, 1), 0)),
            pl.BlockSpec((None, None, 1, d), lambda i, j: (row_fn(fin(i, j, 1)), 5, 0, 0)),
            vec,
        ],
        out_specs=pl.BlockSpec((tm, d), lambda i, j: (fin(i, j, 3), 0)),
        out_shape=jax.ShapeDtypeStruct((t, d), F32),
        scratch_shapes=[pltpu.VMEM((tm, d), F32), u_buf, u_buf, u_buf, u_buf, act_buf, act_buf],
        compiler_params=_params("arbitrary", "arbitrary"),
        name="conv_ffn",
    )(h, w_up, w_up, edges, conv_p, w_down, x, mods,
      g_post.reshape(1, d))


def kernel(x, c, ctx, c_ctx, w_ada, b_ada, g_pre_mix, w_in, rpb, w_four, w_out, g_post_mix,
           g_pre_ffn, w_up, conv_w, conv_b, w_down, g_post_ffn):
    b, n, d = x.shape
    n_ctx = ctx.shape[1]
    assert (d, n, n_ctx, b) == (D_MODEL, GRID_W * GRID_W, CTX_LEN, CTX_MOD_ROW)
    t_lat, t_ctx = b * n, b * n_ctx
    tm_lat, tm_ctx = 512, CTX_LEN
    lat_row = lambda i: (i * tm_lat) // n
    ctx_row = lambda i: CTX_MOD_ROW

    cond = jnp.zeros((MOD_ROWS, d), F32).at[:b].set(c).at[CTX_MOD_ROW].set(c_ctx)
    mods_all = _mods(cond, w_ada, b_ada)

    scale = (n * FOURIER_GROUP_DIM) ** -0.5
    scale_ctx = (n_ctx * FOURIER_GROUP_DIM) ** -0.5
    kk = np.outer(np.arange(FOURIER_GROUP_DIM), np.arange(FOURIER_GROUP_DIM)) % FOURIER_GROUP_DIM
    ang = 2.0 * np.pi * kk / FOURIER_GROUP_DIM
    cc, sc = np.cos(ang), -np.sin(ang)
    mw_lat = _four_w(jnp.asarray(cc * scale, F32), jnp.asarray(sc * scale, F32), w_four)
    dft_lat = _dft_tables(n, GRID_W)

    xt = x.reshape(t_lat, d)
    ct = ctx.reshape(t_ctx, d)
    for l in range(DEPTH):
        last = l == DEPTH - 1
        mods = mods_all[l].reshape(MOD_ROWS, N_MOD, 1, d)
        w_in_l = w_in[l].astype(BF16)
        w_out_l = w_out[l].astype(BF16)
        w_up_l = w_up[l].astype(BF16)
        w_down_l = w_down[l].astype(BF16)

        proj = _inproj(xt, g_pre_mix[l], mods, lat_row, w_in_l, tm_lat)
        if last:
            cproj = _inproj(ct, g_pre_mix[l], mods, ctx_row, w_in_l[:, NA_WIDTH:3 * NA_WIDTH], tm_ctx)
            kc_blk, vc_blk = 0, N_PAIRS
        else:
            cproj = _inproj(ct, g_pre_mix[l], mods, ctx_row, w_in_l, tm_ctx)
            kc_blk, vc_blk = N_PAIRS, 2 * N_PAIRS
        cproj3 = cproj.reshape(b, n_ctx, cproj.shape[1])

        attn = _attn(proj.reshape(b, n, PROJ_WIDTH), cproj3, _attn_bias_slabs(rpb[l]), kc_blk, vc_blk)
        four = _four_pos(dft_lat, _four_chan(proj, mw_lat[l], 1024), b, 1024, 512)
        xt, h2 = _outproj(attn.reshape(t_lat, NA_WIDTH), four.reshape(t_lat, FOURIER_WIDTH), w_out_l, xt,
                          g_post_mix[l], g_pre_ffn[l], mods, lat_row, tm_lat)
        xt = _ffn(h2, w_up_l, conv_w[l], conv_b[l], w_down_l, xt, g_post_ffn[l], mods, lat_row, n, tm_lat)

        if not last:
            mw_ctx = _four_w(jnp.asarray(cc * scale_ctx, F32), jnp.asarray(sc * scale_ctx, F32),
                             w_four[l:l + 1])[0]
            dft_ctx = _dft_tables(n_ctx, 16)
            attn_c = _ctx_attn(cproj3)
            four_c = _four_pos(dft_ctx, _four_chan(cproj, mw_ctx, 512), b, n_ctx // DFT_RADIX, 512)
            ct, hc2 = _outproj(attn_c.reshape(t_ctx, NA_WIDTH), four_c.reshape(t_ctx, FOURIER_WIDTH),
                               w_out_l, ct, g_post_mix[l], g_pre_ffn[l], mods, ctx_row, tm_ctx)
            ct = _ffn(hc2, w_up_l, conv_w[l], conv_b[l], w_down_l, ct, g_post_ffn[l], mods, ctx_row,
                      n_ctx, tm_ctx)
    return xt.reshape(b, n, d)
```

```python
import functools

import jax
import jax.numpy as jnp
import numpy as np
from jax import lax
from jax.experimental import pallas as pl
from jax.experimental.pallas import tpu as pltpu

D_MODEL = 2048
DEPTH = 2
GRID_W = 64
CTX_LEN = 256
NA_HEADS = 16
HEAD_DIM = 64
NA_WIDTH = NA_HEADS * HEAD_DIM
FOURIER_GROUPS = 8
FOURIER_GROUP_DIM = 128
FOURIER_WIDTH = FOURIER_GROUPS * FOURIER_GROUP_DIM
PROJ_WIDTH = 3 * NA_WIDTH + FOURIER_WIDTH
WIN_ROWS = 8
WIN_COLS = 16
D_FF = 5632
N_MOD = 6
EPS = 1e-6
ATTN_SCALE = HEAD_DIM ** -0.5
MOD_ROWS = 16
CTX_MOD_ROW = 8

LANES = 128
DFT_RADIX = 4
DFT_ROWS = 512
FF_TILE = 512
CONV_TAPS = 3
FINISH_ROWS = 64
UP_K = 512
DOWN_COLS = 512
PROJ_COLS = 512
NORM_ROWS = 16
CONV_ROWS = 64
F32_SUBLANES = 8
BF16_SUBLANES = 16
HEAD_PAIR = LANES // HEAD_DIM
N_PAIRS = NA_HEADS // HEAD_PAIR
Q_ROWS = 4
K_ROWS = Q_ROWS + WIN_ROWS - 1
TQ = Q_ROWS * GRID_W
NK = K_ROWS * GRID_W
ATTN_UNROLL = 4
NEG = -1e30
VMEM_LIMIT = 56 * 1024 * 1024

F32 = jnp.float32
BF16 = jnp.bfloat16


def _params(*sem, flags=None):
    return pltpu.CompilerParams(dimension_semantics=sem, vmem_limit_bytes=VMEM_LIMIT, flags=flags)


def _dot(a, b):
    return jnp.dot(a, b, preferred_element_type=F32)


def _dot_nt(a, b):
    return lax.dot_general(a, b, (((1,), (1,)), ((), ())), preferred_element_type=F32)


def _rms(x, g):
    ms = jnp.mean(x * x, axis=-1, keepdims=True)
    return x * lax.rsqrt(ms + EPS) * g


def _norm_mod(x, g, shift, scale):
    return _rms(x, g) * (1.0 + scale) + shift


def _mods_kernel(cond_ref, w_ref, b_ref, o_ref):
    c = cond_ref[...]
    s = (c * jax.nn.sigmoid(c)).astype(BF16)
    o_ref[...] = _dot(s, w_ref[...].astype(BF16)) + b_ref[...]


def _mods(cond, w_ada, b_ada, tn=1024):
    depth, d, n = w_ada.shape
    return pl.pallas_call(
        _mods_kernel,
        grid=(depth, n // tn),
        in_specs=[
            pl.BlockSpec((MOD_ROWS, d), lambda l, j: (0, 0)),
            pl.BlockSpec((None, d, tn), lambda l, j: (l, 0, j)),
            pl.BlockSpec((None, 1, tn), lambda l, j: (l, 0, j)),
        ],
        out_specs=pl.BlockSpec((None, MOD_ROWS, tn), lambda l, j: (l, 0, j)),
        out_shape=jax.ShapeDtypeStruct((depth, MOD_ROWS, n), F32),
        compiler_params=_params("parallel", "parallel"),
        name="adaln_mods",
    )(cond, w_ada, b_ada.reshape(depth, 1, n))


def _wspec(w, block, index_fn, **kw):
    _, layer = w
    return pl.BlockSpec((None,) + tuple(block), lambda *idx: (layer,) + tuple(index_fn(*idx)), **kw)


def _mod_spec(which, row_fn):
    return pl.BlockSpec((None, None, 1, D_MODEL),
                        lambda i, *_: (row_fn(i), which, 0, 0))


def _interleave(pieces, chunks):
    if not pieces:
        for chunk in chunks:
            chunk()
        return
    per = -(-len(chunks) // len(pieces))
    for n, piece in enumerate(pieces):
        piece()
        for chunk in chunks[n * per:(n + 1) * per]:
            chunk()


def _two_stage(i, n_tiles, run):
    @pl.when(i == 0)
    def _():
        run(0, None)

    for parity in range(2):
        @pl.when((i >= 1) & (i < n_tiles) & (i % 2 == parity))
        def _():
            run(parity, 1 - parity)

    @pl.when(i == n_tiles)
    def _():
        run(None, (n_tiles - 1) % 2)


def _inproj_kernel(x_ref, g_ref, sh_ref, sc_ref, w_ref, o_ref, h0_ref, h1_ref, *, n_tiles):
    h_refs = (h0_ref, h1_ref)
    tm = x_ref.shape[0]

    def run(norm_to, dot_from):
        pieces, chunks = [], []
        if dot_from is not None:
            for c0 in range(0, w_ref.shape[1], PROJ_COLS):
                def piece(cols=slice(c0, c0 + PROJ_COLS)):
                    o_ref[:, cols] = _dot(h_refs[dot_from][...], w_ref[:, cols]).astype(o_ref.dtype)
                pieces.append(piece)
        if norm_to is not None:
            for r0 in range(0, tm, NORM_ROWS):
                def chunk(rows=slice(r0, r0 + NORM_ROWS)):
                    h_refs[norm_to][rows, :] = _norm_mod(
                        x_ref[rows, :], g_ref[...], sh_ref[...], sc_ref[...]).astype(BF16)
                chunks.append(chunk)
        _interleave(pieces, chunks)

    _two_stage(pl.program_id(0), n_tiles, run)


def _inproj(x, g, mods, row_fn, w, tm):
    t, d = x.shape
    n = w[0].shape[2]
    n_tiles = t // tm
    cur = lambda i: jnp.minimum(i, n_tiles - 1)
    prev = lambda i: jnp.maximum(i - 1, 0)
    return pl.pallas_call(
        functools.partial(_inproj_kernel, n_tiles=n_tiles),
        grid=(n_tiles + 1,),
        in_specs=[
            pl.BlockSpec((tm, d), lambda i: (cur(i), 0)),
            pl.BlockSpec((1, d), lambda i: (0, 0)),
            _mod_spec(0, lambda i: row_fn(cur(i))),
            _mod_spec(1, lambda i: row_fn(cur(i))),
            _wspec(w, (d, n), lambda i: (0, 0), pipeline_mode=pl.Buffered(1)),
        ],
        out_specs=pl.BlockSpec((tm, n), lambda i: (prev(i), 0)),
        out_shape=jax.ShapeDtypeStruct((t, n), BF16),
        scratch_shapes=[pltpu.VMEM((tm, d), BF16)] * 2,
        compiler_params=_params("arbitrary"),
        name="norm_inproj",
    )(x, g.reshape(1, d), mods, mods, w[0])


def _softmax_pv(parts):
    m = functools.reduce(jnp.maximum, [s.max(axis=-1, keepdims=True) for s, _ in parts])
    num = 0.0
    den = 0.0
    for s, v in parts:
        p = jnp.exp(s - m)
        den = den + p.sum(axis=-1, keepdims=True)
        num = num + _dot(p.astype(BF16), v)
    return num / den


def _head_masks():
    lane = lax.broadcasted_iota(jnp.int32, (1, LANES), 1)
    first = lane < HEAD_DIM
    return first, [first, jnp.logical_not(first)]


def _attn_kernel(q_ref, k_ref, v_ref, kc_ref, vc_ref, slab_ref, o_ref, tab_ref, *, n_tiles, rows):
    first, sels = _head_masks()
    kc = kc_ref[...]
    vc = vc_ref[...]

    @pl.when(pl.program_id(1) == 0)
    def _():
        pattern = _attn_slab_pattern(rows)
        for p in range(pattern.shape[0]):
            for hh in range(HEAD_PAIR):
                for qr in range(Q_ROWS):
                    for kr in range(K_ROWS):
                        tab_ref[p, hh, qr * GRID_W:(qr + 1) * GRID_W, kr * GRID_W:(kr + 1) * GRID_W] = (
                            slab_ref[hh, int(pattern[p, qr, kr])])

    def scores(t, hh):
        q0 = pl.multiple_of(t * TQ, TQ)
        k0 = pl.multiple_of(jnp.clip(t * Q_ROWS - WIN_ROWS // 2, 0, rows - K_ROWS) * GRID_W, GRID_W)
        pat = jnp.where(t == 0, 0, jnp.where(t == n_tiles - 1, 2, 1))
        q = q_ref[pl.ds(q0, TQ), :] * ATTN_SCALE
        qm = jnp.where(sels[hh], q, jnp.zeros_like(q))
        s_loc = _dot_nt(qm, k_ref[pl.ds(k0, NK), :]) + tab_ref[pat, hh]
        s_ctx = _dot_nt(qm, kc)
        return k0, s_loc, s_ctx

    def tile_group(tg, carry):
        units = [(tg * ATTN_UNROLL + ti, hh) for ti in range(ATTN_UNROLL) for hh in range(HEAD_PAIR)]
        pending = None
        outs = {}
        for unit in units + [None]:
            nxt = scores(*unit) if unit is not None else None
            if pending is not None:
                (t, hh), (k0, s_loc, s_ctx) = pending
                outs[hh] = _softmax_pv([(s_loc, v_ref[pl.ds(k0, NK), :]), (s_ctx, vc)])
                if hh == HEAD_PAIR - 1:
                    q0 = pl.multiple_of(t * TQ, TQ)
                    o_ref[pl.ds(q0, TQ), :] = jnp.where(first, outs[0], outs[1]).astype(o_ref.dtype)
            pending = (unit, nxt)
        return carry

    lax.fori_loop(0, n_tiles // ATTN_UNROLL, tile_group, 0)


def _attn(proj, cproj, slabs, kc_blk, vc_blk):
    b, n, _ = proj.shape
    rows = n // GRID_W
    n_tiles = rows // Q_ROWS
    lat = lambda off: pl.BlockSpec((None, n, LANES), lambda p, i: (i, 0, off + p))
    cx = lambda off: pl.BlockSpec((None, CTX_LEN, LANES), lambda p, i: (i, 0, off + p))
    return pl.pallas_call(
        functools.partial(_attn_kernel, n_tiles=n_tiles, rows=rows),
        grid=(N_PAIRS, b),
        in_specs=[
            lat(0), lat(N_PAIRS), lat(2 * N_PAIRS), cx(kc_blk), cx(vc_blk),
            pl.BlockSpec((HEAD_PAIR,) + slabs.shape[1:], lambda p, i: (p, 0, 0, 0)),
        ],
        out_specs=pl.BlockSpec((None, n, LANES), lambda p, i: (i, 0, p)),
        out_shape=jax.ShapeDtypeStruct((b, n, NA_WIDTH), BF16),
        scratch_shapes=[pltpu.VMEM((3, HEAD_PAIR, TQ, NK), F32)],
        compiler_params=_params("arbitrary", "arbitrary"),
        name="nbr_attn",
    )(proj, proj, proj, cproj, cproj, slabs)


def _ctx_attn_kernel(q_ref, k_ref, v_ref, o_ref):
    first, sels = _head_masks()
    q = q_ref[...] * ATTN_SCALE
    k = k_ref[...]
    v = v_ref[...]
    outs = []
    for hh in range(HEAD_PAIR):
        qm = jnp.where(sels[hh], q, jnp.zeros_like(q))
        outs.append(_softmax_pv([(_dot_nt(qm, k), v)]))
    o_ref[...] = jnp.where(first, outs[0], outs[1]).astype(o_ref.dtype)


def _ctx_attn(cproj):
    b, n, _ = cproj.shape
    spec = lambda off: pl.BlockSpec((None, n, LANES), lambda i, p: (i, 0, off + p))
    return pl.pallas_call(
        _ctx_attn_kernel,
        grid=(b, N_PAIRS),
        in_specs=[spec(0), spec(N_PAIRS), spec(2 * N_PAIRS)],
        out_specs=spec(0),
        out_shape=jax.ShapeDtypeStruct((b, n, NA_WIDTH), BF16),
        compiler_params=_params("parallel", "parallel"),
        name="ctx_attn",
    )(cproj, cproj, cproj)


MASKED_SLAB = 2 * WIN_ROWS - 1


def _attn_bias_slabs(rpb):
    qc = np.arange(GRID_W)[:, None]
    kc = np.arange(GRID_W)[None, :]
    cs = np.clip(qc - WIN_COLS // 2, 0, GRID_W - WIN_COLS)
    col_ok = (kc >= cs) & (kc < cs + WIN_COLS)
    dc = kc - qc + WIN_COLS - 1
    onehot = np.zeros((2 * WIN_COLS - 1, GRID_W, GRID_W), np.float32)
    qi, ki = np.nonzero(col_ok)
    onehot[dc[qi, ki], qi, ki] = 1.0
    t1 = jnp.einsum('hrd,dqk->hrqk', rpb, jnp.asarray(onehot), precision=lax.Precision.HIGHEST)
    t1 = jnp.where(jnp.asarray(col_ok)[None, None], t1, NEG)
    neg_slab = jnp.full((NA_HEADS, 1, GRID_W, GRID_W), NEG, F32)
    return jnp.concatenate([t1, neg_slab], axis=1)


def _attn_slab_pattern(rows):
    n_tiles = rows // Q_ROWS
    idx = np.full((3, Q_ROWS, K_ROWS), MASKED_SLAB, np.int32)
    for p, t in enumerate((0, 1, n_tiles - 1)):
        k_start = int(np.clip(t * Q_ROWS - WIN_ROWS // 2, 0, rows - K_ROWS))
        for qr in range(Q_ROWS):
            r = t * Q_ROWS + qr
            rs = int(np.clip(r - WIN_ROWS // 2, 0, rows - WIN_ROWS))
            for kr in range(K_ROWS):
                krow = k_start + kr
                if rs <= krow < rs + WIN_ROWS:
                    idx[p, qr, kr] = krow - r + WIN_ROWS - 1
    return idx


def _split_bf16(a):
    hi = a.astype(BF16)
    lo = (a - hi.astype(F32)).astype(BF16)
    return hi, lo


def _dot3(a, b):
    ah, al = _split_bf16(a)
    bh, bl = _split_bf16(b)
    return _dot(ah, bh) + (_dot(ah, bl) + _dot(al, bh))


def _four_w_kernel(cc_ref, sc_ref, w_ref, o_ref):
    w = w_ref[...]
    o_ref[:, :FOURIER_GROUP_DIM] = _dot3(cc_ref[...], w).astype(BF16)
    o_ref[:, FOURIER_GROUP_DIM:] = _dot3(sc_ref[...], w).astype(BF16)


def _four_w(cc, sc, w_four):
    depth, g, c, _ = w_four.shape
    tab = pl.BlockSpec((c, c), lambda l, i: (0, 0))
    return pl.pallas_call(
        _four_w_kernel,
        grid=(depth, g),
        in_specs=[tab, tab, pl.BlockSpec((None, None, c, c), lambda l, i: (l, i, 0, 0))],
        out_specs=pl.BlockSpec((None, None, c, 2 * c), lambda l, i: (l, i, 0, 0)),
        out_shape=jax.ShapeDtypeStruct((depth, g, c, 2 * c), BF16),
        compiler_params=_params("parallel", "parallel"),
        name="fourier_weights",
    )(cc, sc, w_four)


def _four_chan_kernel(f_ref, m_ref, *refs):
    out_refs, (yc_scr, ys_scr) = refs[:2 * DFT_RADIX], refs[2 * DFT_RADIX:]
    c = FOURIER_GROUP_DIM
    part_rows = f_ref.shape[0] // DFT_RADIX
    for g in range(FOURIER_GROUPS):
        cols = slice(g * c, (g + 1) * c)
        y = _dot(f_ref[:, cols], m_ref[g])
        yc_scr[...] = y[:, :c]
        ys_scr[...] = y[:, c:]
        for p in range(DFT_RADIX):
            out_refs[2 * p][:, cols] = yc_scr[pl.ds(p, part_rows, stride=DFT_RADIX), :].astype(BF16)
            out_refs[2 * p + 1][:, cols] = ys_scr[pl.ds(p, part_rows, stride=DFT_RADIX), :].astype(BF16)


def _four_chan(proj, mw, tm):
    t = proj.shape[0]
    f_blk = proj.shape[1] // FOURIER_WIDTH - 1
    out = pl.BlockSpec((tm // DFT_RADIX, FOURIER_WIDTH), lambda i: (i, 0))
    return pl.pallas_call(
        _four_chan_kernel,
        grid=(t // tm,),
        in_specs=[
            pl.BlockSpec((tm, FOURIER_WIDTH), lambda i: (i, f_blk)),
            pl.BlockSpec(mw.shape, lambda i: (0, 0, 0)),
        ],
        out_specs=[out] * (2 * DFT_RADIX),
        out_shape=[jax.ShapeDtypeStruct((t // DFT_RADIX, FOURIER_WIDTH), BF16)] * (2 * DFT_RADIX),
        scratch_shapes=[pltpu.VMEM((tm, FOURIER_GROUP_DIM), F32)] * 2,
        compiler_params=_params("parallel"),
        name="fourier_chan",
    )(proj, mw)


def _four_pos_kernel(c0, s0, c1, s1, c2, s2, c3, s3, ns1, nc3, a0, b0, a1, b1, a2, b2, a3, b3, o_ref):
    step = min(DFT_ROWS, c0.shape[0])
    for r0 in range(0, c0.shape[0], step):
        rows = slice(r0, r0 + step)
        dot = lambda table, data: _dot(table[rows, :], data[...])
        re0 = dot(c0, a0) + dot(s0, b0)
        re2 = dot(c2, a2) + dot(s2, b2)
        re13 = dot(c1, a1) + dot(s1, b1) + dot(c3, a3) + dot(s3, b3)
        im13 = dot(c1, b1) + dot(ns1, a1) + dot(nc3, b3) + dot(s3, a3)
        plus, minus = re0 + re2, re0 - re2
        o_ref[0, rows, :] = (plus + re13).astype(o_ref.dtype)
        o_ref[1, rows, :] = (minus + im13).astype(o_ref.dtype)
        o_ref[2, rows, :] = (plus - re13).astype(o_ref.dtype)
        o_ref[3, rows, :] = (minus - im13).astype(o_ref.dtype)


def _four_pos(tables, y_parts, b, tm, tn):
    data = [y.reshape(b, y.shape[0] // b, y.shape[1]) for y in y_parts]
    _, part, w = data[0].shape
    mat = pl.BlockSpec((tm, part), lambda i, bb, j: (i, 0), pipeline_mode=pl.Buffered(1))
    dat = pl.BlockSpec((None, part, tn), lambda i, bb, j: (bb, 0, j))
    out = pl.pallas_call(
        _four_pos_kernel,
        grid=(part // tm, b, w // tn),
        in_specs=[mat] * len(tables) + [dat] * len(data),
        out_specs=pl.BlockSpec((None, DFT_RADIX, tm, tn), lambda i, bb, j: (bb, 0, i, j)),
        out_shape=jax.ShapeDtypeStruct((b, DFT_RADIX, part, w), BF16),
        compiler_params=_params("parallel", "parallel", "parallel"),
        name="fourier_pos",
    )(*tables, *data)
    return out.reshape(b, DFT_RADIX * part, w)


def _dft_tables(n, split):
    part = n // DFT_RADIX
    j = jnp.arange(part, dtype=jnp.int32)[:, None]
    hi = jnp.arange(part // split, dtype=jnp.int32)[None, :] * split
    lo = jnp.arange(split, dtype=jnp.int32)[None, :]
    ang = lambda k: (2.0 * np.pi / part) * ((j * k) % part).astype(F32)
    ch, sh, cl, sl = jnp.cos(ang(hi)), jnp.sin(ang(hi)), jnp.cos(ang(lo)), jnp.sin(ang(lo))
    c0 = (ch[:, :, None] * cl[:, None, :] - sh[:, :, None] * sl[:, None, :]).reshape(part, part)
    s0 = (sh[:, :, None] * cl[:, None, :] + ch[:, :, None] * sl[:, None, :]).reshape(part, part)
    tables = [c0, s0]
    for p in range(1, DFT_RADIX):
        phi = (2.0 * np.pi * p / n) * j.astype(F32)
        cp, sp = jnp.cos(phi), jnp.sin(phi)
        tables += [c0 * cp - s0 * sp, s0 * cp + c0 * sp]
    tables += [-tables[3], -tables[6]]
    return tuple(t.astype(BF16) for t in tables)


def _outproj_kernel(a_ref, f_ref, w_ref, x_ref, gpost_ref, gate_ref, gpre_ref, sh_ref, sc_ref,
                    xo_ref, h_ref):
    mix = _dot(a_ref[...], w_ref[:NA_WIDTH, :]) + _dot(f_ref[...], w_ref[NA_WIDTH:, :])
    x = x_ref[...] + gate_ref[...] * _rms(mix, gpost_ref[...])
    xo_ref[...] = x
    h_ref[...] = _norm_mod(x, gpre_ref[...], sh_ref[...], sc_ref[...]).astype(BF16)


def _outproj(attn, four, w, x, g_post, g_pre, mods, row_fn, tm):
    t, d = x.shape
    vec = pl.BlockSpec((1, d), lambda i: (0, 0))
    row = pl.BlockSpec((tm, d), lambda i: (i, 0))
    half = pl.BlockSpec((tm, NA_WIDTH), lambda i: (i, 0))
    return pl.pallas_call(
        _outproj_kernel,
        grid=(t // tm,),
        in_specs=[half, half, _wspec(w, w[0].shape[1:], lambda i: (0, 0), pipeline_mode=pl.Buffered(1)), row, vec,
                  _mod_spec(2, row_fn), vec, _mod_spec(3, row_fn), _mod_spec(4, row_fn)],
        out_specs=[row, row],
        out_shape=[jax.ShapeDtypeStruct((t, d), F32), jax.ShapeDtypeStruct((t, d), BF16)],
        compiler_params=_params("parallel"),
        name="outproj_residual",
    )(attn, four, w[0], x, g_post.reshape(1, d), mods, g_pre.reshape(1, d), mods, mods)


def _matmul_kernel(x_ref, w_ref, o_ref):
    o_ref[...] = _dot(x_ref[...], w_ref[...]).astype(o_ref.dtype)


def _matmul(x, w, tn, out_dtype, name):
    m, k = x.shape
    n = w[0].shape[2]
    return pl.pallas_call(
        _matmul_kernel,
        grid=(n // tn,),
        in_specs=[pl.BlockSpec((m, k), lambda j: (0, 0)), _wspec(w, (k, tn), lambda j: (0, j))],
        out_specs=pl.BlockSpec((m, tn), lambda j: (0, j)),
        out_shape=jax.ShapeDtypeStruct((m, n), out_dtype),
        compiler_params=_params("parallel"),
        name=name,
    )(x, w[0])


def _ffn_kernel(h_ref, wa_ref, wg_ref, edge_ref, cp_ref, wd_ref, x_ref, gate_ref, gpost_ref, o_ref,
                acc_ref, ua0, ug0, ua1, ug1, act0, act1,
                *, tm, n_ff_tiles, n_row_tiles):
    i = pl.program_id(0)
    j = pl.program_id(1)
    nj = n_ff_tiles
    pad = F32_SUBLANES
    tf = wa_ref.shape[1]
    u_refs = ((ua0, ug0), (ua1, ug1))
    act_refs = (act0, act1)

    def up_proj_part(half, k0):
        w_ref = (wa_ref, wg_ref)[half]
        return _dot(h_ref[:, k0:k0 + UP_K], w_ref[k0:k0 + UP_K, :])

    def up_proj_store(parity, half, u):
        u_ref = u_refs[parity][half]
        u_ref[0:pad, :] = jnp.broadcast_to(edge_ref[half:half + 1, :], (pad, tf))
        u_ref[pad:pad + tm, :] = u
        u_ref[pad + tm:, :] = jnp.broadcast_to(edge_ref[2 + half:3 + half, :], (pad, tf))

    def conv(u_ref, r0, cols, p0):
        win = u_ref[r0:r0 + CONV_ROWS + 2 * pad, cols]
        mid = win[pad:pad + CONV_ROWS]
        down = pltpu.roll(win, 1, axis=0)[pad:pad + CONV_ROWS]
        up = pltpu.roll(win, CONV_ROWS + 2 * pad - 1, axis=0)[pad:pad + CONV_ROWS]
        return (down * cp_ref[p0:p0 + 1, cols] + mid * cp_ref[p0 + 1:p0 + 2, cols]
                + up * cp_ref[p0 + 2:p0 + 3, cols] + cp_ref[p0 + 3:p0 + 4, cols])

    def conv_gate(parity, r0, c0):
        ua_ref, ug_ref = u_refs[parity]
        cols = slice(c0, c0 + LANES)
        a = conv(ua_ref, r0, cols, 0)
        g = conv(ug_ref, r0, cols, CONV_TAPS + 1)
        act_refs[parity][r0:r0 + CONV_ROWS, cols] = (g * jax.nn.sigmoid(g) * a).astype(BF16)

    def down_proj(parity, cols):
        return _dot(act_refs[parity][...], wd_ref[:, cols])

    chunks = [(r0, c0) for r0 in range(0, tm, CONV_ROWS) for c0 in range(0, tf, LANES)]
    d_out = wd_ref.shape[1]
    down_groups = [slice(c, c + DOWN_COLS) for c in range(0, d_out, DOWN_COLS)]

    def finish_rows(r0):
        rows = slice(r0, r0 + FINISH_ROWS)
        o_ref[rows, :] = x_ref[rows, :] + gate_ref[...] * _rms(acc_ref[rows, :], gpost_ref[...])
        acc_ref[rows, :] = jnp.zeros((FINISH_ROWS, d_out), F32)

    def run_step(up=None, conv_of=None, down=None, finish=False):
        pieces = []
        if down is not None:
            for cols in down_groups:
                def piece(cols=cols):
                    acc_ref[:, cols] += down_proj(down, cols)
                pieces.append((piece, tf * DOWN_COLS))
        n_down = len(pieces)
        if up is not None:
            d_in = h_ref.shape[1]
            for half in range(2):
                partial_u = []
                for k0 in range(0, d_in, UP_K):
                    def piece(half=half, k0=k0, partial_u=partial_u):
                        part = up_proj_part(half, k0)
                        partial_u[:] = [part if not partial_u else partial_u[0] + part]
                        if k0 + UP_K == d_in:
                            up_proj_store(up, half, partial_u[0])
                    pieces.append((piece, UP_K * tf))
        conv_todo = [functools.partial(conv_gate, conv_of, *c) for c in chunks] if conv_of is not None else []
        fin_todo = [functools.partial(finish_rows, r0) for r0 in range(0, tm, FINISH_ROWS)] if finish else []
        n_conv, n_fin = len(conv_todo), len(fin_todo)
        total = sum(w for _, w in pieces)
        total_up = sum(w for _, w in pieces[n_down:])
        done = done_up = 0.0
        for n, (piece, w) in enumerate(pieces):
            piece()
            done += w
            is_last = n == len(pieces) - 1
            conv_upto = n_conv if is_last else round(n_conv * done / total)
            while n_conv - len(conv_todo) < conv_upto:
                conv_todo.pop(0)()
            if n >= n_down:
                done_up += w
                fin_upto = n_fin if is_last else round(n_fin * done_up / total_up)
                while n_fin - len(fin_todo) < fin_upto:
                    fin_todo.pop(0)()
        for chunk in conv_todo + fin_todo:
            chunk()

    s = i * nj + j
    last = n_row_tiles * nj
    finishing = (j == 1) & (i >= 1)

    @pl.when(s == 0)
    def _():
        acc_ref[...] = jnp.zeros_like(acc_ref)
        run_step(up=0)

    @pl.when(s == 1)
    def _():
        run_step(up=1, conv_of=0)

    for parity in range(2):
        steady = (s >= 2) & (s < last) & (s % 2 == parity)

        @pl.when(steady & jnp.logical_not(finishing))
        def _():
            run_step(up=parity, conv_of=1 - parity, down=parity)

        @pl.when(steady & finishing)
        def _():
            run_step(up=parity, conv_of=1 - parity, down=parity, finish=True)

    @pl.when(s == last)
    def _():
        run_step(conv_of=(last - 1) % 2, down=last % 2)

    @pl.when(s == last + 1)
    def _():
        run_step(down=(last + 1) % 2, finish=True)


def _ffn(h, w_up, conv_w, conv_b, w_down, x, g_post, mods, row_fn, seq_len, tm):
    t, d = x.shape
    ff = w_down[0].shape[1]
    tf = FF_TILE
    nj = ff // tf
    n_tiles = t // tm
    assert seq_len % tm == 0

    tile0 = np.arange(n_tiles) * tm
    start = tile0 % seq_len == 0
    end = (tile0 + tm) % seq_len == 0
    prev = jnp.take(h, jnp.asarray(np.maximum(tile0 - 1, 0)), axis=0)
    nxt = jnp.take(h, jnp.asarray(np.minimum(tile0 + tm, t - 1)), axis=0)
    prev = jnp.where(jnp.asarray(start)[:, None], jnp.zeros_like(prev), prev)
    nxt = jnp.where(jnp.asarray(end)[:, None], jnp.zeros_like(nxt), nxt)
    edge_rows = jnp.concatenate([prev, nxt], axis=0)
    pad = -edge_rows.shape[0] % BF16_SUBLANES
    edge_rows = jnp.pad(edge_rows, ((0, pad), (0, 0)))
    u_edge = _matmul(edge_rows, w_up, 2 * ff // 8, F32, "conv_ffn_edges")
    u_prev, u_next = u_edge[:n_tiles], u_edge[n_tiles:2 * n_tiles]
    edges = jnp.stack([u_prev[:, :ff], u_prev[:, ff:], u_next[:, :ff], u_next[:, ff:]], axis=1)
    edges = edges.reshape(n_tiles, 4, nj, tf).transpose(0, 2, 1, 3)
    conv_p = jnp.concatenate([conv_w[:, :ff], conv_b[None, :ff], conv_w[:, ff:], conv_b[None, ff:]], axis=0)
    conv_p = conv_p.reshape(2 * (CONV_TAPS + 1), nj, tf).transpose(1, 0, 2)

    assert nj >= 3
    vec = pl.BlockSpec((1, d), lambda i, j: (0, 0))
    cur = lambda i: jnp.minimum(i, n_tiles - 1)
    fin = lambda i, j, switch: jnp.where(j <= switch, jnp.maximum(i - 1, 0), cur(i))
    lag = lambda j, k: (j + nj - k) % nj
    up_cols = lambda off: _wspec(w_up, (d, tf), lambda i, j: (0, off + j))
    u_buf = pltpu.VMEM((tm + 2 * F32_SUBLANES, tf), F32)
    act_buf = pltpu.VMEM((tm, tf), BF16)
    return pl.pallas_call(
        functools.partial(_ffn_kernel, tm=tm, n_ff_tiles=nj, n_row_tiles=n_tiles),
        grid=(n_tiles + 1, nj),
        in_specs=[
            pl.BlockSpec((tm, d), lambda i, j: (cur(i), 0)),
            up_cols(0), up_cols(nj),
            pl.BlockSpec((None, None, 4, tf), lambda i, j: (cur(i), j, 0, 0)),
            pl.BlockSpec((None, 2 * (CONV_TAPS + 1), tf), lambda i, j: (lag(j, 1), 0, 0)),
            _wspec(w_down, (tf, d), lambda i, j: (lag(j, 2), 0)),
            pl.BlockSpec((tm, d), lambda i, j: (fin(i, j, 1), 0)),
            pl.BlockSpec((None, None, 1, d), lambda i, j: (row_fn(fin(i, j, 1)), 5, 0, 0)),
            vec,
        ],
        out_specs=pl.BlockSpec((tm, d), lambda i, j: (fin(i, j, 3), 0)),
        out_shape=jax.ShapeDtypeStruct((t, d), F32),
        scratch_shapes=[pltpu.VMEM((tm, d), F32), u_buf, u_buf, u_buf, u_buf, act_buf, act_buf],
        compiler_params=_params("arbitrary", "arbitrary"),
        name="conv_ffn",
    )(h, w_up[0], w_up[0], edges, conv_p, w_down[0], x, mods,
      g_post.reshape(1, d))


def kernel(x, c, ctx, c_ctx, w_ada, b_ada, g_pre_mix, w_in, rpb, w_four, w_out, g_post_mix,
           g_pre_ffn, w_up, conv_w, conv_b, w_down, g_post_ffn):
    b, n, d = x.shape
    n_ctx = ctx.shape[1]
    assert (d, n, n_ctx, b) == (D_MODEL, GRID_W * GRID_W, CTX_LEN, CTX_MOD_ROW)
    t_lat, t_ctx = b * n, b * n_ctx
    tm_lat, tm_ctx = 512, CTX_LEN
    lat_row = lambda i: (i * tm_lat) // n
    ctx_row = lambda i: CTX_MOD_ROW

    cond = jnp.zeros((MOD_ROWS, d), F32).at[:b].set(c).at[CTX_MOD_ROW].set(c_ctx)
    mods_all = _mods(cond, w_ada, b_ada)

    scale = (n * FOURIER_GROUP_DIM) ** -0.5
    scale_ctx = (n_ctx * FOURIER_GROUP_DIM) ** -0.5
    kk = np.outer(np.arange(FOURIER_GROUP_DIM), np.arange(FOURIER_GROUP_DIM)) % FOURIER_GROUP_DIM
    ang = 2.0 * np.pi * kk / FOURIER_GROUP_DIM
    cc, sc = np.cos(ang), -np.sin(ang)
    mw_lat = _four_w(jnp.asarray(cc * scale, F32), jnp.asarray(sc * scale, F32), w_four)
    dft_lat = _dft_tables(n, GRID_W)

    w_in_b, w_out_b, w_up_b, w_down_b = (w.astype(BF16) for w in (w_in, w_out, w_up, w_down))
    xt = x.reshape(t_lat, d)
    ct = ctx.reshape(t_ctx, d)
    for l in range(DEPTH):
        last = l == DEPTH - 1
        mods = mods_all[l].reshape(MOD_ROWS, N_MOD, 1, d)
        w_in_l, w_out_l, w_up_l, w_down_l = ((w, l) for w in (w_in_b, w_out_b, w_up_b, w_down_b))

        proj = _inproj(xt, g_pre_mix[l], mods, lat_row, w_in_l, tm_lat)
        if last:
            w_kv = (w_in_b[l:l + 1, :, NA_WIDTH:3 * NA_WIDTH], 0)
            cproj = _inproj(ct, g_pre_mix[l], mods, ctx_row, w_kv, tm_ctx)
            kc_blk, vc_blk = 0, N_PAIRS
        else:
            cproj = _inproj(ct, g_pre_mix[l], mods, ctx_row, w_in_l, tm_ctx)
            kc_blk, vc_blk = N_PAIRS, 2 * N_PAIRS
        cproj3 = cproj.reshape(b, n_ctx, cproj.shape[1])

        attn = _attn(proj.reshape(b, n, PROJ_WIDTH), cproj3, _attn_bias_slabs(rpb[l]), kc_blk, vc_blk)
        four = _four_pos(dft_lat, _four_chan(proj, mw_lat[l], 1024), b, 1024, 512)
        xt, h2 = _outproj(attn.reshape(t_lat, NA_WIDTH), four.reshape(t_lat, FOURIER_WIDTH), w_out_l, xt,
                          g_post_mix[l], g_pre_ffn[l], mods, lat_row, tm_lat)
        xt = _ffn(h2, w_up_l, conv_w[l], conv_b[l], w_down_l, xt, g_post_ffn[l], mods, lat_row, n, tm_lat)

        if not last:
            mw_ctx = _four_w(jnp.asarray(cc * scale_ctx, F32), jnp.asarray(sc * scale_ctx, F32),
                             w_four[l:l + 1])[0]
            dft_ctx = _dft_tables(n_ctx, 16)
            attn_c = _ctx_attn(cproj3)
            four_c = _four_pos(dft_ctx, _four_chan(cproj, mw_ctx, 512), b, n_ctx // DFT_RADIX, 512)
            ct, hc2 = _outproj(attn_c.reshape(t_ctx, NA_WIDTH), four_c.reshape(t_ctx, FOURIER_WIDTH),
                               w_out_l, ct, g_post_mix[l], g_pre_ffn[l], mods, ctx_row, tm_ctx)
            ct = _ffn(hc2, w_up_l, conv_w[l], conv_b[l], w_down_l, ct, g_post_ffn[l], mods, ctx_row,
                      n_ctx, tm_ctx)
    return xt.reshape(b, n, d)
```

```python
import functools

import jax
import jax.numpy as jnp
import numpy as np
from jax import lax
from jax.experimental import pallas as pl
from jax.experimental.pallas import tpu as pltpu

D_MODEL = 2048
DEPTH = 2
GRID_W = 64
CTX_LEN = 256
NA_HEADS = 16
HEAD_DIM = 64
NA_WIDTH = NA_HEADS * HEAD_DIM
FOURIER_GROUPS = 8
FOURIER_GROUP_DIM = 128
FOURIER_WIDTH = FOURIER_GROUPS * FOURIER_GROUP_DIM
PROJ_WIDTH = 3 * NA_WIDTH + FOURIER_WIDTH
WIN_ROWS = 8
WIN_COLS = 16
D_FF = 5632
N_MOD = 6
EPS = 1e-6
ATTN_SCALE = HEAD_DIM ** -0.5
MOD_ROWS = 16
CTX_MOD_ROW = 8

LANES = 128
DFT_RADIX = 4
DFT_ROWS = 512
FF_TILE = 512
CONV_TAPS = 3
FINISH_ROWS = 64
UP_K = 512
DOWN_COLS = 512
OUT_COLS = 512
OUT_ROWS = 64
PROJ_COLS = 512
NORM_ROWS = 16
CONV_ROWS = 64
F32_SUBLANES = 8
BF16_SUBLANES = 16
HEAD_PAIR = LANES // HEAD_DIM
N_PAIRS = NA_HEADS // HEAD_PAIR
Q_ROWS = 4
K_ROWS = Q_ROWS + WIN_ROWS - 1
TQ = Q_ROWS * GRID_W
NK = K_ROWS * GRID_W
ATTN_UNROLL = 8
NEG = -1e30
VMEM_LIMIT = 56 * 1024 * 1024

F32 = jnp.float32
BF16 = jnp.bfloat16


def _params(*sem, flags=None):
    return pltpu.CompilerParams(dimension_semantics=sem, vmem_limit_bytes=VMEM_LIMIT, flags=flags)


def _dot(a, b):
    return jnp.dot(a, b, preferred_element_type=F32)


def _dot_nt(a, b):
    return lax.dot_general(a, b, (((1,), (1,)), ((), ())), preferred_element_type=F32)


def _rms(x, g):
    ms = jnp.mean(x * x, axis=-1, keepdims=True)
    return x * lax.rsqrt(ms + EPS) * g


def _norm_mod(x, g, shift, scale):
    return _rms(x, g) * (1.0 + scale) + shift


def _mods_kernel(cond_ref, w_ref, b_ref, o_ref):
    c = cond_ref[...]
    s = (c * jax.nn.sigmoid(c)).astype(BF16)
    o_ref[...] = _dot(s, w_ref[...].astype(BF16)) + b_ref[...]


def _mods(cond, w_ada, b_ada, tn=1024):
    depth, d, n = w_ada.shape
    return pl.pallas_call(
        _mods_kernel,
        grid=(depth, n // tn),
        in_specs=[
            pl.BlockSpec((MOD_ROWS, d), lambda l, j: (0, 0)),
            pl.BlockSpec((None, d, tn), lambda l, j: (l, 0, j)),
            pl.BlockSpec((None, 1, tn), lambda l, j: (l, 0, j)),
        ],
        out_specs=pl.BlockSpec((None, MOD_ROWS, tn), lambda l, j: (l, 0, j)),
        out_shape=jax.ShapeDtypeStruct((depth, MOD_ROWS, n), F32),
        compiler_params=_params("parallel", "parallel"),
        name="adaln_mods",
    )(cond, w_ada, b_ada.reshape(depth, 1, n))


def _wspec(w, block, index_fn, **kw):
    _, layer = w
    return pl.BlockSpec((None,) + tuple(block), lambda *idx: (layer,) + tuple(index_fn(*idx)), **kw)


def _mod_spec(which, row_fn):
    return pl.BlockSpec((None, None, 1, D_MODEL),
                        lambda i, *_: (row_fn(i), which, 0, 0))


def _interleave(pieces, chunks):
    if not pieces:
        for chunk in chunks:
            chunk()
        return
    slots = max(len(pieces) - 1, 1)
    per = -(-len(chunks) // slots)
    for n, piece in enumerate(pieces):
        piece()
        for chunk in chunks[n * per:(n + 1) * per]:
            chunk()


def _order_before_reads(ref, value):
    tile = (slice(0, BF16_SUBLANES), slice(0, LANES))
    bits = pltpu.bitcast(value[tile], jnp.uint32)
    zero = pltpu.bitcast((bits >> 16) >> 16, F32)
    ref[tile] = (ref[tile].astype(F32) + zero).astype(ref.dtype)


def _two_stage(i, n_tiles, run):
    @pl.when(i == 0)
    def _():
        run(0, None)

    for parity in range(2):
        @pl.when((i >= 1) & (i < n_tiles) & (i % 2 == parity))
        def _():
            run(parity, 1 - parity)

    @pl.when(i == n_tiles)
    def _():
        run(None, (n_tiles - 1) % 2)


def _inproj_kernel(x_ref, g_ref, sh_ref, sc_ref, w_ref, o_ref, h0_ref, h1_ref, *, n_tiles):
    h_refs = (h0_ref, h1_ref)
    tm = x_ref.shape[0]

    def run(norm_to, dot_from):
        pieces, chunks = [], []
        if dot_from is not None:
            for c0 in range(0, w_ref.shape[1], PROJ_COLS):
                def piece(cols=slice(c0, c0 + PROJ_COLS)):
                    o_ref[:, cols] = _dot(h_refs[dot_from][...], w_ref[:, cols]).astype(o_ref.dtype)
                pieces.append(piece)
        if norm_to is not None:
            for r0 in range(0, tm, NORM_ROWS):
                def chunk(rows=slice(r0, r0 + NORM_ROWS)):
                    val = _norm_mod(x_ref[rows, :], g_ref[...], sh_ref[...], sc_ref[...])
                    h_refs[norm_to][rows, :] = val.astype(BF16)
                    if dot_from is not None:
                        _order_before_reads(h_refs[dot_from], val)
                chunks.append(chunk)
        _interleave(pieces, chunks)

    _two_stage(pl.program_id(0), n_tiles, run)


def _inproj(x, g, mods, row_fn, w, tm):
    t, d = x.shape
    n = w[0].shape[2]
    n_tiles = t // tm
    cur = lambda i: jnp.minimum(i, n_tiles - 1)
    prev = lambda i: jnp.maximum(i - 1, 0)
    return pl.pallas_call(
        functools.partial(_inproj_kernel, n_tiles=n_tiles),
        grid=(n_tiles + 1,),
        in_specs=[
            pl.BlockSpec((tm, d), lambda i: (cur(i), 0)),
            pl.BlockSpec((1, d), lambda i: (0, 0)),
            _mod_spec(0, lambda i: row_fn(cur(i))),
            _mod_spec(1, lambda i: row_fn(cur(i))),
            _wspec(w, (d, n), lambda i: (0, 0), pipeline_mode=pl.Buffered(1)),
        ],
        out_specs=pl.BlockSpec((tm, n), lambda i: (prev(i), 0)),
        out_shape=jax.ShapeDtypeStruct((t, n), BF16),
        scratch_shapes=[pltpu.VMEM((tm, d), BF16)] * 2,
        compiler_params=_params("arbitrary"),
        name="norm_inproj",
    )(x, g.reshape(1, d), mods, mods, w[0])


def _softmax_pv(parts):
    m = functools.reduce(jnp.maximum, [s.max(axis=-1, keepdims=True) for s, _ in parts])
    num = 0.0
    den = 0.0
    for s, v in parts:
        p = jnp.exp(s - m)
        den = den + p.sum(axis=-1, keepdims=True)
        num = num + _dot(p.astype(BF16), v)
    return num / den


def _head_masks():
    lane = lax.broadcasted_iota(jnp.int32, (1, LANES), 1)
    first = lane < HEAD_DIM
    return first, [first, jnp.logical_not(first)]


def _attn_kernel(q_ref, k_ref, v_ref, kc_ref, vc_ref, slab_ref, o_ref, tab_ref, *, n_tiles, rows):
    first, sels = _head_masks()
    kc = kc_ref[...]
    vc = vc_ref[...]

    @pl.when(pl.program_id(1) == 0)
    def _():
        pattern = _attn_slab_pattern(rows)
        for p in range(pattern.shape[0]):
            for hh in range(HEAD_PAIR):
                for qr in range(Q_ROWS):
                    for kr in range(K_ROWS):
                        tab_ref[p, hh, qr * GRID_W:(qr + 1) * GRID_W, kr * GRID_W:(kr + 1) * GRID_W] = (
                            slab_ref[hh, int(pattern[p, qr, kr])])

    def scores(t, hh):
        q0 = pl.multiple_of(t * TQ, TQ)
        k0 = pl.multiple_of(jnp.clip(t * Q_ROWS - WIN_ROWS // 2, 0, rows - K_ROWS) * GRID_W, GRID_W)
        pat = jnp.where(t == 0, 0, jnp.where(t == n_tiles - 1, 2, 1))
        q = q_ref[pl.ds(q0, TQ), :] * ATTN_SCALE
        qm = jnp.where(sels[hh], q, jnp.zeros_like(q))
        s_loc = _dot_nt(qm, k_ref[pl.ds(k0, NK), :]) + tab_ref[pat, hh]
        s_ctx = _dot_nt(qm, kc)
        return k0, s_loc, s_ctx

    def tile_group(tg, carry):
        units = [(tg * ATTN_UNROLL + ti, hh) for ti in range(ATTN_UNROLL) for hh in range(HEAD_PAIR)]
        pending = None
        outs = {}
        for unit in units + [None]:
            nxt = scores(*unit) if unit is not None else None
            if pending is not None:
                (t, hh), (k0, s_loc, s_ctx) = pending
                outs[hh] = _softmax_pv([(s_loc, v_ref[pl.ds(k0, NK), :]), (s_ctx, vc)])
                if hh == HEAD_PAIR - 1:
                    q0 = pl.multiple_of(t * TQ, TQ)
                    o_ref[pl.ds(q0, TQ), :] = jnp.where(first, outs[0], outs[1]).astype(o_ref.dtype)
            pending = (unit, nxt)
        return carry

    lax.fori_loop(0, n_tiles // ATTN_UNROLL, tile_group, 0)


def _attn(proj, cproj, slabs, kc_blk, vc_blk):
    b, n, _ = proj.shape
    rows = n // GRID_W
    n_tiles = rows // Q_ROWS
    lat = lambda off: pl.BlockSpec((None, n, LANES), lambda p, i: (i, 0, off + p))
    cx = lambda off: pl.BlockSpec((None, CTX_LEN, LANES), lambda p, i: (i, 0, off + p))
    return pl.pallas_call(
        functools.partial(_attn_kernel, n_tiles=n_tiles, rows=rows),
        grid=(N_PAIRS, b),
        in_specs=[
            lat(0), lat(N_PAIRS), lat(2 * N_PAIRS), cx(kc_blk), cx(vc_blk),
            pl.BlockSpec((HEAD_PAIR,) + slabs.shape[1:], lambda p, i: (p, 0, 0, 0)),
        ],
        out_specs=pl.BlockSpec((None, n, LANES), lambda p, i: (i, 0, p)),
        out_shape=jax.ShapeDtypeStruct((b, n, NA_WIDTH), BF16),
        scratch_shapes=[pltpu.VMEM((3, HEAD_PAIR, TQ, NK), F32)],
        compiler_params=_params("arbitrary", "arbitrary"),
        name="nbr_attn",
    )(proj, proj, proj, cproj, cproj, slabs)


def _ctx_attn_kernel(q_ref, k_ref, v_ref, o_ref):
    first, sels = _head_masks()
    q = q_ref[...] * ATTN_SCALE
    k = k_ref[...]
    v = v_ref[...]
    outs = []
    for hh in range(HEAD_PAIR):
        qm = jnp.where(sels[hh], q, jnp.zeros_like(q))
        outs.append(_softmax_pv([(_dot_nt(qm, k), v)]))
    o_ref[...] = jnp.where(first, outs[0], outs[1]).astype(o_ref.dtype)


def _ctx_attn(cproj):
    b, n, _ = cproj.shape
    spec = lambda off: pl.BlockSpec((None, n, LANES), lambda i, p: (i, 0, off + p))
    return pl.pallas_call(
        _ctx_attn_kernel,
        grid=(b, N_PAIRS),
        in_specs=[spec(0), spec(N_PAIRS), spec(2 * N_PAIRS)],
        out_specs=spec(0),
        out_shape=jax.ShapeDtypeStruct((b, n, NA_WIDTH), BF16),
        compiler_params=_params("parallel", "parallel"),
        name="ctx_attn",
    )(cproj, cproj, cproj)


MASKED_SLAB = 2 * WIN_ROWS - 1


def _attn_bias_slabs(rpb):
    qc = np.arange(GRID_W)[:, None]
    kc = np.arange(GRID_W)[None, :]
    cs = np.clip(qc - WIN_COLS // 2, 0, GRID_W - WIN_COLS)
    col_ok = (kc >= cs) & (kc < cs + WIN_COLS)
    dc = kc - qc + WIN_COLS - 1
    onehot = np.zeros((2 * WIN_COLS - 1, GRID_W, GRID_W), np.float32)
    qi, ki = np.nonzero(col_ok)
    onehot[dc[qi, ki], qi, ki] = 1.0
    t1 = jnp.einsum('hrd,dqk->hrqk', rpb, jnp.asarray(onehot), precision=lax.Precision.HIGHEST)
    t1 = jnp.where(jnp.asarray(col_ok)[None, None], t1, NEG)
    neg_slab = jnp.full((NA_HEADS, 1, GRID_W, GRID_W), NEG, F32)
    return jnp.concatenate([t1, neg_slab], axis=1)


def _attn_slab_pattern(rows):
    n_tiles = rows // Q_ROWS
    idx = np.full((3, Q_ROWS, K_ROWS), MASKED_SLAB, np.int32)
    for p, t in enumerate((0, 1, n_tiles - 1)):
        k_start = int(np.clip(t * Q_ROWS - WIN_ROWS // 2, 0, rows - K_ROWS))
        for qr in range(Q_ROWS):
            r = t * Q_ROWS + qr
            rs = int(np.clip(r - WIN_ROWS // 2, 0, rows - WIN_ROWS))
            for kr in range(K_ROWS):
                krow = k_start + kr
                if rs <= krow < rs + WIN_ROWS:
                    idx[p, qr, kr] = krow - r + WIN_ROWS - 1
    return idx


def _split_bf16(a):
    hi = a.astype(BF16)
    lo = (a - hi.astype(F32)).astype(BF16)
    return hi, lo


def _dot3(a, b):
    ah, al = _split_bf16(a)
    bh, bl = _split_bf16(b)
    return _dot(ah, bh) + (_dot(ah, bl) + _dot(al, bh))


def _four_w_kernel(cc_ref, sc_ref, w_ref, o_ref):
    w = w_ref[...]
    o_ref[:, :FOURIER_GROUP_DIM] = _dot3(cc_ref[...], w).astype(BF16)
    o_ref[:, FOURIER_GROUP_DIM:] = _dot3(sc_ref[...], w).astype(BF16)


def _four_w(cc, sc, w_four):
    depth, g, c, _ = w_four.shape
    tab = pl.BlockSpec((c, c), lambda l, i: (0, 0))
    return pl.pallas_call(
        _four_w_kernel,
        grid=(depth, g),
        in_specs=[tab, tab, pl.BlockSpec((None, None, c, c), lambda l, i: (l, i, 0, 0))],
        out_specs=pl.BlockSpec((None, None, c, 2 * c), lambda l, i: (l, i, 0, 0)),
        out_shape=jax.ShapeDtypeStruct((depth, g, c, 2 * c), BF16),
        compiler_params=_params("parallel", "parallel"),
        name="fourier_weights",
    )(cc, sc, w_four)


def _four_chan_kernel(f_ref, m_ref, *refs):
    out_refs, (yc_scr, ys_scr) = refs[:2 * DFT_RADIX], refs[2 * DFT_RADIX:]
    c = FOURIER_GROUP_DIM
    part_rows = f_ref.shape[0] // DFT_RADIX
    for g in range(FOURIER_GROUPS):
        cols = slice(g * c, (g + 1) * c)
        y = _dot(f_ref[:, cols], m_ref[g])
        yc_scr[...] = y[:, :c]
        ys_scr[...] = y[:, c:]
        for p in range(DFT_RADIX):
            out_refs[2 * p][:, cols] = yc_scr[pl.ds(p, part_rows, stride=DFT_RADIX), :].astype(BF16)
            out_refs[2 * p + 1][:, cols] = ys_scr[pl.ds(p, part_rows, stride=DFT_RADIX), :].astype(BF16)


def _four_chan(proj, mw, tm):
    t = proj.shape[0]
    f_blk = proj.shape[1] // FOURIER_WIDTH - 1
    out = pl.BlockSpec((tm // DFT_RADIX, FOURIER_WIDTH), lambda i: (i, 0))
    return pl.pallas_call(
        _four_chan_kernel,
        grid=(t // tm,),
        in_specs=[
            pl.BlockSpec((tm, FOURIER_WIDTH), lambda i: (i, f_blk)),
            pl.BlockSpec(mw.shape, lambda i: (0, 0, 0)),
        ],
        out_specs=[out] * (2 * DFT_RADIX),
        out_shape=[jax.ShapeDtypeStruct((t // DFT_RADIX, FOURIER_WIDTH), BF16)] * (2 * DFT_RADIX),
        scratch_shapes=[pltpu.VMEM((tm, FOURIER_GROUP_DIM), F32)] * 2,
        compiler_params=_params("parallel"),
        name="fourier_chan",
    )(proj, mw)


def _four_pos_kernel(c0, s0, c1, s1, c2, s2, c3, s3, ns1, nc3, a0, b0, a1, b1, a2, b2, a3, b3, o_ref):
    step = min(DFT_ROWS, c0.shape[0])
    for r0 in range(0, c0.shape[0], step):
        rows = slice(r0, r0 + step)
        dot = lambda table, data: _dot(table[rows, :], data[...])
        re0 = dot(c0, a0) + dot(s0, b0)
        re2 = dot(c2, a2) + dot(s2, b2)
        re13 = dot(c1, a1) + dot(s1, b1) + dot(c3, a3) + dot(s3, b3)
        im13 = dot(c1, b1) + dot(ns1, a1) + dot(nc3, b3) + dot(s3, a3)
        plus, minus = re0 + re2, re0 - re2
        o_ref[0, rows, :] = (plus + re13).astype(o_ref.dtype)
        o_ref[1, rows, :] = (minus + im13).astype(o_ref.dtype)
        o_ref[2, rows, :] = (plus - re13).astype(o_ref.dtype)
        o_ref[3, rows, :] = (minus - im13).astype(o_ref.dtype)


def _four_pos(tables, y_parts, b, tm, tn):
    data = [y.reshape(b, y.shape[0] // b, y.shape[1]) for y in y_parts]
    _, part, w = data[0].shape
    mat = pl.BlockSpec((tm, part), lambda i, bb, j: (i, 0), pipeline_mode=pl.Buffered(1))
    dat = pl.BlockSpec((None, part, tn), lambda i, bb, j: (bb, 0, j))
    out = pl.pallas_call(
        _four_pos_kernel,
        grid=(part // tm, b, w // tn),
        in_specs=[mat] * len(tables) + [dat] * len(data),
        out_specs=pl.BlockSpec((None, DFT_RADIX, tm, tn), lambda i, bb, j: (bb, 0, i, j)),
        out_shape=jax.ShapeDtypeStruct((b, DFT_RADIX, part, w), BF16),
        compiler_params=_params("parallel", "parallel", "parallel"),
        name="fourier_pos",
    )(*tables, *data)
    return out.reshape(b, DFT_RADIX * part, w)


def _dft_tables(n, split):
    part = n // DFT_RADIX
    j = jnp.arange(part, dtype=jnp.int32)[:, None]
    hi = jnp.arange(part // split, dtype=jnp.int32)[None, :] * split
    lo = jnp.arange(split, dtype=jnp.int32)[None, :]
    ang = lambda k: (2.0 * np.pi / part) * ((j * k) % part).astype(F32)
    ch, sh, cl, sl = jnp.cos(ang(hi)), jnp.sin(ang(hi)), jnp.cos(ang(lo)), jnp.sin(ang(lo))
    c0 = (ch[:, :, None] * cl[:, None, :] - sh[:, :, None] * sl[:, None, :]).reshape(part, part)
    s0 = (sh[:, :, None] * cl[:, None, :] + ch[:, :, None] * sl[:, None, :]).reshape(part, part)
    tables = [c0, s0]
    for p in range(1, DFT_RADIX):
        phi = (2.0 * np.pi * p / n) * j.astype(F32)
        cp, sp = jnp.cos(phi), jnp.sin(phi)
        tables += [c0 * cp - s0 * sp, s0 * cp + c0 * sp]
    tables += [-tables[3], -tables[6]]
    return tuple(t.astype(BF16) for t in tables)


def _outproj_kernel(a_ref, f_ref, w_ref, x_ref, gpost_ref, gate_ref, gpre_ref, sh_ref, sc_ref,
                    xo_ref, h_ref, m0_ref, m1_ref, lhs_ref, *, n_tiles):
    mix_refs = (m0_ref, m1_ref)
    tm = x_ref.shape[0]

    def run(mix_to, epi_from):
        pieces, chunks = [], []
        if mix_to is not None:
            lhs_ref[:, :NA_WIDTH] = a_ref[...]
            lhs_ref[:, NA_WIDTH:] = f_ref[...]
            for c0 in range(0, w_ref.shape[1], OUT_COLS):
                def piece(cols=slice(c0, c0 + OUT_COLS)):
                    mix_refs[mix_to][:, cols] = _dot(lhs_ref[...], w_ref[:, cols])
                pieces.append(piece)
        if epi_from is not None:
            for r0 in range(0, tm, OUT_ROWS):
                def chunk(rows=slice(r0, r0 + OUT_ROWS)):
                    x = x_ref[rows, :] + gate_ref[...] * _rms(mix_refs[epi_from][rows, :], gpost_ref[...])
                    xo_ref[rows, :] = x
                    h = _norm_mod(x, gpre_ref[...], sh_ref[...], sc_ref[...])
                    h_ref[rows, :] = h.astype(BF16)
                    if mix_to is not None:
                        _order_before_reads(lhs_ref, h)
                chunks.append(chunk)
        _interleave(pieces, chunks)

    _two_stage(pl.program_id(0), n_tiles, run)


def _outproj(attn, four, w, x, g_post, g_pre, mods, row_fn, tm):
    t, d = x.shape
    n_tiles = t // tm
    cur = lambda i: jnp.minimum(i, n_tiles - 1)
    prev = lambda i: jnp.maximum(i - 1, 0)
    vec = pl.BlockSpec((1, d), lambda i: (0, 0))
    row = pl.BlockSpec((tm, d), lambda i: (prev(i), 0))
    half = pl.BlockSpec((tm, NA_WIDTH), lambda i: (cur(i), 0))
    mod = lambda which: _mod_spec(which, lambda i: row_fn(prev(i)))
    return pl.pallas_call(
        functools.partial(_outproj_kernel, n_tiles=n_tiles),
        grid=(n_tiles + 1,),
        in_specs=[half, half, _wspec(w, w[0].shape[1:], lambda i: (0, 0), pipeline_mode=pl.Buffered(1)), row, vec,
                  mod(2), vec, mod(3), mod(4)],
        out_specs=[row, row],
        out_shape=[jax.ShapeDtypeStruct((t, d), F32), jax.ShapeDtypeStruct((t, d), BF16)],
        scratch_shapes=[pltpu.VMEM((tm, d), F32), pltpu.VMEM((tm, d), F32),
                        pltpu.VMEM((tm, NA_WIDTH + FOURIER_WIDTH), BF16)],
        compiler_params=_params("arbitrary"),
        name="outproj_residual",
    )(attn, four, w[0], x, g_post.reshape(1, d), mods, g_pre.reshape(1, d), mods, mods)


def _matmul_kernel(x_ref, w_ref, o_ref):
    o_ref[...] = _dot(x_ref[...], w_ref[...]).astype(o_ref.dtype)


def _matmul(x, w, tn, out_dtype, name):
    m, k = x.shape
    n = w[0].shape[2]
    return pl.pallas_call(
        _matmul_kernel,
        grid=(n // tn,),
        in_specs=[pl.BlockSpec((m, k), lambda j: (0, 0)), _wspec(w, (k, tn), lambda j: (0, j))],
        out_specs=pl.BlockSpec((m, tn), lambda j: (0, j)),
        out_shape=jax.ShapeDtypeStruct((m, n), out_dtype),
        compiler_params=_params("parallel"),
        name=name,
    )(x, w[0])


def _ffn_kernel(h_ref, wa_ref, wg_ref, edge_ref, cp_ref, wd_ref, x_ref, gate_ref, gpost_ref, o_ref,
                acc_ref, ua0, ug0, ua1, ug1, act0, act1,
                *, tm, n_ff_tiles, n_row_tiles):
    i = pl.program_id(0)
    j = pl.program_id(1)
    nj = n_ff_tiles
    pad = F32_SUBLANES
    tf = wa_ref.shape[1]
    u_refs = ((ua0, ug0), (ua1, ug1))
    act_refs = (act0, act1)

    def up_proj_part(half, k0):
        w_ref = (wa_ref, wg_ref)[half]
        return _dot(h_ref[:, k0:k0 + UP_K], w_ref[k0:k0 + UP_K, :])

    def up_proj_store(parity, half, u):
        u_ref = u_refs[parity][half]
        u_ref[0:pad, :] = jnp.broadcast_to(edge_ref[half:half + 1, :], (pad, tf))
        u_ref[pad:pad + tm, :] = u
        u_ref[pad + tm:, :] = jnp.broadcast_to(edge_ref[2 + half:3 + half, :], (pad, tf))

    def conv(u_ref, r0, cols, p0):
        win = u_ref[r0:r0 + CONV_ROWS + 2 * pad, cols]
        mid = win[pad:pad + CONV_ROWS]
        down = pltpu.roll(win, 1, axis=0)[pad:pad + CONV_ROWS]
        up = pltpu.roll(win, CONV_ROWS + 2 * pad - 1, axis=0)[pad:pad + CONV_ROWS]
        return (down * cp_ref[p0:p0 + 1, cols] + mid * cp_ref[p0 + 1:p0 + 2, cols]
                + up * cp_ref[p0 + 2:p0 + 3, cols] + cp_ref[p0 + 3:p0 + 4, cols])

    def conv_gate(parity, r0, c0):
        ua_ref, ug_ref = u_refs[parity]
        cols = slice(c0, c0 + LANES)
        a = conv(ua_ref, r0, cols, 0)
        g = conv(ug_ref, r0, cols, CONV_TAPS + 1)
        act_refs[parity][r0:r0 + CONV_ROWS, cols] = (g * jax.nn.sigmoid(g) * a).astype(BF16)

    def down_proj(parity, cols):
        return _dot(act_refs[parity][...], wd_ref[:, cols])

    chunks = [(r0, c0) for r0 in range(0, tm, CONV_ROWS) for c0 in range(0, tf, LANES)]
    d_out = wd_ref.shape[1]
    down_groups = [slice(c, c + DOWN_COLS) for c in range(0, d_out, DOWN_COLS)]

    def finish_rows(r0):
        rows = slice(r0, r0 + FINISH_ROWS)
        o_ref[rows, :] = x_ref[rows, :] + gate_ref[...] * _rms(acc_ref[rows, :], gpost_ref[...])
        acc_ref[rows, :] = jnp.zeros((FINISH_ROWS, d_out), F32)

    def run_step(up=None, conv_of=None, down=None, finish=False):
        pieces = []
        if down is not None:
            for cols in down_groups:
                def piece(cols=cols):
                    acc_ref[:, cols] += down_proj(down, cols)
                pieces.append((piece, tf * DOWN_COLS))
        n_down = len(pieces)
        if up is not None:
            d_in = h_ref.shape[1]
            for half in range(2):
                partial_u = []
                for k0 in range(0, d_in, UP_K):
                    def piece(half=half, k0=k0, partial_u=partial_u):
                        part = up_proj_part(half, k0)
                        partial_u[:] = [part if not partial_u else partial_u[0] + part]
                        if k0 + UP_K == d_in:
                            up_proj_store(up, half, partial_u[0])
                    pieces.append((piece, UP_K * tf))
        conv_todo = [functools.partial(conv_gate, conv_of, *c) for c in chunks] if conv_of is not None else []
        fin_todo = [functools.partial(finish_rows, r0) for r0 in range(0, tm, FINISH_ROWS)] if finish else []
        n_conv, n_fin = len(conv_todo), len(fin_todo)
        total = sum(w for _, w in pieces)
        total_up = sum(w for _, w in pieces[n_down:])
        done = done_up = 0.0
        for n, (piece, w) in enumerate(pieces):
            piece()
            done += w
            is_last = n == len(pieces) - 1
            conv_upto = n_conv if is_last else round(n_conv * done / total)
            while n_conv - len(conv_todo) < conv_upto:
                conv_todo.pop(0)()
            if n >= n_down:
                done_up += w
                fin_upto = n_fin if is_last else round(n_fin * done_up / total_up)
                while n_fin - len(fin_todo) < fin_upto:
                    fin_todo.pop(0)()
        for chunk in conv_todo + fin_todo:
            chunk()

    s = i * nj + j
    last = n_row_tiles * nj
    finishing = (j == 1) & (i >= 1)

    @pl.when(s == 0)
    def _():
        acc_ref[...] = jnp.zeros_like(acc_ref)
        run_step(up=0)

    @pl.when(s == 1)
    def _():
        run_step(up=1, conv_of=0)

    for parity in range(2):
        steady = (s >= 2) & (s < last) & (s % 2 == parity)

        @pl.when(steady & jnp.logical_not(finishing))
        def _():
            run_step(up=parity, conv_of=1 - parity, down=parity)

        @pl.when(steady & finishing)
        def _():
            run_step(up=parity, conv_of=1 - parity, down=parity, finish=True)

    @pl.when(s == last)
    def _():
        run_step(conv_of=(last - 1) % 2, down=last % 2)

    @pl.when(s == last + 1)
    def _():
        run_step(down=(last + 1) % 2, finish=True)


def _ffn(h, w_up, conv_w, conv_b, w_down, x, g_post, mods, row_fn, seq_len, tm):
    t, d = x.shape
    ff = w_down[0].shape[1]
    tf = FF_TILE
    nj = ff // tf
    n_tiles = t // tm
    assert seq_len % tm == 0

    tile0 = np.arange(n_tiles) * tm
    start = tile0 % seq_len == 0
    end = (tile0 + tm) % seq_len == 0
    prev = jnp.take(h, jnp.asarray(np.maximum(tile0 - 1, 0)), axis=0)
    nxt = jnp.take(h, jnp.asarray(np.minimum(tile0 + tm, t - 1)), axis=0)
    prev = jnp.where(jnp.asarray(start)[:, None], jnp.zeros_like(prev), prev)
    nxt = jnp.where(jnp.asarray(end)[:, None], jnp.zeros_like(nxt), nxt)
    edge_rows = jnp.concatenate([prev, nxt], axis=0)
    pad = -edge_rows.shape[0] % BF16_SUBLANES
    edge_rows = jnp.pad(edge_rows, ((0, pad), (0, 0)))
    u_edge = _matmul(edge_rows, w_up, 2 * ff // 8, F32, "conv_ffn_edges")
    u_prev, u_next = u_edge[:n_tiles], u_edge[n_tiles:2 * n_tiles]
    edges = jnp.stack([u_prev[:, :ff], u_prev[:, ff:], u_next[:, :ff], u_next[:, ff:]], axis=1)
    edges = edges.reshape(n_tiles, 4, nj, tf).transpose(0, 2, 1, 3)
    conv_p = jnp.concatenate([conv_w[:, :ff], conv_b[None, :ff], conv_w[:, ff:], conv_b[None, ff:]], axis=0)
    conv_p = conv_p.reshape(2 * (CONV_TAPS + 1), nj, tf).transpose(1, 0, 2)

    assert nj >= 3
    vec = pl.BlockSpec((1, d), lambda i, j: (0, 0))
    cur = lambda i: jnp.minimum(i, n_tiles - 1)
    fin = lambda i, j, switch: jnp.where(j <= switch, jnp.maximum(i - 1, 0), cur(i))
    lag = lambda j, k: jnp.where(j >= k, j - k, j + nj - k)
    up_cols = lambda off: _wspec(w_up, (d, tf), lambda i, j: (0, off + j))
    u_buf = pltpu.VMEM((tm + 2 * F32_SUBLANES, tf), F32)
    act_buf = pltpu.VMEM((tm, tf), BF16)
    return pl.pallas_call(
        functools.partial(_ffn_kernel, tm=tm, n_ff_tiles=nj, n_row_tiles=n_tiles),
        grid=(n_tiles + 1, nj),
        in_specs=[
            pl.BlockSpec((tm, d), lambda i, j: (cur(i), 0)),
            up_cols(0), up_cols(nj),
            pl.BlockSpec((None, None, 4, tf), lambda i, j: (cur(i), j, 0, 0)),
            pl.BlockSpec((None, 2 * (CONV_TAPS + 1), tf), lambda i, j: (lag(j, 1), 0, 0)),
            _wspec(w_down, (tf, d), lambda i, j: (lag(j, 2), 0)),
            pl.BlockSpec((tm, d), lambda i, j: (fin(i, j, 1), 0)),
            pl.BlockSpec((None, None, 1, d), lambda i, j: (row_fn(fin(i, j, 1)), 5, 0, 0)),
            vec,
        ],
        out_specs=pl.BlockSpec((tm, d), lambda i, j: (fin(i, j, 3), 0)),
        out_shape=jax.ShapeDtypeStruct((t, d), F32),
        scratch_shapes=[pltpu.VMEM((tm, d), F32), u_buf, u_buf, u_buf, u_buf, act_buf, act_buf],
        compiler_params=_params("arbitrary", "arbitrary"),
        name="conv_ffn",
    )(h, w_up[0], w_up[0], edges, conv_p, w_down[0], x, mods,
      g_post.reshape(1, d))


def kernel(x, c, ctx, c_ctx, w_ada, b_ada, g_pre_mix, w_in, rpb, w_four, w_out, g_post_mix,
           g_pre_ffn, w_up, conv_w, conv_b, w_down, g_post_ffn):
    b, n, d = x.shape
    n_ctx = ctx.shape[1]
    assert (d, n, n_ctx, b) == (D_MODEL, GRID_W * GRID_W, CTX_LEN, CTX_MOD_ROW)
    t_lat, t_ctx = b * n, b * n_ctx
    tm_lat, tm_ctx = 512, CTX_LEN
    tiles_per_seq = n // tm_lat
    assert tiles_per_seq & (tiles_per_seq - 1) == 0
    lat_row = lambda i: i >> (tiles_per_seq.bit_length() - 1)
    ctx_row = lambda i: CTX_MOD_ROW

    cond = jnp.zeros((MOD_ROWS, d), F32).at[:b].set(c).at[CTX_MOD_ROW].set(c_ctx)
    mods_all = _mods(cond, w_ada, b_ada)

    scale = (n * FOURIER_GROUP_DIM) ** -0.5
    scale_ctx = (n_ctx * FOURIER_GROUP_DIM) ** -0.5
    kk = np.outer(np.arange(FOURIER_GROUP_DIM), np.arange(FOURIER_GROUP_DIM)) % FOURIER_GROUP_DIM
    ang = 2.0 * np.pi * kk / FOURIER_GROUP_DIM
    cc, sc = np.cos(ang), -np.sin(ang)
    mw_lat = _four_w(jnp.asarray(cc * scale, F32), jnp.asarray(sc * scale, F32), w_four)
    dft_lat = _dft_tables(n, GRID_W)

    w_in_b, w_out_b, w_up_b, w_down_b = (w.astype(BF16) for w in (w_in, w_out, w_up, w_down))
    xt = x.reshape(t_lat, d)
    ct = ctx.reshape(t_ctx, d)
    for l in range(DEPTH):
        last = l == DEPTH - 1
        mods = mods_all[l].reshape(MOD_ROWS, N_MOD, 1, d)
        w_in_l, w_out_l, w_up_l, w_down_l = ((w, l) for w in (w_in_b, w_out_b, w_up_b, w_down_b))

        proj = _inproj(xt, g_pre_mix[l], mods, lat_row, w_in_l, tm_lat)
        if last:
            w_kv = (w_in_b[l:l + 1, :, NA_WIDTH:3 * NA_WIDTH], 0)
            cproj = _inproj(ct, g_pre_mix[l], mods, ctx_row, w_kv, tm_ctx)
            kc_blk, vc_blk = 0, N_PAIRS
        else:
            cproj = _inproj(ct, g_pre_mix[l], mods, ctx_row, w_in_l, tm_ctx)
            kc_blk, vc_blk = N_PAIRS, 2 * N_PAIRS
        cproj3 = cproj.reshape(b, n_ctx, cproj.shape[1])

        attn = _attn(proj.reshape(b, n, PROJ_WIDTH), cproj3, _attn_bias_slabs(rpb[l]), kc_blk, vc_blk)
        four = _four_pos(dft_lat, _four_chan(proj, mw_lat[l], 1024), b, 1024, 512)
        xt, h2 = _outproj(attn.reshape(t_lat, NA_WIDTH), four.reshape(t_lat, FOURIER_WIDTH), w_out_l, xt,
                          g_post_mix[l], g_pre_ffn[l], mods, lat_row, tm_lat)
        xt = _ffn(h2, w_up_l, conv_w[l], conv_b[l], w_down_l, xt, g_post_ffn[l], mods, lat_row, n, tm_lat)

        if not last:
            mw_ctx = _four_w(jnp.asarray(cc * scale_ctx, F32), jnp.asarray(sc * scale_ctx, F32),
                             w_four[l:l + 1])[0]
            dft_ctx = _dft_tables(n_ctx, 16)
            attn_c = _ctx_attn(cproj3)
            four_c = _four_pos(dft_ctx, _four_chan(cproj, mw_ctx, 512), b, n_ctx // DFT_RADIX, 512)
            ct, hc2 = _outproj(attn_c.reshape(t_ctx, NA_WIDTH), four_c.reshape(t_ctx, FOURIER_WIDTH),
                               w_out_l, ct, g_post_mix[l], g_pre_ffn[l], mods, ctx_row, tm_ctx)
            ct = _ffn(hc2, w_up_l, conv_w[l], conv_b[l], w_down_l, ct, g_post_ffn[l], mods, ctx_row,
                      n_ctx, tm_ctx)
    return xt.reshape(b, n, d)
```

```python
import functools

import jax
import jax.numpy as jnp
import numpy as np
from jax import lax
from jax.experimental import pallas as pl
from jax.experimental.pallas import tpu as pltpu

D_MODEL = 2048
DEPTH = 2
GRID_W = 64
CTX_LEN = 256
NA_HEADS = 16
HEAD_DIM = 64
NA_WIDTH = NA_HEADS * HEAD_DIM
FOURIER_GROUPS = 8
FOURIER_GROUP_DIM = 128
FOURIER_WIDTH = FOURIER_GROUPS * FOURIER_GROUP_DIM
PROJ_WIDTH = 3 * NA_WIDTH + FOURIER_WIDTH
WIN_ROWS = 8
WIN_COLS = 16
D_FF = 5632
N_MOD = 6
EPS = 1e-6
ATTN_SCALE = HEAD_DIM ** -0.5
MOD_ROWS = 16
CTX_MOD_ROW = 8

LANES = 128
DFT_RADIX = 4
DFT_ROWS = 512
FF_TILE = 512
CONV_TAPS = 3
FINISH_ROWS = 64
UP_K = 512
DOWN_COLS = 512
OUT_COLS = 512
OUT_ROWS = 64
PROJ_COLS = 512
NORM_ROWS = 16
CONV_ROWS = 64
F32_SUBLANES = 8
BF16_SUBLANES = 16
HEAD_PAIR = LANES // HEAD_DIM
N_PAIRS = NA_HEADS // HEAD_PAIR
Q_ROWS = 4
K_ROWS = Q_ROWS + WIN_ROWS - 1
TQ = Q_ROWS * GRID_W
NK = K_ROWS * GRID_W
ATTN_UNROLL = 8
NEG = -1e30
VMEM_LIMIT = 56 * 1024 * 1024

F32 = jnp.float32
BF16 = jnp.bfloat16


def _params(*sem, flags=None):
    return pltpu.CompilerParams(dimension_semantics=sem, vmem_limit_bytes=VMEM_LIMIT, flags=flags)


def _dot(a, b):
    return jnp.dot(a, b, preferred_element_type=F32)


def _dot_nt(a, b):
    return lax.dot_general(a, b, (((1,), (1,)), ((), ())), preferred_element_type=F32)


def _rms(x, g):
    ms = jnp.mean(x * x, axis=-1, keepdims=True)
    return x * lax.rsqrt(ms + EPS) * g


def _norm_mod(x, g, shift, scale):
    return _rms(x, g) * (1.0 + scale) + shift


def _mods_kernel(cond_ref, w_ref, b_ref, o_ref):
    c = cond_ref[...]
    s = (c * jax.nn.sigmoid(c)).astype(BF16)
    o_ref[...] = _dot(s, w_ref[...].astype(BF16)) + b_ref[...]


def _mods(cond, w_ada, b_ada, tn=1024):
    depth, d, n = w_ada.shape
    return pl.pallas_call(
        _mods_kernel,
        grid=(depth, n // tn),
        in_specs=[
            pl.BlockSpec((MOD_ROWS, d), lambda l, j: (0, 0)),
            pl.BlockSpec((None, d, tn), lambda l, j: (l, 0, j)),
            pl.BlockSpec((None, 1, tn), lambda l, j: (l, 0, j)),
        ],
        out_specs=pl.BlockSpec((None, MOD_ROWS, tn), lambda l, j: (l, 0, j)),
        out_shape=jax.ShapeDtypeStruct((depth, MOD_ROWS, n), F32),
        compiler_params=_params("parallel", "parallel"),
        name="adaln_mods",
    )(cond, w_ada, b_ada.reshape(depth, 1, n))


def _wspec(w, block, index_fn, **kw):
    _, layer = w
    return pl.BlockSpec((None,) + tuple(block), lambda *idx: (layer,) + tuple(index_fn(*idx)), **kw)


def _mod_spec(which, row_fn):
    return pl.BlockSpec((None, None, 1, D_MODEL),
                        lambda i, *_: (row_fn(i), which, 0, 0))


def _interleave(pieces, chunks):
    if not pieces:
        for chunk in chunks:
            chunk()
        return
    slots = max(len(pieces) - 1, 1)
    per = -(-len(chunks) // slots)
    for n, piece in enumerate(pieces):
        piece()
        for chunk in chunks[n * per:(n + 1) * per]:
            chunk()


def _order_before_reads(ref, value):
    tile = (slice(0, BF16_SUBLANES), slice(0, LANES))
    bits = pltpu.bitcast(value[tile], jnp.uint32)
    zero = pltpu.bitcast((bits >> 16) >> 16, F32)
    ref[tile] = (ref[tile].astype(F32) + zero).astype(ref.dtype)


def _two_stage(i, n_tiles, run):
    @pl.when(i == 0)
    def _():
        run(0, None)

    for parity in range(2):
        @pl.when((i >= 1) & (i < n_tiles) & (i % 2 == parity))
        def _():
            run(parity, 1 - parity)

    @pl.when(i == n_tiles)
    def _():
        run(None, (n_tiles - 1) % 2)


def _inproj_kernel(x_ref, g_ref, sh_ref, sc_ref, w_ref, o_ref, h0_ref, h1_ref, *, n_tiles):
    h_refs = (h0_ref, h1_ref)
    tm = x_ref.shape[0]

    def run(norm_to, dot_from):
        pieces, chunks = [], []
        if dot_from is not None:
            for c0 in range(0, w_ref.shape[1], PROJ_COLS):
                def piece(cols=slice(c0, c0 + PROJ_COLS)):
                    o_ref[:, cols] = _dot(h_refs[dot_from][...], w_ref[:, cols]).astype(o_ref.dtype)
                pieces.append(piece)
        if norm_to is not None:
            for r0 in range(0, tm, NORM_ROWS):
                def chunk(rows=slice(r0, r0 + NORM_ROWS)):
                    val = _norm_mod(x_ref[rows, :], g_ref[...], sh_ref[...], sc_ref[...])
                    h_refs[norm_to][rows, :] = val.astype(BF16)
                    if dot_from is not None:
                        _order_before_reads(h_refs[dot_from], val)
                chunks.append(chunk)
        _interleave(pieces, chunks)

    _two_stage(pl.program_id(0), n_tiles, run)


def _inproj(x, g, mods, row_fn, w, tm):
    t, d = x.shape
    n = w[0].shape[2]
    n_tiles = t // tm
    cur = lambda i: jnp.minimum(i, n_tiles - 1)
    prev = lambda i: jnp.maximum(i - 1, 0)
    return pl.pallas_call(
        functools.partial(_inproj_kernel, n_tiles=n_tiles),
        grid=(n_tiles + 1,),
        in_specs=[
            pl.BlockSpec((tm, d), lambda i: (cur(i), 0)),
            pl.BlockSpec((1, d), lambda i: (0, 0)),
            _mod_spec(0, lambda i: row_fn(cur(i))),
            _mod_spec(1, lambda i: row_fn(cur(i))),
            _wspec(w, (d, n), lambda i: (0, 0), pipeline_mode=pl.Buffered(1)),
        ],
        out_specs=pl.BlockSpec((tm, n), lambda i: (prev(i), 0)),
        out_shape=jax.ShapeDtypeStruct((t, n), BF16),
        scratch_shapes=[pltpu.VMEM((tm, d), BF16)] * 2,
        compiler_params=_params("arbitrary"),
        name="norm_inproj",
    )(x, g.reshape(1, d), mods, mods, w[0])


def _softmax_pv(parts):
    m = functools.reduce(jnp.maximum, [s.max(axis=-1, keepdims=True) for s, _ in parts])
    num = 0.0
    den = 0.0
    for s, v in parts:
        p = jnp.exp(s - m)
        den = den + p.sum(axis=-1, keepdims=True)
        num = num + _dot(p.astype(BF16), v)
    return num / den


def _head_masks():
    lane = lax.broadcasted_iota(jnp.int32, (1, LANES), 1)
    first = lane < HEAD_DIM
    return first, [first, jnp.logical_not(first)]


def _attn_kernel(q_ref, k_ref, v_ref, kc_ref, vc_ref, slab_ref, o_ref, tab_ref, *, n_tiles, rows):
    first, sels = _head_masks()
    kc = kc_ref[...]
    vc = vc_ref[...]

    @pl.when(pl.program_id(1) == 0)
    def _():
        pattern = _attn_slab_pattern(rows)
        for p in range(pattern.shape[0]):
            for hh in range(HEAD_PAIR):
                for qr in range(Q_ROWS):
                    for kr in range(K_ROWS):
                        tab_ref[p, hh, qr * GRID_W:(qr + 1) * GRID_W, kr * GRID_W:(kr + 1) * GRID_W] = (
                            slab_ref[hh, int(pattern[p, qr, kr])])

    def scores(t, hh):
        q0 = pl.multiple_of(t * TQ, TQ)
        k0 = pl.multiple_of(jnp.clip(t * Q_ROWS - WIN_ROWS // 2, 0, rows - K_ROWS) * GRID_W, GRID_W)
        pat = jnp.where(t == 0, 0, jnp.where(t == n_tiles - 1, 2, 1))
        q = q_ref[pl.ds(q0, TQ), :] * ATTN_SCALE
        qm = jnp.where(sels[hh], q, jnp.zeros_like(q))
        s_loc = _dot_nt(qm, k_ref[pl.ds(k0, NK), :]) + tab_ref[pat, hh]
        s_ctx = _dot_nt(qm, kc)
        return k0, s_loc, s_ctx

    def tile_group(tg, carry):
        units = [(tg * ATTN_UNROLL + ti, hh) for ti in range(ATTN_UNROLL) for hh in range(HEAD_PAIR)]
        pending = None
        outs = {}
        for unit in units + [None]:
            nxt = scores(*unit) if unit is not None else None
            if pending is not None:
                (t, hh), (k0, s_loc, s_ctx) = pending
                outs[hh] = _softmax_pv([(s_loc, v_ref[pl.ds(k0, NK), :]), (s_ctx, vc)])
                if hh == HEAD_PAIR - 1:
                    q0 = pl.multiple_of(t * TQ, TQ)
                    o_ref[pl.ds(q0, TQ), :] = jnp.where(first, outs[0], outs[1]).astype(o_ref.dtype)
            pending = (unit, nxt)
        return carry

    lax.fori_loop(0, n_tiles // ATTN_UNROLL, tile_group, 0)


def _attn(proj, cproj, slabs, kc_blk, vc_blk):
    b, n, _ = proj.shape
    rows = n // GRID_W
    n_tiles = rows // Q_ROWS
    lat = lambda off: pl.BlockSpec((None, n, LANES), lambda p, i: (i, 0, off + p))
    cx = lambda off: pl.BlockSpec((None, CTX_LEN, LANES), lambda p, i: (i, 0, off + p))
    return pl.pallas_call(
        functools.partial(_attn_kernel, n_tiles=n_tiles, rows=rows),
        grid=(N_PAIRS, b),
        in_specs=[
            lat(0), lat(N_PAIRS), lat(2 * N_PAIRS), cx(kc_blk), cx(vc_blk),
            pl.BlockSpec((HEAD_PAIR,) + slabs.shape[1:], lambda p, i: (p, 0, 0, 0)),
        ],
        out_specs=pl.BlockSpec((None, n, LANES), lambda p, i: (i, 0, p)),
        out_shape=jax.ShapeDtypeStruct((b, n, NA_WIDTH), BF16),
        scratch_shapes=[pltpu.VMEM((3, HEAD_PAIR, TQ, NK), F32)],
        compiler_params=_params("arbitrary", "arbitrary"),
        name="nbr_attn",
    )(proj, proj, proj, cproj, cproj, slabs)


def _ctx_attn_kernel(q_ref, k_ref, v_ref, o_ref):
    first, sels = _head_masks()
    q = q_ref[...] * ATTN_SCALE
    k = k_ref[...]
    v = v_ref[...]
    outs = []
    for hh in range(HEAD_PAIR):
        qm = jnp.where(sels[hh], q, jnp.zeros_like(q))
        outs.append(_softmax_pv([(_dot_nt(qm, k), v)]))
    o_ref[...] = jnp.where(first, outs[0], outs[1]).astype(o_ref.dtype)


def _ctx_attn(cproj):
    b, n, _ = cproj.shape
    spec = lambda off: pl.BlockSpec((None, n, LANES), lambda i, p: (i, 0, off + p))
    return pl.pallas_call(
        _ctx_attn_kernel,
        grid=(b, N_PAIRS),
        in_specs=[spec(0), spec(N_PAIRS), spec(2 * N_PAIRS)],
        out_specs=spec(0),
        out_shape=jax.ShapeDtypeStruct((b, n, NA_WIDTH), BF16),
        compiler_params=_params("parallel", "parallel"),
        name="ctx_attn",
    )(cproj, cproj, cproj)


MASKED_SLAB = 2 * WIN_ROWS - 1


def _attn_bias_slabs(rpb):
    qc = np.arange(GRID_W)[:, None]
    kc = np.arange(GRID_W)[None, :]
    cs = np.clip(qc - WIN_COLS // 2, 0, GRID_W - WIN_COLS)
    col_ok = (kc >= cs) & (kc < cs + WIN_COLS)
    dc = kc - qc + WIN_COLS - 1
    onehot = np.zeros((2 * WIN_COLS - 1, GRID_W, GRID_W), np.float32)
    qi, ki = np.nonzero(col_ok)
    onehot[dc[qi, ki], qi, ki] = 1.0
    t1 = jnp.einsum('hrd,dqk->hrqk', rpb, jnp.asarray(onehot), precision=lax.Precision.HIGHEST)
    t1 = jnp.where(jnp.asarray(col_ok)[None, None], t1, NEG)
    neg_slab = jnp.full((NA_HEADS, 1, GRID_W, GRID_W), NEG, F32)
    return jnp.concatenate([t1, neg_slab], axis=1)


def _attn_slab_pattern(rows):
    n_tiles = rows // Q_ROWS
    idx = np.full((3, Q_ROWS, K_ROWS), MASKED_SLAB, np.int32)
    for p, t in enumerate((0, 1, n_tiles - 1)):
        k_start = int(np.clip(t * Q_ROWS - WIN_ROWS // 2, 0, rows - K_ROWS))
        for qr in range(Q_ROWS):
            r = t * Q_ROWS + qr
            rs = int(np.clip(r - WIN_ROWS // 2, 0, rows - WIN_ROWS))
            for kr in range(K_ROWS):
                krow = k_start + kr
                if rs <= krow < rs + WIN_ROWS:
                    idx[p, qr, kr] = krow - r + WIN_ROWS - 1
    return idx


def _split_bf16(a):
    hi = a.astype(BF16)
    lo = (a - hi.astype(F32)).astype(BF16)
    return hi, lo


def _dot3(a, b):
    ah, al = _split_bf16(a)
    bh, bl = _split_bf16(b)
    return _dot(ah, bh) + (_dot(ah, bl) + _dot(al, bh))


def _four_w_kernel(cc_ref, sc_ref, w_ref, o_ref):
    w = w_ref[...]
    o_ref[:, :FOURIER_GROUP_DIM] = _dot3(cc_ref[...], w).astype(BF16)
    o_ref[:, FOURIER_GROUP_DIM:] = _dot3(sc_ref[...], w).astype(BF16)


def _four_w(cc, sc, w_four):
    depth, g, c, _ = w_four.shape
    tab = pl.BlockSpec((c, c), lambda l, i: (0, 0))
    return pl.pallas_call(
        _four_w_kernel,
        grid=(depth, g),
        in_specs=[tab, tab, pl.BlockSpec((None, None, c, c), lambda l, i: (l, i, 0, 0))],
        out_specs=pl.BlockSpec((None, None, c, 2 * c), lambda l, i: (l, i, 0, 0)),
        out_shape=jax.ShapeDtypeStruct((depth, g, c, 2 * c), BF16),
        compiler_params=_params("parallel", "parallel"),
        name="fourier_weights",
    )(cc, sc, w_four)


def _four_chan_kernel(f_ref, m_ref, *refs):
    out_refs, (yc_scr, ys_scr) = refs[:2 * DFT_RADIX], refs[2 * DFT_RADIX:]
    c = FOURIER_GROUP_DIM
    part_rows = f_ref.shape[0] // DFT_RADIX
    for g in range(FOURIER_GROUPS):
        cols = slice(g * c, (g + 1) * c)
        y = _dot(f_ref[:, cols], m_ref[g])
        yc_scr[...] = y[:, :c]
        ys_scr[...] = y[:, c:]
        for p in range(DFT_RADIX):
            out_refs[2 * p][:, cols] = yc_scr[pl.ds(p, part_rows, stride=DFT_RADIX), :].astype(BF16)
            out_refs[2 * p + 1][:, cols] = ys_scr[pl.ds(p, part_rows, stride=DFT_RADIX), :].astype(BF16)


def _four_chan(proj, mw, tm):
    t = proj.shape[0]
    f_blk = proj.shape[1] // FOURIER_WIDTH - 1
    out = pl.BlockSpec((tm // DFT_RADIX, FOURIER_WIDTH), lambda i: (i, 0))
    return pl.pallas_call(
        _four_chan_kernel,
        grid=(t // tm,),
        in_specs=[
            pl.BlockSpec((tm, FOURIER_WIDTH), lambda i: (i, f_blk)),
            pl.BlockSpec(mw.shape, lambda i: (0, 0, 0)),
        ],
        out_specs=[out] * (2 * DFT_RADIX),
        out_shape=[jax.ShapeDtypeStruct((t // DFT_RADIX, FOURIER_WIDTH), BF16)] * (2 * DFT_RADIX),
        scratch_shapes=[pltpu.VMEM((tm, FOURIER_GROUP_DIM), F32)] * 2,
        compiler_params=_params("parallel"),
        name="fourier_chan",
    )(proj, mw)


def _four_pos_kernel(c0, s0, c1, s1, c2, s2, c3, s3, ns1, nc3, a0, b0, a1, b1, a2, b2, a3, b3, o_ref):
    step = min(DFT_ROWS, c0.shape[0])
    for r0 in range(0, c0.shape[0], step):
        rows = slice(r0, r0 + step)
        dot = lambda table, data: _dot(table[rows, :], data[...])
        re0 = dot(c0, a0) + dot(s0, b0)
        re2 = dot(c2, a2) + dot(s2, b2)
        re13 = dot(c1, a1) + dot(s1, b1) + dot(c3, a3) + dot(s3, b3)
        im13 = dot(c1, b1) + dot(ns1, a1) + dot(nc3, b3) + dot(s3, a3)
        plus, minus = re0 + re2, re0 - re2
        o_ref[0, rows, :] = (plus + re13).astype(o_ref.dtype)
        o_ref[1, rows, :] = (minus + im13).astype(o_ref.dtype)
        o_ref[2, rows, :] = (plus - re13).astype(o_ref.dtype)
        o_ref[3, rows, :] = (minus - im13).astype(o_ref.dtype)


def _four_pos(tables, y_parts, b, tm, tn):
    data = [y.reshape(b, y.shape[0] // b, y.shape[1]) for y in y_parts]
    _, part, w = data[0].shape
    mat = pl.BlockSpec((tm, part), lambda i, bb, j: (i, 0), pipeline_mode=pl.Buffered(1))
    dat = pl.BlockSpec((None, part, tn), lambda i, bb, j: (bb, 0, j))
    out = pl.pallas_call(
        _four_pos_kernel,
        grid=(part // tm, b, w // tn),
        in_specs=[mat] * len(tables) + [dat] * len(data),
        out_specs=pl.BlockSpec((None, DFT_RADIX, tm, tn), lambda i, bb, j: (bb, 0, i, j)),
        out_shape=jax.ShapeDtypeStruct((b, DFT_RADIX, part, w), BF16),
        compiler_params=_params("parallel", "parallel", "parallel"),
        name="fourier_pos",
    )(*tables, *data)
    return out.reshape(b, DFT_RADIX * part, w)


def _dft_tables(n, split):
    part = n // DFT_RADIX
    j = jnp.arange(part, dtype=jnp.int32)[:, None]
    hi = jnp.arange(part // split, dtype=jnp.int32)[None, :] * split
    lo = jnp.arange(split, dtype=jnp.int32)[None, :]
    ang = lambda k: (2.0 * np.pi / part) * ((j * k) % part).astype(F32)
    ch, sh, cl, sl = jnp.cos(ang(hi)), jnp.sin(ang(hi)), jnp.cos(ang(lo)), jnp.sin(ang(lo))
    c0 = (ch[:, :, None] * cl[:, None, :] - sh[:, :, None] * sl[:, None, :]).reshape(part, part)
    s0 = (sh[:, :, None] * cl[:, None, :] + ch[:, :, None] * sl[:, None, :]).reshape(part, part)
    tables = [c0, s0]
    for p in range(1, DFT_RADIX):
        phi = (2.0 * np.pi * p / n) * j.astype(F32)
        cp, sp = jnp.cos(phi), jnp.sin(phi)
        tables += [c0 * cp - s0 * sp, s0 * cp + c0 * sp]
    tables += [-tables[3], -tables[6]]
    return tuple(t.astype(BF16) for t in tables)


def _outproj_kernel(a_ref, f_ref, w_ref, x_ref, gpost_ref, gate_ref, gpre_ref, sh_ref, sc_ref,
                    xo_ref, h_ref, m0_ref, m1_ref, lhs_ref, *, n_tiles):
    mix_refs = (m0_ref, m1_ref)
    tm = x_ref.shape[0]

    def run(mix_to, epi_from):
        pieces, chunks = [], []
        if mix_to is not None:
            lhs_ref[:, :NA_WIDTH] = a_ref[...]
            lhs_ref[:, NA_WIDTH:] = f_ref[...]
            for c0 in range(0, w_ref.shape[1], OUT_COLS):
                def piece(cols=slice(c0, c0 + OUT_COLS)):
                    mix_refs[mix_to][:, cols] = _dot(lhs_ref[...], w_ref[:, cols])
                pieces.append(piece)
        if epi_from is not None:
            for r0 in range(0, tm, OUT_ROWS):
                def chunk(rows=slice(r0, r0 + OUT_ROWS)):
                    x = x_ref[rows, :] + gate_ref[...] * _rms(mix_refs[epi_from][rows, :], gpost_ref[...])
                    xo_ref[rows, :] = x
                    h = _norm_mod(x, gpre_ref[...], sh_ref[...], sc_ref[...])
                    h_ref[rows, :] = h.astype(BF16)
                    if mix_to is not None:
                        _order_before_reads(lhs_ref, h)
                chunks.append(chunk)
        _interleave(pieces, chunks)

    _two_stage(pl.program_id(0), n_tiles, run)


def _outproj(attn, four, w, x, g_post, g_pre, mods, row_fn, tm):
    t, d = x.shape
    n_tiles = t // tm
    cur = lambda i: jnp.minimum(i, n_tiles - 1)
    prev = lambda i: jnp.maximum(i - 1, 0)
    vec = pl.BlockSpec((1, d), lambda i: (0, 0))
    row = pl.BlockSpec((tm, d), lambda i: (prev(i), 0))
    half = pl.BlockSpec((tm, NA_WIDTH), lambda i: (cur(i), 0))
    mod = lambda which: _mod_spec(which, lambda i: row_fn(prev(i)))
    return pl.pallas_call(
        functools.partial(_outproj_kernel, n_tiles=n_tiles),
        grid=(n_tiles + 1,),
        in_specs=[half, half, _wspec(w, w[0].shape[1:], lambda i: (0, 0), pipeline_mode=pl.Buffered(1)), row, vec,
                  mod(2), vec, mod(3), mod(4)],
        out_specs=[row, row],
        out_shape=[jax.ShapeDtypeStruct((t, d), F32), jax.ShapeDtypeStruct((t, d), BF16)],
        scratch_shapes=[pltpu.VMEM((tm, d), F32), pltpu.VMEM((tm, d), F32),
                        pltpu.VMEM((tm, NA_WIDTH + FOURIER_WIDTH), BF16)],
        compiler_params=_params("arbitrary"),
        name="outproj_residual",
    )(attn, four, w[0], x, g_post.reshape(1, d), mods, g_pre.reshape(1, d), mods, mods)


def _matmul_kernel(x_ref, w_ref, o_ref):
    o_ref[...] = _dot(x_ref[...], w_ref[...]).astype(o_ref.dtype)


def _matmul(x, w, tn, out_dtype, name):
    m, k = x.shape
    n = w[0].shape[2]
    return pl.pallas_call(
        _matmul_kernel,
        grid=(n // tn,),
        in_specs=[pl.BlockSpec((m, k), lambda j: (0, 0)), _wspec(w, (k, tn), lambda j: (0, j))],
        out_specs=pl.BlockSpec((m, tn), lambda j: (0, j)),
        out_shape=jax.ShapeDtypeStruct((m, n), out_dtype),
        compiler_params=_params("parallel"),
        name=name,
    )(x, w[0])


def _ffn_kernel(h_ref, wa_ref, wg_ref, edge_ref, cp_ref, wd_ref, x_ref, gate_ref, gpost_ref, o_ref,
                acc_ref, ua0, ug0, ua1, ug1, act0, act1,
                *, tm, n_ff_tiles, n_row_tiles, seq_len):
    i = pl.program_id(0)
    j = pl.program_id(1)
    nj = n_ff_tiles
    pad = F32_SUBLANES
    tf = wa_ref.shape[1]
    u_refs = ((ua0, ug0), (ua1, ug1))
    act_refs = (act0, act1)

    def up_proj_part(half, k0):
        w_ref = (wa_ref, wg_ref)[half]
        return _dot(h_ref[:, k0:k0 + UP_K], w_ref[k0:k0 + UP_K, :])

    def up_proj_store(parity, half, u):
        u_ref = u_refs[parity][half]
        u_ref[0:pad, :] = jnp.broadcast_to(edge_ref[half:half + 1, :], (pad, tf))
        u_ref[pad:pad + tm, :] = u
        u_ref[pad + tm:, :] = jnp.broadcast_to(edge_ref[2 + half:3 + half, :], (pad, tf))

    def conv(u_ref, r0, cols, p0):
        win = u_ref[r0:r0 + CONV_ROWS + 2 * pad, cols]
        mid = win[pad:pad + CONV_ROWS]
        down = pltpu.roll(win, 1, axis=0)[pad:pad + CONV_ROWS]
        up = pltpu.roll(win, CONV_ROWS + 2 * pad - 1, axis=0)[pad:pad + CONV_ROWS]
        row = lax.broadcasted_iota(jnp.int32, (CONV_ROWS, 1), 0)
        if r0 > 0 and r0 % seq_len == 0:
            down = jnp.where(row == 0, 0.0, down)
        if r0 + CONV_ROWS < tm and (r0 + CONV_ROWS) % seq_len == 0:
            up = jnp.where(row == CONV_ROWS - 1, 0.0, up)
        return (down * cp_ref[p0:p0 + 1, cols] + mid * cp_ref[p0 + 1:p0 + 2, cols]
                + up * cp_ref[p0 + 2:p0 + 3, cols] + cp_ref[p0 + 3:p0 + 4, cols])

    def conv_gate(parity, r0, c0):
        ua_ref, ug_ref = u_refs[parity]
        cols = slice(c0, c0 + LANES)
        a = conv(ua_ref, r0, cols, 0)
        g = conv(ug_ref, r0, cols, CONV_TAPS + 1)
        act_refs[parity][r0:r0 + CONV_ROWS, cols] = (g * jax.nn.sigmoid(g) * a).astype(BF16)

    def down_proj(parity, cols):
        return _dot(act_refs[parity][...], wd_ref[:, cols])

    chunks = [(r0, c0) for r0 in range(0, tm, CONV_ROWS) for c0 in range(0, tf, LANES)]
    d_out = wd_ref.shape[1]
    down_groups = [slice(c, c + DOWN_COLS) for c in range(0, d_out, DOWN_COLS)]

    def finish_rows(r0):
        rows = slice(r0, r0 + FINISH_ROWS)
        o_ref[rows, :] = x_ref[rows, :] + gate_ref[...] * _rms(acc_ref[rows, :], gpost_ref[...])
        acc_ref[rows, :] = jnp.zeros((FINISH_ROWS, d_out), F32)

    def run_step(up=None, conv_of=None, down=None, finish=False):
        pieces = []
        if down is not None:
            for cols in down_groups:
                def piece(cols=cols):
                    acc_ref[:, cols] += down_proj(down, cols)
                pieces.append((piece, tf * DOWN_COLS))
        n_down = len(pieces)
        if up is not None:
            d_in = h_ref.shape[1]
            for half in range(2):
                partial_u = []
                for k0 in range(0, d_in, UP_K):
                    def piece(half=half, k0=k0, partial_u=partial_u):
                        part = up_proj_part(half, k0)
                        partial_u[:] = [part if not partial_u else partial_u[0] + part]
                        if k0 + UP_K == d_in:
                            up_proj_store(up, half, partial_u[0])
                    pieces.append((piece, UP_K * tf))
        conv_todo = [functools.partial(conv_gate, conv_of, *c) for c in chunks] if conv_of is not None else []
        fin_todo = [functools.partial(finish_rows, r0) for r0 in range(0, tm, FINISH_ROWS)] if finish else []
        n_conv, n_fin = len(conv_todo), len(fin_todo)
        total = sum(w for _, w in pieces)
        total_up = sum(w for _, w in pieces[n_down:])
        done = done_up = 0.0
        for n, (piece, w) in enumerate(pieces):
            piece()
            done += w
            is_last = n == len(pieces) - 1
            conv_upto = n_conv if is_last else round(n_conv * done / total)
            while n_conv - len(conv_todo) < conv_upto:
                conv_todo.pop(0)()
            if n >= n_down:
                done_up += w
                fin_upto = n_fin if is_last else round(n_fin * done_up / total_up)
                while n_fin - len(fin_todo) < fin_upto:
                    fin_todo.pop(0)()
        for chunk in conv_todo + fin_todo:
            chunk()

    s = i * nj + j
    last = n_row_tiles * nj
    finishing = (j == 1) & (i >= 1)

    @pl.when(s == 0)
    def _():
        acc_ref[...] = jnp.zeros_like(acc_ref)
        run_step(up=0)

    @pl.when(s == 1)
    def _():
        run_step(up=1, conv_of=0)

    for parity in range(2):
        steady = (s >= 2) & (s < last) & (s % 2 == parity)

        @pl.when(steady & jnp.logical_not(finishing))
        def _():
            run_step(up=parity, conv_of=1 - parity, down=parity)

        @pl.when(steady & finishing)
        def _():
            run_step(up=parity, conv_of=1 - parity, down=parity, finish=True)

    @pl.when(s == last)
    def _():
        run_step(conv_of=(last - 1) % 2, down=last % 2)

    @pl.when(s == last + 1)
    def _():
        run_step(down=(last + 1) % 2, finish=True)


def _ffn(h, w_up, conv_w, conv_b, w_down, x, g_post, mods, row_fn, seq_len, tm):
    t, d = x.shape
    ff = w_down[0].shape[1]
    tf = FF_TILE
    nj = ff // tf
    n_tiles = t // tm
    assert seq_len % tm == 0 or (tm % seq_len == 0 and seq_len % CONV_ROWS == 0)

    tile0 = np.arange(n_tiles) * tm
    start = tile0 % seq_len == 0
    end = (tile0 + tm) % seq_len == 0
    prev = jnp.take(h, jnp.asarray(np.maximum(tile0 - 1, 0)), axis=0)
    nxt = jnp.take(h, jnp.asarray(np.minimum(tile0 + tm, t - 1)), axis=0)
    prev = jnp.where(jnp.asarray(start)[:, None], jnp.zeros_like(prev), prev)
    nxt = jnp.where(jnp.asarray(end)[:, None], jnp.zeros_like(nxt), nxt)
    edge_rows = jnp.concatenate([prev, nxt], axis=0)
    pad = -edge_rows.shape[0] % BF16_SUBLANES
    edge_rows = jnp.pad(edge_rows, ((0, pad), (0, 0)))
    u_edge = _matmul(edge_rows, w_up, 2 * ff // 8, F32, "conv_ffn_edges")
    u_prev, u_next = u_edge[:n_tiles], u_edge[n_tiles:2 * n_tiles]
    edges = jnp.stack([u_prev[:, :ff], u_prev[:, ff:], u_next[:, :ff], u_next[:, ff:]], axis=1)
    edges = edges.reshape(n_tiles, 4, nj, tf).transpose(0, 2, 1, 3)
    conv_p = jnp.concatenate([conv_w[:, :ff], conv_b[None, :ff], conv_w[:, ff:], conv_b[None, ff:]], axis=0)
    conv_p = conv_p.reshape(2 * (CONV_TAPS + 1), nj, tf).transpose(1, 0, 2)

    assert nj >= 3
    vec = pl.BlockSpec((1, d), lambda i, j: (0, 0))
    cur = lambda i: jnp.minimum(i, n_tiles - 1)
    fin = lambda i, j, switch: jnp.where(j <= switch, jnp.maximum(i - 1, 0), cur(i))
    lag = lambda j, k: jnp.where(j >= k, j - k, j + nj - k)
    up_cols = lambda off: _wspec(w_up, (d, tf), lambda i, j: (0, off + j))
    u_buf = pltpu.VMEM((tm + 2 * F32_SUBLANES, tf), F32)
    act_buf = pltpu.VMEM((tm, tf), BF16)
    return pl.pallas_call(
        functools.partial(_ffn_kernel, tm=tm, n_ff_tiles=nj, n_row_tiles=n_tiles, seq_len=seq_len),
        grid=(n_tiles + 1, nj),
        in_specs=[
            pl.BlockSpec((tm, d), lambda i, j: (cur(i), 0)),
            up_cols(0), up_cols(nj),
            pl.BlockSpec((None, None, 4, tf), lambda i, j: (cur(i), j, 0, 0)),
            pl.BlockSpec((None, 2 * (CONV_TAPS + 1), tf), lambda i, j: (lag(j, 1), 0, 0)),
            _wspec(w_down, (tf, d), lambda i, j: (lag(j, 2), 0)),
            pl.BlockSpec((tm, d), lambda i, j: (fin(i, j, 1), 0)),
            pl.BlockSpec((None, None, 1, d), lambda i, j: (row_fn(fin(i, j, 1)), 5, 0, 0)),
            vec,
        ],
        out_specs=pl.BlockSpec((tm, d), lambda i, j: (fin(i, j, 3), 0)),
        out_shape=jax.ShapeDtypeStruct((t, d), F32),
        scratch_shapes=[pltpu.VMEM((tm, d), F32), u_buf, u_buf, u_buf, u_buf, act_buf, act_buf],
        compiler_params=_params("arbitrary", "arbitrary"),
        name="conv_ffn",
    )(h, w_up[0], w_up[0], edges, conv_p, w_down[0], x, mods,
      g_post.reshape(1, d))


def kernel(x, c, ctx, c_ctx, w_ada, b_ada, g_pre_mix, w_in, rpb, w_four, w_out, g_post_mix,
           g_pre_ffn, w_up, conv_w, conv_b, w_down, g_post_ffn):
    b, n, d = x.shape
    n_ctx = ctx.shape[1]
    assert (d, n, n_ctx, b) == (D_MODEL, GRID_W * GRID_W, CTX_LEN, CTX_MOD_ROW)
    t_lat, t_ctx = b * n, b * n_ctx
    tm_lat, tm_ctx = 512, 512
    tiles_per_seq = n // tm_lat
    assert tiles_per_seq & (tiles_per_seq - 1) == 0
    lat_row = lambda i: i >> (tiles_per_seq.bit_length() - 1)
    ctx_row = lambda i: CTX_MOD_ROW

    cond = jnp.zeros((MOD_ROWS, d), F32).at[:b].set(c).at[CTX_MOD_ROW].set(c_ctx)
    mods_all = _mods(cond, w_ada, b_ada)

    scale = (n * FOURIER_GROUP_DIM) ** -0.5
    scale_ctx = (n_ctx * FOURIER_GROUP_DIM) ** -0.5
    kk = np.outer(np.arange(FOURIER_GROUP_DIM), np.arange(FOURIER_GROUP_DIM)) % FOURIER_GROUP_DIM
    ang = 2.0 * np.pi * kk / FOURIER_GROUP_DIM
    cc, sc = np.cos(ang), -np.sin(ang)
    mw_lat = _four_w(jnp.asarray(cc * scale, F32), jnp.asarray(sc * scale, F32), w_four)
    dft_lat = _dft_tables(n, GRID_W)

    w_in_b, w_out_b, w_up_b, w_down_b = (w.astype(BF16) for w in (w_in, w_out, w_up, w_down))
    xt = x.reshape(t_lat, d)
    ct = ctx.reshape(t_ctx, d)
    for l in range(DEPTH):
        last = l == DEPTH - 1
        mods = mods_all[l].reshape(MOD_ROWS, N_MOD, 1, d)
        w_in_l, w_out_l, w_up_l, w_down_l = ((w, l) for w in (w_in_b, w_out_b, w_up_b, w_down_b))

        proj = _inproj(xt, g_pre_mix[l], mods, lat_row, w_in_l, tm_lat)
        if last:
            w_kv = (w_in_b[l:l + 1, :, NA_WIDTH:3 * NA_WIDTH], 0)
            cproj = _inproj(ct, g_pre_mix[l], mods, ctx_row, w_kv, tm_ctx)
            kc_blk, vc_blk = 0, N_PAIRS
        else:
            cproj = _inproj(ct, g_pre_mix[l], mods, ctx_row, w_in_l, tm_ctx)
            kc_blk, vc_blk = N_PAIRS, 2 * N_PAIRS
        cproj3 = cproj.reshape(b, n_ctx, cproj.shape[1])

        attn = _attn(proj.reshape(b, n, PROJ_WIDTH), cproj3, _attn_bias_slabs(rpb[l]), kc_blk, vc_blk)
        four = _four_pos(dft_lat, _four_chan(proj, mw_lat[l], 1024), b, 1024, 512)
        xt, h2 = _outproj(attn.reshape(t_lat, NA_WIDTH), four.reshape(t_lat, FOURIER_WIDTH), w_out_l, xt,
                          g_post_mix[l], g_pre_ffn[l], mods, lat_row, tm_lat)
        xt = _ffn(h2, w_up_l, conv_w[l], conv_b[l], w_down_l, xt, g_post_ffn[l], mods, lat_row, n, tm_lat)

        if not last:
            mw_ctx = _four_w(jnp.asarray(cc * scale_ctx, F32), jnp.asarray(sc * scale_ctx, F32),
                             w_four[l:l + 1])[0]
            dft_ctx = _dft_tables(n_ctx, 16)
            attn_c = _ctx_attn(cproj3)
            four_c = _four_pos(dft_ctx, _four_chan(cproj, mw_ctx, 512), b, n_ctx // DFT_RADIX, 512)
            ct, hc2 = _outproj(attn_c.reshape(t_ctx, NA_WIDTH), four_c.reshape(t_ctx, FOURIER_WIDTH),
                               w_out_l, ct, g_post_mix[l], g_pre_ffn[l], mods, ctx_row, tm_ctx)
            ct = _ffn(hc2, w_up_l, conv_w[l], conv_b[l], w_down_l, ct, g_post_ffn[l], mods, ctx_row,
                      n_ctx, tm_ctx)
    return xt.reshape(b, n, d)
```

```python
import functools

import jax
import jax.numpy as jnp
import numpy as np
from jax import lax
from jax.experimental import pallas as pl
from jax.experimental.pallas import tpu as pltpu

D_MODEL = 2048
DEPTH = 2
GRID_W = 64
CTX_LEN = 256
NA_HEADS = 16
HEAD_DIM = 64
NA_WIDTH = NA_HEADS * HEAD_DIM
FOURIER_GROUPS = 8
FOURIER_GROUP_DIM = 128
FOURIER_WIDTH = FOURIER_GROUPS * FOURIER_GROUP_DIM
PROJ_WIDTH = 3 * NA_WIDTH + FOURIER_WIDTH
WIN_ROWS = 8
WIN_COLS = 16
D_FF = 5632
N_MOD = 6
EPS = 1e-6
ATTN_SCALE = HEAD_DIM ** -0.5
MOD_ROWS = 16
CTX_MOD_ROW = 8

ROW_TILE = 512
CHAN_TILE = 1024
DFT_COLS = 512
MODS_COLS = 1024
EDGE_STEPS = 8
LANES = 128
DFT_RADIX = 4
DFT_ROWS = 512
FF_TILE = 512
CONV_TAPS = 3
FINISH_ROWS = 64
UP_K = 512
DOWN_COLS = 512
OUT_COLS = 512
OUT_ROWS = 64
PROJ_COLS = 512
NORM_ROWS = 16
CONV_ROWS = 64
F32_SUBLANES = 8
BF16_SUBLANES = 16
HEAD_PAIR = LANES // HEAD_DIM
N_PAIRS = NA_HEADS // HEAD_PAIR
Q_ROWS = 4
K_ROWS = Q_ROWS + WIN_ROWS - 1
TQ = Q_ROWS * GRID_W
NK = K_ROWS * GRID_W
ATTN_UNROLL = 8
NEG = -1e30
VMEM_LIMIT = 56 * 1024 * 1024

F32 = jnp.float32
BF16 = jnp.bfloat16


def _params(*sem, flags=None):
    return pltpu.CompilerParams(dimension_semantics=sem, vmem_limit_bytes=VMEM_LIMIT, flags=flags)


def _dot(a, b):
    return jnp.dot(a, b, preferred_element_type=F32)


def _dot_nt(a, b):
    return lax.dot_general(a, b, (((1,), (1,)), ((), ())), preferred_element_type=F32)


def _rms(x, g):
    ms = jnp.mean(x * x, axis=-1, keepdims=True)
    return x * lax.rsqrt(ms + EPS) * g


def _norm_mod(x, g, shift, scale):
    return _rms(x, g) * (1.0 + scale) + shift


def _mods_kernel(cond_ref, w_ref, b_ref, o_ref):
    c = cond_ref[...]
    s = (c * jax.nn.sigmoid(c)).astype(BF16)
    o_ref[...] = _dot(s, w_ref[...].astype(BF16)) + b_ref[...]


def _mods(cond, w_ada, b_ada):
    depth, d, n = w_ada.shape
    tn = MODS_COLS
    return pl.pallas_call(
        _mods_kernel,
        grid=(depth, n // tn),
        in_specs=[
            pl.BlockSpec((MOD_ROWS, d), lambda l, j: (0, 0)),
            pl.BlockSpec((None, d, tn), lambda l, j: (l, 0, j)),
            pl.BlockSpec((None, 1, tn), lambda l, j: (l, 0, j)),
        ],
        out_specs=pl.BlockSpec((None, MOD_ROWS, tn), lambda l, j: (l, 0, j)),
        out_shape=jax.ShapeDtypeStruct((depth, MOD_ROWS, n), F32),
        compiler_params=_params("parallel", "parallel"),
        name="adaln_mods",
    )(cond, w_ada, b_ada.reshape(depth, 1, n))


def _wspec(w, block, index_fn, **kw):
    _, layer = w
    return pl.BlockSpec((None,) + tuple(block), lambda *idx: (layer,) + tuple(index_fn(*idx)), **kw)


def _mod_spec(which, row_fn):
    return pl.BlockSpec((None, None, 1, D_MODEL),
                        lambda i, *_: (row_fn(i), which, 0, 0))


def _interleave(pieces, chunks):
    if not pieces:
        for chunk in chunks:
            chunk()
        return
    slots = max(len(pieces) - 1, 1)
    per = -(-len(chunks) // slots)
    for n, piece in enumerate(pieces):
        piece()
        for chunk in chunks[n * per:(n + 1) * per]:
            chunk()


def _order_before_reads(ref, value):
    tile = (slice(0, BF16_SUBLANES), slice(0, LANES))
    bits = pltpu.bitcast(value[tile], jnp.uint32)
    zero = pltpu.bitcast((bits >> 16) >> 16, F32)
    ref[tile] = (ref[tile].astype(F32) + zero).astype(ref.dtype)


def _two_stage(i, n_tiles, run):
    @pl.when(i == 0)
    def _():
        run(0, None)

    for parity in range(2):
        @pl.when((i >= 1) & (i < n_tiles) & (i % 2 == parity))
        def _():
            run(parity, 1 - parity)

    @pl.when(i == n_tiles)
    def _():
        run(None, (n_tiles - 1) % 2)


def _inproj_kernel(x_ref, g_ref, sh_ref, sc_ref, w_ref, o_ref, h0_ref, h1_ref, *, n_tiles):
    h_refs = (h0_ref, h1_ref)
    tm = x_ref.shape[0]

    def run(norm_to, dot_from):
        pieces, chunks = [], []
        if dot_from is not None:
            for c0 in range(0, w_ref.shape[1], PROJ_COLS):
                def piece(cols=slice(c0, c0 + PROJ_COLS)):
                    o_ref[:, cols] = _dot(h_refs[dot_from][...], w_ref[:, cols]).astype(o_ref.dtype)
                pieces.append(piece)
        if norm_to is not None:
            for r0 in range(0, tm, NORM_ROWS):
                def chunk(rows=slice(r0, r0 + NORM_ROWS)):
                    val = _norm_mod(x_ref[rows, :], g_ref[...], sh_ref[...], sc_ref[...])
                    h_refs[norm_to][rows, :] = val.astype(BF16)
                    if dot_from is not None:
                        _order_before_reads(h_refs[dot_from], val)
                chunks.append(chunk)
        _interleave(pieces, chunks)

    _two_stage(pl.program_id(0), n_tiles, run)


def _inproj(x, g, mods, row_fn, w, tm):
    t, d = x.shape
    n = w[0].shape[2]
    n_tiles = t // tm
    cur = lambda i: jnp.minimum(i, n_tiles - 1)
    prev = lambda i: jnp.maximum(i - 1, 0)
    return pl.pallas_call(
        functools.partial(_inproj_kernel, n_tiles=n_tiles),
        grid=(n_tiles + 1,),
        in_specs=[
            pl.BlockSpec((tm, d), lambda i: (cur(i), 0)),
            pl.BlockSpec((1, d), lambda i: (0, 0)),
            _mod_spec(0, lambda i: row_fn(cur(i))),
            _mod_spec(1, lambda i: row_fn(cur(i))),
            _wspec(w, (d, n), lambda i: (0, 0), pipeline_mode=pl.Buffered(1)),
        ],
        out_specs=pl.BlockSpec((tm, n), lambda i: (prev(i), 0)),
        out_shape=jax.ShapeDtypeStruct((t, n), BF16),
        scratch_shapes=[pltpu.VMEM((tm, d), BF16)] * 2,
        compiler_params=_params("arbitrary"),
        name="norm_inproj",
    )(x, g.reshape(1, d), mods, mods, w[0])


def _softmax_pv(parts):
    m = functools.reduce(jnp.maximum, [s.max(axis=-1, keepdims=True) for s, _ in parts])
    num = 0.0
    den = 0.0
    for s, v in parts:
        p = jnp.exp(s - m)
        den = den + p.sum(axis=-1, keepdims=True)
        num = num + _dot(p.astype(BF16), v)
    return num / den


def _head_masks():
    lane = lax.broadcasted_iota(jnp.int32, (1, LANES), 1)
    first = lane < HEAD_DIM
    return first, [first, jnp.logical_not(first)]


def _attn_kernel(q_ref, k_ref, v_ref, kc_ref, vc_ref, slab_ref, o_ref, tab_ref, *, n_tiles, rows):
    first, sels = _head_masks()
    kc = kc_ref[...]
    vc = vc_ref[...]

    @pl.when(pl.program_id(1) == 0)
    def _():
        pattern = _attn_slab_pattern(rows)
        for p in range(pattern.shape[0]):
            for hh in range(HEAD_PAIR):
                for qr in range(Q_ROWS):
                    for kr in range(K_ROWS):
                        tab_ref[p, hh, qr * GRID_W:(qr + 1) * GRID_W, kr * GRID_W:(kr + 1) * GRID_W] = (
                            slab_ref[hh, int(pattern[p, qr, kr])])

    def scores(t, hh):
        q0 = pl.multiple_of(t * TQ, TQ)
        k0 = pl.multiple_of(jnp.clip(t * Q_ROWS - WIN_ROWS // 2, 0, rows - K_ROWS) * GRID_W, GRID_W)
        pat = jnp.where(t == 0, 0, jnp.where(t == n_tiles - 1, 2, 1))
        q = q_ref[pl.ds(q0, TQ), :] * ATTN_SCALE
        qm = jnp.where(sels[hh], q, jnp.zeros_like(q))
        s_loc = _dot_nt(qm, k_ref[pl.ds(k0, NK), :]) + tab_ref[pat, hh]
        s_ctx = _dot_nt(qm, kc)
        return k0, s_loc, s_ctx

    def tile_group(tg, carry):
        units = [(tg * ATTN_UNROLL + ti, hh) for ti in range(ATTN_UNROLL) for hh in range(HEAD_PAIR)]
        pending = None
        outs = {}
        for unit in units + [None]:
            nxt = scores(*unit) if unit is not None else None
            if pending is not None:
                (t, hh), (k0, s_loc, s_ctx) = pending
                outs[hh] = _softmax_pv([(s_loc, v_ref[pl.ds(k0, NK), :]), (s_ctx, vc)])
                if hh == HEAD_PAIR - 1:
                    q0 = pl.multiple_of(t * TQ, TQ)
                    o_ref[pl.ds(q0, TQ), :] = jnp.where(first, outs[0], outs[1]).astype(o_ref.dtype)
            pending = (unit, nxt)
        return carry

    lax.fori_loop(0, n_tiles // ATTN_UNROLL, tile_group, 0)


def _attn(proj, cproj, slabs, kc_blk, vc_blk):
    b, n, _ = proj.shape
    rows = n // GRID_W
    n_tiles = rows // Q_ROWS
    lat = lambda off: pl.BlockSpec((None, n, LANES), lambda p, i: (i, 0, off + p))
    cx = lambda off: pl.BlockSpec((None, CTX_LEN, LANES), lambda p, i: (i, 0, off + p))
    return pl.pallas_call(
        functools.partial(_attn_kernel, n_tiles=n_tiles, rows=rows),
        grid=(N_PAIRS, b),
        in_specs=[
            lat(0), lat(N_PAIRS), lat(2 * N_PAIRS), cx(kc_blk), cx(vc_blk),
            pl.BlockSpec((HEAD_PAIR,) + slabs.shape[1:], lambda p, i: (p, 0, 0, 0)),
        ],
        out_specs=pl.BlockSpec((None, n, LANES), lambda p, i: (i, 0, p)),
        out_shape=jax.ShapeDtypeStruct((b, n, NA_WIDTH), BF16),
        scratch_shapes=[pltpu.VMEM((3, HEAD_PAIR, TQ, NK), F32)],
        compiler_params=_params("arbitrary", "arbitrary"),
        name="nbr_attn",
    )(proj, proj, proj, cproj, cproj, slabs)


def _ctx_attn_kernel(q_ref, k_ref, v_ref, o_ref):
    first, sels = _head_masks()
    q = q_ref[...] * ATTN_SCALE
    k = k_ref[...]
    v = v_ref[...]
    outs = []
    for hh in range(HEAD_PAIR):
        qm = jnp.where(sels[hh], q, jnp.zeros_like(q))
        outs.append(_softmax_pv([(_dot_nt(qm, k), v)]))
    o_ref[...] = jnp.where(first, outs[0], outs[1]).astype(o_ref.dtype)


def _ctx_attn(cproj):
    b, n, _ = cproj.shape
    spec = lambda off: pl.BlockSpec((None, n, LANES), lambda i, p: (i, 0, off + p))
    return pl.pallas_call(
        _ctx_attn_kernel,
        grid=(b, N_PAIRS),
        in_specs=[spec(0), spec(N_PAIRS), spec(2 * N_PAIRS)],
        out_specs=spec(0),
        out_shape=jax.ShapeDtypeStruct((b, n, NA_WIDTH), BF16),
        compiler_params=_params("parallel", "parallel"),
        name="ctx_attn",
    )(cproj, cproj, cproj)


MASKED_SLAB = 2 * WIN_ROWS - 1


def _attn_bias_slabs(rpb):
    qc = np.arange(GRID_W)[:, None]
    kc = np.arange(GRID_W)[None, :]
    cs = np.clip(qc - WIN_COLS // 2, 0, GRID_W - WIN_COLS)
    col_ok = (kc >= cs) & (kc < cs + WIN_COLS)
    dc = kc - qc + WIN_COLS - 1
    onehot = np.zeros((2 * WIN_COLS - 1, GRID_W, GRID_W), np.float32)
    qi, ki = np.nonzero(col_ok)
    onehot[dc[qi, ki], qi, ki] = 1.0
    t1 = jnp.einsum('hrd,dqk->hrqk', rpb, jnp.asarray(onehot), precision=lax.Precision.HIGHEST)
    t1 = jnp.where(jnp.asarray(col_ok)[None, None], t1, NEG)
    neg_slab = jnp.full((NA_HEADS, 1, GRID_W, GRID_W), NEG, F32)
    return jnp.concatenate([t1, neg_slab], axis=1)


def _attn_slab_pattern(rows):
    n_tiles = rows // Q_ROWS
    idx = np.full((3, Q_ROWS, K_ROWS), MASKED_SLAB, np.int32)
    for p, t in enumerate((0, 1, n_tiles - 1)):
        k_start = int(np.clip(t * Q_ROWS - WIN_ROWS // 2, 0, rows - K_ROWS))
        for qr in range(Q_ROWS):
            r = t * Q_ROWS + qr
            rs = int(np.clip(r - WIN_ROWS // 2, 0, rows - WIN_ROWS))
            for kr in range(K_ROWS):
                krow = k_start + kr
                if rs <= krow < rs + WIN_ROWS:
                    idx[p, qr, kr] = krow - r + WIN_ROWS - 1
    return idx


def _split_bf16(a):
    hi = a.astype(BF16)
    lo = (a - hi.astype(F32)).astype(BF16)
    return hi, lo


def _dot3(a, b):
    ah, al = _split_bf16(a)
    bh, bl = _split_bf16(b)
    return _dot(ah, bh) + (_dot(ah, bl) + _dot(al, bh))


def _four_w_kernel(cc_ref, sc_ref, w_ref, o_ref):
    w = w_ref[...]
    o_ref[:, :FOURIER_GROUP_DIM] = _dot3(cc_ref[...], w).astype(BF16)
    o_ref[:, FOURIER_GROUP_DIM:] = _dot3(sc_ref[...], w).astype(BF16)


def _four_w(cc, sc, w_four):
    depth, g, c, _ = w_four.shape
    tab = pl.BlockSpec((c, c), lambda l, i: (0, 0))
    return pl.pallas_call(
        _four_w_kernel,
        grid=(depth, g),
        in_specs=[tab, tab, pl.BlockSpec((None, None, c, c), lambda l, i: (l, i, 0, 0))],
        out_specs=pl.BlockSpec((None, None, c, 2 * c), lambda l, i: (l, i, 0, 0)),
        out_shape=jax.ShapeDtypeStruct((depth, g, c, 2 * c), BF16),
        compiler_params=_params("parallel", "parallel"),
        name="fourier_weights",
    )(cc, sc, w_four)


def _four_chan_kernel(f_ref, m_ref, *refs):
    out_refs, (yc_scr, ys_scr) = refs[:2 * DFT_RADIX], refs[2 * DFT_RADIX:]
    c = FOURIER_GROUP_DIM
    part_rows = f_ref.shape[0] // DFT_RADIX
    for g in range(FOURIER_GROUPS):
        cols = slice(g * c, (g + 1) * c)
        y = _dot(f_ref[:, cols], m_ref[g])
        yc_scr[...] = y[:, :c]
        ys_scr[...] = y[:, c:]
        for p in range(DFT_RADIX):
            out_refs[2 * p][:, cols] = yc_scr[pl.ds(p, part_rows, stride=DFT_RADIX), :].astype(BF16)
            out_refs[2 * p + 1][:, cols] = ys_scr[pl.ds(p, part_rows, stride=DFT_RADIX), :].astype(BF16)


def _four_chan(proj, mw, tm):
    t = proj.shape[0]
    f_blk = proj.shape[1] // FOURIER_WIDTH - 1
    out = pl.BlockSpec((tm // DFT_RADIX, FOURIER_WIDTH), lambda i: (i, 0))
    return pl.pallas_call(
        _four_chan_kernel,
        grid=(t // tm,),
        in_specs=[
            pl.BlockSpec((tm, FOURIER_WIDTH), lambda i: (i, f_blk)),
            pl.BlockSpec(mw.shape, lambda i: (0, 0, 0)),
        ],
        out_specs=[out] * (2 * DFT_RADIX),
        out_shape=[jax.ShapeDtypeStruct((t // DFT_RADIX, FOURIER_WIDTH), BF16)] * (2 * DFT_RADIX),
        scratch_shapes=[pltpu.VMEM((tm, FOURIER_GROUP_DIM), F32)] * 2,
        compiler_params=_params("parallel"),
        name="fourier_chan",
    )(proj, mw)


def _four_pos_kernel(c0, s0, c1, s1, c2, s2, c3, s3, ns1, nc3, a0, b0, a1, b1, a2, b2, a3, b3, o_ref):
    step = min(DFT_ROWS, c0.shape[0])
    for r0 in range(0, c0.shape[0], step):
        rows = slice(r0, r0 + step)
        dot = lambda table, data: _dot(table[rows, :], data[...])
        re0 = dot(c0, a0) + dot(s0, b0)
        re2 = dot(c2, a2) + dot(s2, b2)
        re13 = dot(c1, a1) + dot(s1, b1) + dot(c3, a3) + dot(s3, b3)
        im13 = dot(c1, b1) + dot(ns1, a1) + dot(nc3, b3) + dot(s3, a3)
        plus, minus = re0 + re2, re0 - re2
        o_ref[0, rows, :] = (plus + re13).astype(o_ref.dtype)
        o_ref[1, rows, :] = (minus + im13).astype(o_ref.dtype)
        o_ref[2, rows, :] = (plus - re13).astype(o_ref.dtype)
        o_ref[3, rows, :] = (minus - im13).astype(o_ref.dtype)


def _four_pos(tables, y_parts, b, tm, tn):
    data = [y.reshape(b, y.shape[0] // b, y.shape[1]) for y in y_parts]
    _, part, w = data[0].shape
    mat = pl.BlockSpec((tm, part), lambda i, bb, j: (i, 0), pipeline_mode=pl.Buffered(1))
    dat = pl.BlockSpec((None, part, tn), lambda i, bb, j: (bb, 0, j))
    out = pl.pallas_call(
        _four_pos_kernel,
        grid=(part // tm, b, w // tn),
        in_specs=[mat] * len(tables) + [dat] * len(data),
        out_specs=pl.BlockSpec((None, DFT_RADIX, tm, tn), lambda i, bb, j: (bb, 0, i, j)),
        out_shape=jax.ShapeDtypeStruct((b, DFT_RADIX, part, w), BF16),
        compiler_params=_params("parallel", "parallel", "parallel"),
        name="fourier_pos",
    )(*tables, *data)
    return out.reshape(b, DFT_RADIX * part, w)


def _dft_tables(n):
    part = n // DFT_RADIX
    split = 2 ** (part.bit_length() // 2)
    j = jnp.arange(part, dtype=jnp.int32)[:, None]
    hi = jnp.arange(part // split, dtype=jnp.int32)[None, :] * split
    lo = jnp.arange(split, dtype=jnp.int32)[None, :]
    ang = lambda k: (2.0 * np.pi / part) * ((j * k) % part).astype(F32)
    ch, sh, cl, sl = jnp.cos(ang(hi)), jnp.sin(ang(hi)), jnp.cos(ang(lo)), jnp.sin(ang(lo))
    c0 = (ch[:, :, None] * cl[:, None, :] - sh[:, :, None] * sl[:, None, :]).reshape(part, part)
    s0 = (sh[:, :, None] * cl[:, None, :] + ch[:, :, None] * sl[:, None, :]).reshape(part, part)
    tables = [c0, s0]
    for p in range(1, DFT_RADIX):
        phi = (2.0 * np.pi * p / n) * j.astype(F32)
        cp, sp = jnp.cos(phi), jnp.sin(phi)
        tables += [c0 * cp - s0 * sp, s0 * cp + c0 * sp]
    tables += [-tables[3], -tables[6]]
    return tuple(t.astype(BF16) for t in tables)


def _outproj_kernel(a_ref, f_ref, w_ref, x_ref, gpost_ref, gate_ref, gpre_ref, sh_ref, sc_ref,
                    xo_ref, h_ref, m0_ref, m1_ref, lhs_ref, *, n_tiles):
    mix_refs = (m0_ref, m1_ref)
    tm = x_ref.shape[0]

    def run(mix_to, epi_from):
        pieces, chunks = [], []
        if mix_to is not None:
            lhs_ref[:, :NA_WIDTH] = a_ref[...]
            lhs_ref[:, NA_WIDTH:] = f_ref[...]
            for c0 in range(0, w_ref.shape[1], OUT_COLS):
                def piece(cols=slice(c0, c0 + OUT_COLS)):
                    mix_refs[mix_to][:, cols] = _dot(lhs_ref[...], w_ref[:, cols])
                pieces.append(piece)
        if epi_from is not None:
            for r0 in range(0, tm, OUT_ROWS):
                def chunk(rows=slice(r0, r0 + OUT_ROWS)):
                    x = x_ref[rows, :] + gate_ref[...] * _rms(mix_refs[epi_from][rows, :], gpost_ref[...])
                    xo_ref[rows, :] = x
                    h = _norm_mod(x, gpre_ref[...], sh_ref[...], sc_ref[...])
                    h_ref[rows, :] = h.astype(BF16)
                    if mix_to is not None:
                        _order_before_reads(lhs_ref, h)
                chunks.append(chunk)
        _interleave(pieces, chunks)

    _two_stage(pl.program_id(0), n_tiles, run)


def _outproj(attn, four, w, x, g_post, g_pre, mods, row_fn, tm):
    t, d = x.shape
    n_tiles = t // tm
    cur = lambda i: jnp.minimum(i, n_tiles - 1)
    prev = lambda i: jnp.maximum(i - 1, 0)
    vec = pl.BlockSpec((1, d), lambda i: (0, 0))
    row = pl.BlockSpec((tm, d), lambda i: (prev(i), 0))
    half = pl.BlockSpec((tm, NA_WIDTH), lambda i: (cur(i), 0))
    mod = lambda which: _mod_spec(which, lambda i: row_fn(prev(i)))
    return pl.pallas_call(
        functools.partial(_outproj_kernel, n_tiles=n_tiles),
        grid=(n_tiles + 1,),
        in_specs=[half, half, _wspec(w, w[0].shape[1:], lambda i: (0, 0), pipeline_mode=pl.Buffered(1)), row, vec,
                  mod(2), vec, mod(3), mod(4)],
        out_specs=[row, row],
        out_shape=[jax.ShapeDtypeStruct((t, d), F32), jax.ShapeDtypeStruct((t, d), BF16)],
        scratch_shapes=[pltpu.VMEM((tm, d), F32), pltpu.VMEM((tm, d), F32),
                        pltpu.VMEM((tm, NA_WIDTH + FOURIER_WIDTH), BF16)],
        compiler_params=_params("arbitrary"),
        name="outproj_residual",
    )(attn, four, w[0], x, g_post.reshape(1, d), mods, g_pre.reshape(1, d), mods, mods)


def _matmul_kernel(x_ref, w_ref, o_ref):
    o_ref[...] = _dot(x_ref[...], w_ref[...]).astype(o_ref.dtype)


def _matmul(x, w, tn, out_dtype, name):
    m, k = x.shape
    n = w[0].shape[2]
    return pl.pallas_call(
        _matmul_kernel,
        grid=(n // tn,),
        in_specs=[pl.BlockSpec((m, k), lambda j: (0, 0)), _wspec(w, (k, tn), lambda j: (0, j))],
        out_specs=pl.BlockSpec((m, tn), lambda j: (0, j)),
        out_shape=jax.ShapeDtypeStruct((m, n), out_dtype),
        compiler_params=_params("parallel"),
        name=name,
    )(x, w[0])


def _ffn_kernel(h_ref, wa_ref, wg_ref, edge_ref, cp_ref, wd_ref, x_ref, gate_ref, gpost_ref, o_ref,
                acc_ref, ua0, ug0, ua1, ug1, act0, act1,
                *, tm, n_ff_tiles, n_row_tiles, seq_len):
    i = pl.program_id(0)
    j = pl.program_id(1)
    nj = n_ff_tiles
    pad = F32_SUBLANES
    tf = wa_ref.shape[1]
    u_refs = ((ua0, ug0), (ua1, ug1))
    act_refs = (act0, act1)

    def up_proj_part(half, k0):
        w_ref = (wa_ref, wg_ref)[half]
        return _dot(h_ref[:, k0:k0 + UP_K], w_ref[k0:k0 + UP_K, :])

    def up_proj_store(parity, half, u):
        u_ref = u_refs[parity][half]
        u_ref[0:pad, :] = jnp.broadcast_to(edge_ref[half:half + 1, :], (pad, tf))
        u_ref[pad:pad + tm, :] = u
        u_ref[pad + tm:, :] = jnp.broadcast_to(edge_ref[2 + half:3 + half, :], (pad, tf))

    def conv(u_ref, r0, cols, p0):
        win = u_ref[r0:r0 + CONV_ROWS + 2 * pad, cols]
        mid = win[pad:pad + CONV_ROWS]
        down = pltpu.roll(win, 1, axis=0)[pad:pad + CONV_ROWS]
        up = pltpu.roll(win, CONV_ROWS + 2 * pad - 1, axis=0)[pad:pad + CONV_ROWS]
        row = lax.broadcasted_iota(jnp.int32, (CONV_ROWS, 1), 0)
        if r0 > 0 and r0 % seq_len == 0:
            down = jnp.where(row == 0, 0.0, down)
        if r0 + CONV_ROWS < tm and (r0 + CONV_ROWS) % seq_len == 0:
            up = jnp.where(row == CONV_ROWS - 1, 0.0, up)
        return (down * cp_ref[p0:p0 + 1, cols] + mid * cp_ref[p0 + 1:p0 + 2, cols]
                + up * cp_ref[p0 + 2:p0 + 3, cols] + cp_ref[p0 + 3:p0 + 4, cols])

    def conv_gate(parity, r0, c0):
        ua_ref, ug_ref = u_refs[parity]
        cols = slice(c0, c0 + LANES)
        a = conv(ua_ref, r0, cols, 0)
        g = conv(ug_ref, r0, cols, CONV_TAPS + 1)
        act_refs[parity][r0:r0 + CONV_ROWS, cols] = (g * jax.nn.sigmoid(g) * a).astype(BF16)

    def down_proj(parity, cols):
        return _dot(act_refs[parity][...], wd_ref[:, cols])

    chunks = [(r0, c0) for r0 in range(0, tm, CONV_ROWS) for c0 in range(0, tf, LANES)]
    d_out = wd_ref.shape[1]
    down_groups = [slice(c, c + DOWN_COLS) for c in range(0, d_out, DOWN_COLS)]

    def finish_rows(r0):
        rows = slice(r0, r0 + FINISH_ROWS)
        o_ref[rows, :] = x_ref[rows, :] + gate_ref[...] * _rms(acc_ref[rows, :], gpost_ref[...])
        acc_ref[rows, :] = jnp.zeros((FINISH_ROWS, d_out), F32)

    def run_step(up=None, conv_of=None, down=None, finish=False):
        pieces = []
        if down is not None:
            for cols in down_groups:
                def piece(cols=cols):
                    acc_ref[:, cols] += down_proj(down, cols)
                pieces.append((piece, tf * DOWN_COLS))
        n_down = len(pieces)
        if up is not None:
            d_in = h_ref.shape[1]
            for half in range(2):
                partial_u = []
                for k0 in range(0, d_in, UP_K):
                    def piece(half=half, k0=k0, partial_u=partial_u):
                        part = up_proj_part(half, k0)
                        partial_u[:] = [part if not partial_u else partial_u[0] + part]
                        if k0 + UP_K == d_in:
                            up_proj_store(up, half, partial_u[0])
                    pieces.append((piece, UP_K * tf))
        conv_todo = [functools.partial(conv_gate, conv_of, *c) for c in chunks] if conv_of is not None else []
        fin_todo = [functools.partial(finish_rows, r0) for r0 in range(0, tm, FINISH_ROWS)] if finish else []
        n_conv, n_fin = len(conv_todo), len(fin_todo)
        total = sum(w for _, w in pieces)
        total_up = sum(w for _, w in pieces[n_down:])
        done = done_up = 0.0
        for n, (piece, w) in enumerate(pieces):
            piece()
            done += w
            is_last = n == len(pieces) - 1
            conv_upto = n_conv if is_last else round(n_conv * done / total)
            while n_conv - len(conv_todo) < conv_upto:
                conv_todo.pop(0)()
            if n >= n_down:
                done_up += w
                fin_upto = n_fin if is_last else round(n_fin * done_up / total_up)
                while n_fin - len(fin_todo) < fin_upto:
                    fin_todo.pop(0)()
        for chunk in conv_todo + fin_todo:
            chunk()

    s = i * nj + j
    last = n_row_tiles * nj
    finishing = (j == 1) & (i >= 1)

    @pl.when(s == 0)
    def _():
        acc_ref[...] = jnp.zeros_like(acc_ref)
        run_step(up=0)

    @pl.when(s == 1)
    def _():
        run_step(up=1, conv_of=0)

    for parity in range(2):
        steady = (s >= 2) & (s < last) & (s % 2 == parity)

        @pl.when(steady & jnp.logical_not(finishing))
        def _():
            run_step(up=parity, conv_of=1 - parity, down=parity)

        @pl.when(steady & finishing)
        def _():
            run_step(up=parity, conv_of=1 - parity, down=parity, finish=True)

    @pl.when(s == last)
    def _():
        run_step(conv_of=(last - 1) % 2, down=last % 2)

    @pl.when(s == last + 1)
    def _():
        run_step(down=(last + 1) % 2, finish=True)


def _ffn(h, w_up, conv_w, conv_b, w_down, x, g_post, mods, row_fn, seq_len, tm):
    t, d = x.shape
    ff = w_down[0].shape[1]
    tf = FF_TILE
    nj = ff // tf
    n_tiles = t // tm
    assert seq_len % tm == 0 or (tm % seq_len == 0 and seq_len % CONV_ROWS == 0)

    tile0 = np.arange(n_tiles) * tm
    start = tile0 % seq_len == 0
    end = (tile0 + tm) % seq_len == 0
    prev = jnp.take(h, jnp.asarray(np.maximum(tile0 - 1, 0)), axis=0)
    nxt = jnp.take(h, jnp.asarray(np.minimum(tile0 + tm, t - 1)), axis=0)
    prev = jnp.where(jnp.asarray(start)[:, None], jnp.zeros_like(prev), prev)
    nxt = jnp.where(jnp.asarray(end)[:, None], jnp.zeros_like(nxt), nxt)
    edge_rows = jnp.concatenate([prev, nxt], axis=0)
    pad = -edge_rows.shape[0] % BF16_SUBLANES
    edge_rows = jnp.pad(edge_rows, ((0, pad), (0, 0)))
    u_edge = _matmul(edge_rows, w_up, 2 * ff // EDGE_STEPS, F32, "conv_ffn_edges")
    u_prev, u_next = u_edge[:n_tiles], u_edge[n_tiles:2 * n_tiles]
    edges = jnp.stack([u_prev[:, :ff], u_prev[:, ff:], u_next[:, :ff], u_next[:, ff:]], axis=1)
    edges = edges.reshape(n_tiles, 4, nj, tf).transpose(0, 2, 1, 3)
    conv_p = jnp.concatenate([conv_w[:, :ff], conv_b[None, :ff], conv_w[:, ff:], conv_b[None, ff:]], axis=0)
    conv_p = conv_p.reshape(2 * (CONV_TAPS + 1), nj, tf).transpose(1, 0, 2)

    assert nj >= 3
    vec = pl.BlockSpec((1, d), lambda i, j: (0, 0))
    cur = lambda i: jnp.minimum(i, n_tiles - 1)
    fin = lambda i, j, switch: jnp.where(j <= switch, jnp.maximum(i - 1, 0), cur(i))
    lag = lambda j, k: jnp.where(j >= k, j - k, j + nj - k)
    up_cols = lambda off: _wspec(w_up, (d, tf), lambda i, j: (0, off + j))
    u_buf = pltpu.VMEM((tm + 2 * F32_SUBLANES, tf), F32)
    act_buf = pltpu.VMEM((tm, tf), BF16)
    return pl.pallas_call(
        functools.partial(_ffn_kernel, tm=tm, n_ff_tiles=nj, n_row_tiles=n_tiles, seq_len=seq_len),
        grid=(n_tiles + 1, nj),
        in_specs=[
            pl.BlockSpec((tm, d), lambda i, j: (cur(i), 0)),
            up_cols(0), up_cols(nj),
            pl.BlockSpec((None, None, 4, tf), lambda i, j: (cur(i), j, 0, 0)),
            pl.BlockSpec((None, 2 * (CONV_TAPS + 1), tf), lambda i, j: (lag(j, 1), 0, 0)),
            _wspec(w_down, (tf, d), lambda i, j: (lag(j, 2), 0)),
            pl.BlockSpec((tm, d), lambda i, j: (fin(i, j, 1), 0)),
            pl.BlockSpec((None, None, 1, d), lambda i, j: (row_fn(fin(i, j, 1)), 5, 0, 0)),
            vec,
        ],
        out_specs=pl.BlockSpec((tm, d), lambda i, j: (fin(i, j, 3), 0)),
        out_shape=jax.ShapeDtypeStruct((t, d), F32),
        scratch_shapes=[pltpu.VMEM((tm, d), F32), u_buf, u_buf, u_buf, u_buf, act_buf, act_buf],
        compiler_params=_params("arbitrary", "arbitrary"),
        name="conv_ffn",
    )(h, w_up[0], w_up[0], edges, conv_p, w_down[0], x, mods,
      g_post.reshape(1, d))


def kernel(x, c, ctx, c_ctx, w_ada, b_ada, g_pre_mix, w_in, rpb, w_four, w_out, g_post_mix,
           g_pre_ffn, w_up, conv_w, conv_b, w_down, g_post_ffn):
    b, n, d = x.shape
    n_ctx = ctx.shape[1]
    assert (d, n, n_ctx, b) == (D_MODEL, GRID_W * GRID_W, CTX_LEN, CTX_MOD_ROW)
    t_lat, t_ctx = b * n, b * n_ctx
    tm_lat = tm_ctx = ROW_TILE
    tiles_per_seq = n // tm_lat
    assert tiles_per_seq & (tiles_per_seq - 1) == 0
    lat_row = lambda i: i >> (tiles_per_seq.bit_length() - 1)
    ctx_row = lambda i: CTX_MOD_ROW

    cond = jnp.zeros((MOD_ROWS, d), F32).at[:b].set(c).at[CTX_MOD_ROW].set(c_ctx)
    mods_all = _mods(cond, w_ada, b_ada)

    scale = (n * FOURIER_GROUP_DIM) ** -0.5
    scale_ctx = (n_ctx * FOURIER_GROUP_DIM) ** -0.5
    kk = np.outer(np.arange(FOURIER_GROUP_DIM), np.arange(FOURIER_GROUP_DIM)) % FOURIER_GROUP_DIM
    ang = 2.0 * np.pi * kk / FOURIER_GROUP_DIM
    cc, sc = np.cos(ang), -np.sin(ang)
    mw_lat = _four_w(jnp.asarray(cc * scale, F32), jnp.asarray(sc * scale, F32), w_four)
    dft_lat = _dft_tables(n)

    w_in_b, w_out_b, w_up_b, w_down_b = (w.astype(BF16) for w in (w_in, w_out, w_up, w_down))
    xt = x.reshape(t_lat, d)
    ct = ctx.reshape(t_ctx, d)
    for l in range(DEPTH):
        last = l == DEPTH - 1
        mods = mods_all[l].reshape(MOD_ROWS, N_MOD, 1, d)
        w_in_l, w_out_l, w_up_l, w_down_l = ((w, l) for w in (w_in_b, w_out_b, w_up_b, w_down_b))

        proj = _inproj(xt, g_pre_mix[l], mods, lat_row, w_in_l, tm_lat)
        if last:
            w_kv = (w_in_b[l:l + 1, :, NA_WIDTH:3 * NA_WIDTH], 0)
            cproj = _inproj(ct, g_pre_mix[l], mods, ctx_row, w_kv, tm_ctx)
            kc_blk, vc_blk = 0, N_PAIRS
        else:
            cproj = _inproj(ct, g_pre_mix[l], mods, ctx_row, w_in_l, tm_ctx)
            kc_blk, vc_blk = N_PAIRS, 2 * N_PAIRS
        cproj3 = cproj.reshape(b, n_ctx, cproj.shape[1])

        attn = _attn(proj.reshape(b, n, PROJ_WIDTH), cproj3, _attn_bias_slabs(rpb[l]), kc_blk, vc_blk)
        four = _four_pos(dft_lat, _four_chan(proj, mw_lat[l], CHAN_TILE), b, n // DFT_RADIX, DFT_COLS)
        xt, h2 = _outproj(attn.reshape(t_lat, NA_WIDTH), four.reshape(t_lat, FOURIER_WIDTH), w_out_l, xt,
                          g_post_mix[l], g_pre_ffn[l], mods, lat_row, tm_lat)
        xt = _ffn(h2, w_up_l, conv_w[l], conv_b[l], w_down_l, xt, g_post_ffn[l], mods, lat_row, n, tm_lat)

        if not last:
            mw_ctx = _four_w(jnp.asarray(cc * scale_ctx, F32), jnp.asarray(sc * scale_ctx, F32),
                             w_four[l:l + 1])[0]
            dft_ctx = _dft_tables(n_ctx)
            attn_c = _ctx_attn(cproj3)
            four_c = _four_pos(dft_ctx, _four_chan(cproj, mw_ctx, ROW_TILE), b, n_ctx // DFT_RADIX, DFT_COLS)
            ct, hc2 = _outproj(attn_c.reshape(t_ctx, NA_WIDTH), four_c.reshape(t_ctx, FOURIER_WIDTH),
                               w_out_l, ct, g_post_mix[l], g_pre_ffn[l], mods, ctx_row, tm_ctx)
            ct = _ffn(hc2, w_up_l, conv_w[l], conv_b[l], w_down_l, ct, g_post_ffn[l], mods, ctx_row,
                      n_ctx, tm_ctx)
    return xt.reshape(b, n, d)
```

```python
import functools

import jax
import jax.numpy as jnp
import numpy as np
from jax import lax
from jax.experimental import pallas as pl
from jax.experimental.pallas import tpu as pltpu

D_MODEL = 2048
DEPTH = 2
GRID_W = 64
CTX_LEN = 256
NA_HEADS = 16
HEAD_DIM = 64
NA_WIDTH = NA_HEADS * HEAD_DIM
FOURIER_GROUPS = 8
FOURIER_GROUP_DIM = 128
FOURIER_WIDTH = FOURIER_GROUPS * FOURIER_GROUP_DIM
PROJ_WIDTH = 3 * NA_WIDTH + FOURIER_WIDTH
WIN_ROWS = 8
WIN_COLS = 16
D_FF = 5632
N_MOD = 6
EPS = 1e-6
ATTN_SCALE = HEAD_DIM ** -0.5
MOD_ROWS = 16
CTX_MOD_ROW = 8

ROW_TILE = 512
CHAN_TILE = 1024
DFT_COLS = 512
MODS_COLS = 1024
EDGE_STEPS = 8
LANES = 128
DFT_RADIX = 4
DFT_ROWS = 512
FF_TILE = 512
CONV_TAPS = 3
FINISH_ROWS = 64
UP_K = 256
DOWN_COLS = 512
OUT_COLS = 512
OUT_ROWS = 64
PROJ_COLS = 512
NORM_ROWS = 16
CONV_ROWS = 64
F32_SUBLANES = 8
BF16_SUBLANES = 16
HEAD_PAIR = LANES // HEAD_DIM
N_PAIRS = NA_HEADS // HEAD_PAIR
Q_ROWS = 4
K_ROWS = Q_ROWS + WIN_ROWS - 1
TQ = Q_ROWS * GRID_W
NK = K_ROWS * GRID_W
ATTN_UNROLL = 8
NEG = -1e30
VMEM_LIMIT = 56 * 1024 * 1024

F32 = jnp.float32
BF16 = jnp.bfloat16


def _params(*sem, flags=None):
    return pltpu.CompilerParams(dimension_semantics=sem, vmem_limit_bytes=VMEM_LIMIT, flags=flags)


def _dot(a, b):
    return jnp.dot(a, b, preferred_element_type=F32)


def _dot_nt(a, b):
    return lax.dot_general(a, b, (((1,), (1,)), ((), ())), preferred_element_type=F32)


def _rms(x, g):
    ms = jnp.mean(x * x, axis=-1, keepdims=True)
    return x * lax.rsqrt(ms + EPS) * g


def _norm_mod(x, g, shift, scale):
    return _rms(x, g) * (1.0 + scale) + shift


def _mods_kernel(cond_ref, w_ref, b_ref, o_ref):
    c = cond_ref[...]
    s = (c * jax.nn.sigmoid(c)).astype(BF16)
    o_ref[...] = _dot(s, w_ref[...].astype(BF16)) + b_ref[...]


def _mods(cond, w_ada, b_ada):
    depth, d, n = w_ada.shape
    tn = MODS_COLS
    return pl.pallas_call(
        _mods_kernel,
        grid=(depth, n // tn),
        in_specs=[
            pl.BlockSpec((MOD_ROWS, d), lambda l, j: (0, 0)),
            pl.BlockSpec((None, d, tn), lambda l, j: (l, 0, j)),
            pl.BlockSpec((None, 1, tn), lambda l, j: (l, 0, j)),
        ],
        out_specs=pl.BlockSpec((None, MOD_ROWS, tn), lambda l, j: (l, 0, j)),
        out_shape=jax.ShapeDtypeStruct((depth, MOD_ROWS, n), F32),
        compiler_params=_params("parallel", "parallel"),
        name="adaln_mods",
    )(cond, w_ada, b_ada.reshape(depth, 1, n))


def _wspec(w, block, index_fn, **kw):
    _, layer = w
    return pl.BlockSpec((None,) + tuple(block), lambda *idx: (layer,) + tuple(index_fn(*idx)), **kw)


def _mod_spec(which, row_fn):
    return pl.BlockSpec((None, None, 1, D_MODEL),
                        lambda i, *_: (row_fn(i), which, 0, 0))


def _interleave(pieces, chunks):
    if not pieces:
        for chunk in chunks:
            chunk()
        return
    slots = max(len(pieces) - 1, 1)
    per = -(-len(chunks) // slots)
    for n, piece in enumerate(pieces):
        piece()
        for chunk in chunks[n * per:(n + 1) * per]:
            chunk()


def _order_before_reads(ref, value):
    tile = (slice(0, BF16_SUBLANES), slice(0, LANES))
    bits = pltpu.bitcast(value[tile], jnp.uint32)
    zero = pltpu.bitcast((bits >> 16) >> 16, F32)
    ref[tile] = (ref[tile].astype(F32) + zero).astype(ref.dtype)


def _two_stage(i, n_tiles, run):
    @pl.when(i == 0)
    def _():
        run(0, None)

    for parity in range(2):
        @pl.when((i >= 1) & (i < n_tiles) & (i % 2 == parity))
        def _():
            run(parity, 1 - parity)

    @pl.when(i == n_tiles)
    def _():
        run(None, (n_tiles - 1) % 2)


def _inproj_kernel(x_ref, g_ref, sh_ref, sc_ref, w_ref, o_ref, h0_ref, h1_ref, *, n_tiles):
    h_refs = (h0_ref, h1_ref)
    tm = x_ref.shape[0]

    def run(norm_to, dot_from):
        pieces, chunks = [], []
        if dot_from is not None:
            for c0 in range(0, w_ref.shape[1], PROJ_COLS):
                def piece(cols=slice(c0, c0 + PROJ_COLS)):
                    o_ref[:, cols] = _dot(h_refs[dot_from][...], w_ref[:, cols]).astype(o_ref.dtype)
                pieces.append(piece)
        if norm_to is not None:
            for r0 in range(0, tm, NORM_ROWS):
                def chunk(rows=slice(r0, r0 + NORM_ROWS)):
                    val = _norm_mod(x_ref[rows, :], g_ref[...], sh_ref[...], sc_ref[...])
                    h_refs[norm_to][rows, :] = val.astype(BF16)
                    if dot_from is not None:
                        _order_before_reads(h_refs[dot_from], val)
                chunks.append(chunk)
        _interleave(pieces, chunks)

    _two_stage(pl.program_id(0), n_tiles, run)


def _inproj(x, g, mods, row_fn, w, tm):
    t, d = x.shape
    n = w[0].shape[2]
    n_tiles = t // tm
    cur = lambda i: jnp.minimum(i, n_tiles - 1)
    prev = lambda i: jnp.maximum(i - 1, 0)
    return pl.pallas_call(
        functools.partial(_inproj_kernel, n_tiles=n_tiles),
        grid=(n_tiles + 1,),
        in_specs=[
            pl.BlockSpec((tm, d), lambda i: (cur(i), 0)),
            pl.BlockSpec((1, d), lambda i: (0, 0)),
            _mod_spec(0, lambda i: row_fn(cur(i))),
            _mod_spec(1, lambda i: row_fn(cur(i))),
            _wspec(w, (d, n), lambda i: (0, 0), pipeline_mode=pl.Buffered(1)),
        ],
        out_specs=pl.BlockSpec((tm, n), lambda i: (prev(i), 0)),
        out_shape=jax.ShapeDtypeStruct((t, n), BF16),
        scratch_shapes=[pltpu.VMEM((tm, d), BF16)] * 2,
        compiler_params=_params("arbitrary"),
        name="norm_inproj",
    )(x, g.reshape(1, d), mods, mods, w[0])


def _softmax_pv(parts):
    m = functools.reduce(jnp.maximum, [s.max(axis=-1, keepdims=True) for s, _ in parts])
    num = 0.0
    den = 0.0
    for s, v in parts:
        p = jnp.exp(s - m)
        den = den + p.sum(axis=-1, keepdims=True)
        num = num + _dot(p.astype(BF16), v)
    return num / den


def _head_masks():
    lane = lax.broadcasted_iota(jnp.int32, (1, LANES), 1)
    first = lane < HEAD_DIM
    return first, [first, jnp.logical_not(first)]


def _attn_kernel(q_ref, k_ref, v_ref, kc_ref, vc_ref, slab_ref, o_ref, tab_ref, *, n_tiles, rows):
    first, sels = _head_masks()
    kc = kc_ref[...]
    vc = vc_ref[...]

    @pl.when(pl.program_id(1) == 0)
    def _():
        pattern = _attn_slab_pattern(rows)
        for p in range(pattern.shape[0]):
            for hh in range(HEAD_PAIR):
                for qr in range(Q_ROWS):
                    for kr in range(K_ROWS):
                        tab_ref[p, hh, qr * GRID_W:(qr + 1) * GRID_W, kr * GRID_W:(kr + 1) * GRID_W] = (
                            slab_ref[hh, int(pattern[p, qr, kr])])

    def scores(t, hh):
        q0 = pl.multiple_of(t * TQ, TQ)
        k0 = pl.multiple_of(jnp.clip(t * Q_ROWS - WIN_ROWS // 2, 0, rows - K_ROWS) * GRID_W, GRID_W)
        pat = jnp.where(t == 0, 0, jnp.where(t == n_tiles - 1, 2, 1))
        q = q_ref[pl.ds(q0, TQ), :] * ATTN_SCALE
        qm = jnp.where(sels[hh], q, jnp.zeros_like(q))
        s_loc = _dot_nt(qm, k_ref[pl.ds(k0, NK), :]) + tab_ref[pat, hh]
        s_ctx = _dot_nt(qm, kc)
        return k0, s_loc, s_ctx

    def tile_group(tg, carry):
        units = [(tg * ATTN_UNROLL + ti, hh) for ti in range(ATTN_UNROLL) for hh in range(HEAD_PAIR)]
        pending = None
        outs = {}
        for unit in units + [None]:
            nxt = scores(*unit) if unit is not None else None
            if pending is not None:
                (t, hh), (k0, s_loc, s_ctx) = pending
                outs[hh] = _softmax_pv([(s_loc, v_ref[pl.ds(k0, NK), :]), (s_ctx, vc)])
                if hh == HEAD_PAIR - 1:
                    q0 = pl.multiple_of(t * TQ, TQ)
                    o_ref[pl.ds(q0, TQ), :] = jnp.where(first, outs[0], outs[1]).astype(o_ref.dtype)
            pending = (unit, nxt)
        return carry

    lax.fori_loop(0, n_tiles // ATTN_UNROLL, tile_group, 0)


def _attn(proj, cproj, slabs, kc_blk, vc_blk):
    b, n, _ = proj.shape
    rows = n // GRID_W
    n_tiles = rows // Q_ROWS
    lat = lambda off: pl.BlockSpec((None, n, LANES), lambda p, i: (i, 0, off + p))
    cx = lambda off: pl.BlockSpec((None, CTX_LEN, LANES), lambda p, i: (i, 0, off + p))
    return pl.pallas_call(
        functools.partial(_attn_kernel, n_tiles=n_tiles, rows=rows),
        grid=(N_PAIRS, b),
        in_specs=[
            lat(0), lat(N_PAIRS), lat(2 * N_PAIRS), cx(kc_blk), cx(vc_blk),
            pl.BlockSpec((HEAD_PAIR,) + slabs.shape[1:], lambda p, i: (p, 0, 0, 0)),
        ],
        out_specs=pl.BlockSpec((None, n, LANES), lambda p, i: (i, 0, p)),
        out_shape=jax.ShapeDtypeStruct((b, n, NA_WIDTH), BF16),
        scratch_shapes=[pltpu.VMEM((3, HEAD_PAIR, TQ, NK), F32)],
        compiler_params=_params("arbitrary", "arbitrary"),
        name="nbr_attn",
    )(proj, proj, proj, cproj, cproj, slabs)


def _ctx_attn_kernel(q_ref, k_ref, v_ref, o_ref):
    first, sels = _head_masks()
    q = q_ref[...] * ATTN_SCALE
    k = k_ref[...]
    v = v_ref[...]
    outs = []
    for hh in range(HEAD_PAIR):
        qm = jnp.where(sels[hh], q, jnp.zeros_like(q))
        outs.append(_softmax_pv([(_dot_nt(qm, k), v)]))
    o_ref[...] = jnp.where(first, outs[0], outs[1]).astype(o_ref.dtype)


def _ctx_attn(cproj):
    b, n, _ = cproj.shape
    spec = lambda off: pl.BlockSpec((None, n, LANES), lambda i, p: (i, 0, off + p))
    return pl.pallas_call(
        _ctx_attn_kernel,
        grid=(b, N_PAIRS),
        in_specs=[spec(0), spec(N_PAIRS), spec(2 * N_PAIRS)],
        out_specs=spec(0),
        out_shape=jax.ShapeDtypeStruct((b, n, NA_WIDTH), BF16),
        compiler_params=_params("parallel", "parallel"),
        name="ctx_attn",
    )(cproj, cproj, cproj)


MASKED_SLAB = 2 * WIN_ROWS - 1


def _attn_bias_slabs(rpb):
    qc = np.arange(GRID_W)[:, None]
    kc = np.arange(GRID_W)[None, :]
    cs = np.clip(qc - WIN_COLS // 2, 0, GRID_W - WIN_COLS)
    col_ok = (kc >= cs) & (kc < cs + WIN_COLS)
    dc = kc - qc + WIN_COLS - 1
    onehot = np.zeros((2 * WIN_COLS - 1, GRID_W, GRID_W), np.float32)
    qi, ki = np.nonzero(col_ok)
    onehot[dc[qi, ki], qi, ki] = 1.0
    t1 = jnp.einsum('hrd,dqk->hrqk', rpb, jnp.asarray(onehot), precision=lax.Precision.HIGHEST)
    t1 = jnp.where(jnp.asarray(col_ok)[None, None], t1, NEG)
    neg_slab = jnp.full((NA_HEADS, 1, GRID_W, GRID_W), NEG, F32)
    return jnp.concatenate([t1, neg_slab], axis=1)


def _attn_slab_pattern(rows):
    n_tiles = rows // Q_ROWS
    idx = np.full((3, Q_ROWS, K_ROWS), MASKED_SLAB, np.int32)
    for p, t in enumerate((0, 1, n_tiles - 1)):
        k_start = int(np.clip(t * Q_ROWS - WIN_ROWS // 2, 0, rows - K_ROWS))
        for qr in range(Q_ROWS):
            r = t * Q_ROWS + qr
            rs = int(np.clip(r - WIN_ROWS // 2, 0, rows - WIN_ROWS))
            for kr in range(K_ROWS):
                krow = k_start + kr
                if rs <= krow < rs + WIN_ROWS:
                    idx[p, qr, kr] = krow - r + WIN_ROWS - 1
    return idx


def _split_bf16(a):
    hi = a.astype(BF16)
    lo = (a - hi.astype(F32)).astype(BF16)
    return hi, lo


def _dot3(a, b):
    ah, al = _split_bf16(a)
    bh, bl = _split_bf16(b)
    return _dot(ah, bh) + (_dot(ah, bl) + _dot(al, bh))


def _four_w_kernel(cc_ref, sc_ref, w_ref, o_ref):
    w = w_ref[...]
    o_ref[:, :FOURIER_GROUP_DIM] = _dot3(cc_ref[...], w).astype(BF16)
    o_ref[:, FOURIER_GROUP_DIM:] = _dot3(sc_ref[...], w).astype(BF16)


def _four_w(cc, sc, w_four):
    depth, g, c, _ = w_four.shape
    tab = pl.BlockSpec((c, c), lambda l, i: (0, 0))
    return pl.pallas_call(
        _four_w_kernel,
        grid=(depth, g),
        in_specs=[tab, tab, pl.BlockSpec((None, None, c, c), lambda l, i: (l, i, 0, 0))],
        out_specs=pl.BlockSpec((None, None, c, 2 * c), lambda l, i: (l, i, 0, 0)),
        out_shape=jax.ShapeDtypeStruct((depth, g, c, 2 * c), BF16),
        compiler_params=_params("parallel", "parallel"),
        name="fourier_weights",
    )(cc, sc, w_four)


def _four_chan_kernel(f_ref, m_ref, *refs):
    out_refs, (yc_scr, ys_scr) = refs[:2 * DFT_RADIX], refs[2 * DFT_RADIX:]
    c = FOURIER_GROUP_DIM
    part_rows = f_ref.shape[0] // DFT_RADIX
    for g in range(FOURIER_GROUPS):
        cols = slice(g * c, (g + 1) * c)
        y = _dot(f_ref[:, cols], m_ref[g])
        yc_scr[...] = y[:, :c]
        ys_scr[...] = y[:, c:]
        for p in range(DFT_RADIX):
            out_refs[2 * p][:, cols] = yc_scr[pl.ds(p, part_rows, stride=DFT_RADIX), :].astype(BF16)
            out_refs[2 * p + 1][:, cols] = ys_scr[pl.ds(p, part_rows, stride=DFT_RADIX), :].astype(BF16)


def _four_chan(proj, mw, tm):
    t = proj.shape[0]
    f_blk = proj.shape[1] // FOURIER_WIDTH - 1
    out = pl.BlockSpec((tm // DFT_RADIX, FOURIER_WIDTH), lambda i: (i, 0))
    return pl.pallas_call(
        _four_chan_kernel,
        grid=(t // tm,),
        in_specs=[
            pl.BlockSpec((tm, FOURIER_WIDTH), lambda i: (i, f_blk)),
            pl.BlockSpec(mw.shape, lambda i: (0, 0, 0)),
        ],
        out_specs=[out] * (2 * DFT_RADIX),
        out_shape=[jax.ShapeDtypeStruct((t // DFT_RADIX, FOURIER_WIDTH), BF16)] * (2 * DFT_RADIX),
        scratch_shapes=[pltpu.VMEM((tm, FOURIER_GROUP_DIM), F32)] * 2,
        compiler_params=_params("parallel"),
        name="fourier_chan",
    )(proj, mw)


def _four_pos_kernel(c0, s0, c1, s1, c2, s2, c3, s3, ns1, nc3, a0, b0, a1, b1, a2, b2, a3, b3, o_ref):
    step = min(DFT_ROWS, c0.shape[0])
    for r0 in range(0, c0.shape[0], step):
        rows = slice(r0, r0 + step)
        dot = lambda table, data: _dot(table[rows, :], data[...])
        re0 = dot(c0, a0) + dot(s0, b0)
        re2 = dot(c2, a2) + dot(s2, b2)
        re13 = dot(c1, a1) + dot(s1, b1) + dot(c3, a3) + dot(s3, b3)
        im13 = dot(c1, b1) + dot(ns1, a1) + dot(nc3, b3) + dot(s3, a3)
        plus, minus = re0 + re2, re0 - re2
        o_ref[0, rows, :] = (plus + re13).astype(o_ref.dtype)
        o_ref[1, rows, :] = (minus + im13).astype(o_ref.dtype)
        o_ref[2, rows, :] = (plus - re13).astype(o_ref.dtype)
        o_ref[3, rows, :] = (minus - im13).astype(o_ref.dtype)


def _four_pos(tables, y_parts, b, tm, tn):
    data = [y.reshape(b, y.shape[0] // b, y.shape[1]) for y in y_parts]
    _, part, w = data[0].shape
    mat = pl.BlockSpec((tm, part), lambda i, bb, j: (i, 0), pipeline_mode=pl.Buffered(1))
    dat = pl.BlockSpec((None, part, tn), lambda i, bb, j: (bb, 0, j))
    out = pl.pallas_call(
        _four_pos_kernel,
        grid=(part // tm, b, w // tn),
        in_specs=[mat] * len(tables) + [dat] * len(data),
        out_specs=pl.BlockSpec((None, DFT_RADIX, tm, tn), lambda i, bb, j: (bb, 0, i, j)),
        out_shape=jax.ShapeDtypeStruct((b, DFT_RADIX, part, w), BF16),
        compiler_params=_params("parallel", "parallel", "parallel"),
        name="fourier_pos",
    )(*tables, *data)
    return out.reshape(b, DFT_RADIX * part, w)


def _dft_tables(n):
    part = n // DFT_RADIX
    split = 2 ** (part.bit_length() // 2)
    j = jnp.arange(part, dtype=jnp.int32)[:, None]
    hi = jnp.arange(part // split, dtype=jnp.int32)[None, :] * split
    lo = jnp.arange(split, dtype=jnp.int32)[None, :]
    ang = lambda k: (2.0 * np.pi / part) * ((j * k) % part).astype(F32)
    ch, sh, cl, sl = jnp.cos(ang(hi)), jnp.sin(ang(hi)), jnp.cos(ang(lo)), jnp.sin(ang(lo))
    c0 = (ch[:, :, None] * cl[:, None, :] - sh[:, :, None] * sl[:, None, :]).reshape(part, part)
    s0 = (sh[:, :, None] * cl[:, None, :] + ch[:, :, None] * sl[:, None, :]).reshape(part, part)
    tables = [c0, s0]
    for p in range(1, DFT_RADIX):
        phi = (2.0 * np.pi * p / n) * j.astype(F32)
        cp, sp = jnp.cos(phi), jnp.sin(phi)
        tables += [c0 * cp - s0 * sp, s0 * cp + c0 * sp]
    tables += [-tables[3], -tables[6]]
    return tuple(t.astype(BF16) for t in tables)


def _outproj_kernel(a_ref, f_ref, w_ref, x_ref, gpost_ref, gate_ref, gpre_ref, sh_ref, sc_ref,
                    xo_ref, h_ref, m0_ref, m1_ref, lhs_ref, *, n_tiles):
    mix_refs = (m0_ref, m1_ref)
    tm = x_ref.shape[0]

    def run(mix_to, epi_from):
        pieces, chunks = [], []
        if mix_to is not None:
            lhs_ref[:, :NA_WIDTH] = a_ref[...]
            lhs_ref[:, NA_WIDTH:] = f_ref[...]
            for c0 in range(0, w_ref.shape[1], OUT_COLS):
                def piece(cols=slice(c0, c0 + OUT_COLS)):
                    mix_refs[mix_to][:, cols] = _dot(lhs_ref[...], w_ref[:, cols])
                pieces.append(piece)
        if epi_from is not None:
            for r0 in range(0, tm, OUT_ROWS):
                def chunk(rows=slice(r0, r0 + OUT_ROWS)):
                    x = x_ref[rows, :] + gate_ref[...] * _rms(mix_refs[epi_from][rows, :], gpost_ref[...])
                    xo_ref[rows, :] = x
                    h = _norm_mod(x, gpre_ref[...], sh_ref[...], sc_ref[...])
                    h_ref[rows, :] = h.astype(BF16)
                    if mix_to is not None:
                        _order_before_reads(lhs_ref, h)
                chunks.append(chunk)
        _interleave(pieces, chunks)

    _two_stage(pl.program_id(0), n_tiles, run)


def _outproj(attn, four, w, x, g_post, g_pre, mods, row_fn, tm):
    t, d = x.shape
    n_tiles = t // tm
    cur = lambda i: jnp.minimum(i, n_tiles - 1)
    prev = lambda i: jnp.maximum(i - 1, 0)
    vec = pl.BlockSpec((1, d), lambda i: (0, 0))
    row = pl.BlockSpec((tm, d), lambda i: (prev(i), 0))
    half = pl.BlockSpec((tm, NA_WIDTH), lambda i: (cur(i), 0))
    mod = lambda which: _mod_spec(which, lambda i: row_fn(prev(i)))
    return pl.pallas_call(
        functools.partial(_outproj_kernel, n_tiles=n_tiles),
        grid=(n_tiles + 1,),
        in_specs=[half, half, _wspec(w, w[0].shape[1:], lambda i: (0, 0), pipeline_mode=pl.Buffered(1)), row, vec,
                  mod(2), vec, mod(3), mod(4)],
        out_specs=[row, row],
        out_shape=[jax.ShapeDtypeStruct((t, d), F32), jax.ShapeDtypeStruct((t, d), BF16)],
        scratch_shapes=[pltpu.VMEM((tm, d), F32), pltpu.VMEM((tm, d), F32),
                        pltpu.VMEM((tm, NA_WIDTH + FOURIER_WIDTH), BF16)],
        compiler_params=_params("arbitrary"),
        name="outproj_residual",
    )(attn, four, w[0], x, g_post.reshape(1, d), mods, g_pre.reshape(1, d), mods, mods)


def _matmul_kernel(x_ref, w_ref, o_ref):
    o_ref[...] = _dot(x_ref[...], w_ref[...]).astype(o_ref.dtype)


def _matmul(x, w, tn, out_dtype, name):
    m, k = x.shape
    n = w[0].shape[2]
    return pl.pallas_call(
        _matmul_kernel,
        grid=(n // tn,),
        in_specs=[pl.BlockSpec((m, k), lambda j: (0, 0)), _wspec(w, (k, tn), lambda j: (0, j))],
        out_specs=pl.BlockSpec((m, tn), lambda j: (0, j)),
        out_shape=jax.ShapeDtypeStruct((m, n), out_dtype),
        compiler_params=_params("parallel"),
        name=name,
    )(x, w[0])


def _ffn_kernel(h_ref, wa_ref, wg_ref, edge_ref, cp_ref, wd_ref, x_ref, gate_ref, gpost_ref, o_ref,
                acc_ref, ua0, ug0, ua1, ug1, act0, act1,
                *, tm, n_ff_tiles, n_row_tiles, seq_len):
    i = pl.program_id(0)
    j = pl.program_id(1)
    nj = n_ff_tiles
    pad = F32_SUBLANES
    tf = wa_ref.shape[1]
    u_refs = ((ua0, ug0), (ua1, ug1))
    act_refs = (act0, act1)

    def up_proj_part(half, k0):
        w_ref = (wa_ref, wg_ref)[half]
        return _dot(h_ref[:, k0:k0 + UP_K], w_ref[k0:k0 + UP_K, :])

    def up_proj_store(parity, half, u):
        u_ref = u_refs[parity][half]
        u_ref[0:pad, :] = jnp.broadcast_to(edge_ref[half:half + 1, :], (pad, tf))
        u_ref[pad:pad + tm, :] = u
        u_ref[pad + tm:, :] = jnp.broadcast_to(edge_ref[2 + half:3 + half, :], (pad, tf))

    def conv(u_ref, r0, cols, p0):
        win = u_ref[r0:r0 + CONV_ROWS + 2 * pad, cols]
        mid = win[pad:pad + CONV_ROWS]
        down = pltpu.roll(win, 1, axis=0)[pad:pad + CONV_ROWS]
        up = pltpu.roll(win, CONV_ROWS + 2 * pad - 1, axis=0)[pad:pad + CONV_ROWS]
        row = lax.broadcasted_iota(jnp.int32, (CONV_ROWS, 1), 0)
        if r0 > 0 and r0 % seq_len == 0:
            down = jnp.where(row == 0, 0.0, down)
        if r0 + CONV_ROWS < tm and (r0 + CONV_ROWS) % seq_len == 0:
            up = jnp.where(row == CONV_ROWS - 1, 0.0, up)
        return (down * cp_ref[p0:p0 + 1, cols] + mid * cp_ref[p0 + 1:p0 + 2, cols]
                + up * cp_ref[p0 + 2:p0 + 3, cols] + cp_ref[p0 + 3:p0 + 4, cols])

    def conv_gate(parity, r0, c0):
        ua_ref, ug_ref = u_refs[parity]
        cols = slice(c0, c0 + LANES)
        a = conv(ua_ref, r0, cols, 0)
        g = conv(ug_ref, r0, cols, CONV_TAPS + 1)
        act_refs[parity][r0:r0 + CONV_ROWS, cols] = (g * jax.nn.sigmoid(g) * a).astype(BF16)

    def down_proj(parity, cols):
        return _dot(act_refs[parity][...], wd_ref[:, cols])

    chunks = [(r0, c0) for r0 in range(0, tm, CONV_ROWS) for c0 in range(0, tf, LANES)]
    d_out = wd_ref.shape[1]
    down_groups = [slice(c, c + DOWN_COLS) for c in range(0, d_out, DOWN_COLS)]

    def finish_rows(r0):
        rows = slice(r0, r0 + FINISH_ROWS)
        o_ref[rows, :] = x_ref[rows, :] + gate_ref[...] * _rms(acc_ref[rows, :], gpost_ref[...])
        acc_ref[rows, :] = jnp.zeros((FINISH_ROWS, d_out), F32)

    def run_step(up=None, conv_of=None, down=None, finish=False):
        pieces = []
        if down is not None:
            for cols in down_groups:
                def piece(cols=cols):
                    acc_ref[:, cols] += down_proj(down, cols)
                pieces.append((piece, tf * DOWN_COLS))
        n_down = len(pieces)
        if up is not None:
            d_in = h_ref.shape[1]
            for half in range(2):
                partial_u = []
                for k0 in range(0, d_in, UP_K):
                    def piece(half=half, k0=k0, partial_u=partial_u):
                        part = up_proj_part(half, k0)
                        partial_u[:] = [part if not partial_u else partial_u[0] + part]
                        if k0 + UP_K == d_in:
                            up_proj_store(up, half, partial_u[0])
                    pieces.append((piece, UP_K * tf))
        conv_todo = [functools.partial(conv_gate, conv_of, *c) for c in chunks] if conv_of is not None else []
        fin_todo = [functools.partial(finish_rows, r0) for r0 in range(0, tm, FINISH_ROWS)] if finish else []
        n_conv, n_fin = len(conv_todo), len(fin_todo)
        total = sum(w for _, w in pieces)
        total_up = sum(w for _, w in pieces[n_down:])
        done = done_up = 0.0
        for n, (piece, w) in enumerate(pieces):
            piece()
            done += w
            is_last = n == len(pieces) - 1
            conv_upto = n_conv if is_last else round(n_conv * done / total)
            while n_conv - len(conv_todo) < conv_upto:
                conv_todo.pop(0)()
            if n >= n_down:
                done_up += w
                fin_upto = n_fin if is_last else round(n_fin * done_up / total_up)
                while n_fin - len(fin_todo) < fin_upto:
                    fin_todo.pop(0)()
        for chunk in conv_todo + fin_todo:
            chunk()

    s = i * nj + j
    last = n_row_tiles * nj
    finishing = (j == 1) & (i >= 1)

    @pl.when(s == 0)
    def _():
        acc_ref[...] = jnp.zeros_like(acc_ref)
        run_step(up=0)

    @pl.when(s == 1)
    def _():
        run_step(up=1, conv_of=0)

    for parity in range(2):
        steady = (s >= 2) & (s < last) & (s % 2 == parity)

        @pl.when(steady & jnp.logical_not(finishing))
        def _():
            run_step(up=parity, conv_of=1 - parity, down=parity)

        @pl.when(steady & finishing)
        def _():
            run_step(up=parity, conv_of=1 - parity, down=parity, finish=True)

    @pl.when(s == last)
    def _():
        run_step(conv_of=(last - 1) % 2, down=last % 2)

    @pl.when(s == last + 1)
    def _():
        run_step(down=(last + 1) % 2, finish=True)


def _ffn(h, w_up, conv_w, conv_b, w_down, x, g_post, mods, row_fn, seq_len, tm):
    t, d = x.shape
    ff = w_down[0].shape[1]
    tf = FF_TILE
    nj = ff // tf
    n_tiles = t // tm
    assert seq_len % tm == 0 or (tm % seq_len == 0 and seq_len % CONV_ROWS == 0)

    tile0 = np.arange(n_tiles) * tm
    start = tile0 % seq_len == 0
    end = (tile0 + tm) % seq_len == 0
    prev = jnp.take(h, jnp.asarray(np.maximum(tile0 - 1, 0)), axis=0)
    nxt = jnp.take(h, jnp.asarray(np.minimum(tile0 + tm, t - 1)), axis=0)
    prev = jnp.where(jnp.asarray(start)[:, None], jnp.zeros_like(prev), prev)
    nxt = jnp.where(jnp.asarray(end)[:, None], jnp.zeros_like(nxt), nxt)
    edge_rows = jnp.concatenate([prev, nxt], axis=0)
    pad = -edge_rows.shape[0] % BF16_SUBLANES
    edge_rows = jnp.pad(edge_rows, ((0, pad), (0, 0)))
    u_edge = _matmul(edge_rows, w_up, 2 * ff // EDGE_STEPS, F32, "conv_ffn_edges")
    u_prev, u_next = u_edge[:n_tiles], u_edge[n_tiles:2 * n_tiles]
    edges = jnp.stack([u_prev[:, :ff], u_prev[:, ff:], u_next[:, :ff], u_next[:, ff:]], axis=1)
    edges = edges.reshape(n_tiles, 4, nj, tf).transpose(0, 2, 1, 3)
    conv_p = jnp.concatenate([conv_w[:, :ff], conv_b[None, :ff], conv_w[:, ff:], conv_b[None, ff:]], axis=0)
    conv_p = conv_p.reshape(2 * (CONV_TAPS + 1), nj, tf).transpose(1, 0, 2)

    assert nj >= 3
    vec = pl.BlockSpec((1, d), lambda i, j: (0, 0))
    cur = lambda i: jnp.minimum(i, n_tiles - 1)
    fin = lambda i, j, switch: jnp.where(j <= switch, jnp.maximum(i - 1, 0), cur(i))
    lag = lambda j, k: jnp.where(j >= k, j - k, j + nj - k)
    up_cols = lambda off: _wspec(w_up, (d, tf), lambda i, j: (0, off + j))
    u_buf = pltpu.VMEM((tm + 2 * F32_SUBLANES, tf), F32)
    act_buf = pltpu.VMEM((tm, tf), BF16)
    return pl.pallas_call(
        functools.partial(_ffn_kernel, tm=tm, n_ff_tiles=nj, n_row_tiles=n_tiles, seq_len=seq_len),
        grid=(n_tiles + 1, nj),
        in_specs=[
            pl.BlockSpec((tm, d), lambda i, j: (cur(i), 0)),
            up_cols(0), up_cols(nj),
            pl.BlockSpec((None, None, 4, tf), lambda i, j: (cur(i), j, 0, 0)),
            pl.BlockSpec((None, 2 * (CONV_TAPS + 1), tf), lambda i, j: (lag(j, 1), 0, 0)),
            _wspec(w_down, (tf, d), lambda i, j: (lag(j, 2), 0)),
            pl.BlockSpec((tm, d), lambda i, j: (fin(i, j, 1), 0)),
            pl.BlockSpec((None, None, 1, d), lambda i, j: (row_fn(fin(i, j, 1)), 5, 0, 0)),
            vec,
        ],
        out_specs=pl.BlockSpec((tm, d), lambda i, j: (fin(i, j, 3), 0)),
        out_shape=jax.ShapeDtypeStruct((t, d), F32),
        scratch_shapes=[pltpu.VMEM((tm, d), F32), u_buf, u_buf, u_buf, u_buf, act_buf, act_buf],
        compiler_params=_params("arbitrary", "arbitrary"),
        name="conv_ffn",
    )(h, w_up[0], w_up[0], edges, conv_p, w_down[0], x, mods,
      g_post.reshape(1, d))


def kernel(x, c, ctx, c_ctx, w_ada, b_ada, g_pre_mix, w_in, rpb, w_four, w_out, g_post_mix,
           g_pre_ffn, w_up, conv_w, conv_b, w_down, g_post_ffn):
    b, n, d = x.shape
    n_ctx = ctx.shape[1]
    assert (d, n, n_ctx, b) == (D_MODEL, GRID_W * GRID_W, CTX_LEN, CTX_MOD_ROW)
    t_lat, t_ctx = b * n, b * n_ctx
    tm_lat = tm_ctx = ROW_TILE
    tiles_per_seq = n // tm_lat
    assert tiles_per_seq & (tiles_per_seq - 1) == 0
    lat_row = lambda i: i >> (tiles_per_seq.bit_length() - 1)
    ctx_row = lambda i: CTX_MOD_ROW

    cond = jnp.zeros((MOD_ROWS, d), F32).at[:b].set(c).at[CTX_MOD_ROW].set(c_ctx)
    mods_all = _mods(cond, w_ada, b_ada)

    scale = (n * FOURIER_GROUP_DIM) ** -0.5
    scale_ctx = (n_ctx * FOURIER_GROUP_DIM) ** -0.5
    kk = np.outer(np.arange(FOURIER_GROUP_DIM), np.arange(FOURIER_GROUP_DIM)) % FOURIER_GROUP_DIM
    ang = 2.0 * np.pi * kk / FOURIER_GROUP_DIM
    cc, sc = np.cos(ang), -np.sin(ang)
    mw_lat = _four_w(jnp.asarray(cc * scale, F32), jnp.asarray(sc * scale, F32), w_four)
    dft_lat = _dft_tables(n)

    w_in_b, w_out_b, w_up_b, w_down_b = (w.astype(BF16) for w in (w_in, w_out, w_up, w_down))
    xt = x.reshape(t_lat, d)
    ct = ctx.reshape(t_ctx, d)
    for l in range(DEPTH):
        last = l == DEPTH - 1
        mods = mods_all[l].reshape(MOD_ROWS, N_MOD, 1, d)
        w_in_l, w_out_l, w_up_l, w_down_l = ((w, l) for w in (w_in_b, w_out_b, w_up_b, w_down_b))

        proj = _inproj(xt, g_pre_mix[l], mods, lat_row, w_in_l, tm_lat)
        if last:
            w_kv = (w_in_b[l:l + 1, :, NA_WIDTH:3 * NA_WIDTH], 0)
            cproj = _inproj(ct, g_pre_mix[l], mods, ctx_row, w_kv, tm_ctx)
            kc_blk, vc_blk = 0, N_PAIRS
        else:
            cproj = _inproj(ct, g_pre_mix[l], mods, ctx_row, w_in_l, tm_ctx)
            kc_blk, vc_blk = N_PAIRS, 2 * N_PAIRS
        cproj3 = cproj.reshape(b, n_ctx, cproj.shape[1])

        attn = _attn(proj.reshape(b, n, PROJ_WIDTH), cproj3, _attn_bias_slabs(rpb[l]), kc_blk, vc_blk)
        four = _four_pos(dft_lat, _four_chan(proj, mw_lat[l], CHAN_TILE), b, n // DFT_RADIX, DFT_COLS)
        xt, h2 = _outproj(attn.reshape(t_lat, NA_WIDTH), four.reshape(t_lat, FOURIER_WIDTH), w_out_l, xt,
                          g_post_mix[l], g_pre_ffn[l], mods, lat_row, tm_lat)
        xt = _ffn(h2, w_up_l, conv_w[l], conv_b[l], w_down_l, xt, g_post_ffn[l], mods, lat_row, n, tm_lat)

        if not last:
            mw_ctx = _four_w(jnp.asarray(cc * scale_ctx, F32), jnp.asarray(sc * scale_ctx, F32),
                             w_four[l:l + 1])[0]
            dft_ctx = _dft_tables(n_ctx)
            attn_c = _ctx_attn(cproj3)
            four_c = _four_pos(dft_ctx, _four_chan(cproj, mw_ctx, ROW_TILE), b, n_ctx // DFT_RADIX, DFT_COLS)
            ct, hc2 = _outproj(attn_c.reshape(t_ctx, NA_WIDTH), four_c.reshape(t_ctx, FOURIER_WIDTH),
                               w_out_l, ct, g_post_mix[l], g_pre_ffn[l], mods, ctx_row, tm_ctx)
            ct = _ffn(hc2, w_up_l, conv_w[l], conv_b[l], w_down_l, ct, g_post_ffn[l], mods, ctx_row,
                      n_ctx, tm_ctx)
    return xt.reshape(b, n, d)
```

```python
import functools

import jax
import jax.numpy as jnp
import numpy as np
from jax import lax
from jax.experimental import pallas as pl
from jax.experimental.pallas import tpu as pltpu

D_MODEL = 2048
DEPTH = 2
GRID_W = 64
CTX_LEN = 256
NA_HEADS = 16
HEAD_DIM = 64
NA_WIDTH = NA_HEADS * HEAD_DIM
FOURIER_GROUPS = 8
FOURIER_GROUP_DIM = 128
FOURIER_WIDTH = FOURIER_GROUPS * FOURIER_GROUP_DIM
PROJ_WIDTH = 3 * NA_WIDTH + FOURIER_WIDTH
WIN_ROWS = 8
WIN_COLS = 16
D_FF = 5632
N_MOD = 6
EPS = 1e-6
ATTN_SCALE = HEAD_DIM ** -0.5
MOD_ROWS = 16
CTX_MOD_ROW = 8

ROW_TILE = 512
CHAN_TILE = 1024
DFT_COLS = 512
MODS_COLS = 1024
EDGE_STEPS = 8
LANES = 128
DFT_RADIX = 4
DFT_ROWS = 512
FF_TILE = 512
CONV_TAPS = 3
FINISH_ROWS = 64
UP_K = 256
DOWN_COLS = 512
OUT_COLS = 512
OUT_ROWS = 64
PROJ_COLS = 512
NORM_ROWS = 16
CONV_ROWS = 64
F32_SUBLANES = 8
BF16_SUBLANES = 16
HEAD_PAIR = LANES // HEAD_DIM
N_PAIRS = NA_HEADS // HEAD_PAIR
Q_ROWS = 4
K_ROWS = Q_ROWS + WIN_ROWS - 1
TQ = Q_ROWS * GRID_W
NK = K_ROWS * GRID_W
ATTN_UNROLL = 16
NEG = -1e30
VMEM_LIMIT = 56 * 1024 * 1024

F32 = jnp.float32
BF16 = jnp.bfloat16


def _params(*sem, flags=None):
    return pltpu.CompilerParams(dimension_semantics=sem, vmem_limit_bytes=VMEM_LIMIT, flags=flags)


def _dot(a, b):
    return jnp.dot(a, b, preferred_element_type=F32)


def _dot_nt(a, b):
    return lax.dot_general(a, b, (((1,), (1,)), ((), ())), preferred_element_type=F32)


def _rms(x, g):
    ms = jnp.mean(x * x, axis=-1, keepdims=True)
    return x * lax.rsqrt(ms + EPS) * g


def _norm_mod(x, g, shift, scale):
    return _rms(x, g) * (1.0 + scale) + shift


def _mods_kernel(cond_ref, w_ref, b_ref, o_ref):
    c = cond_ref[...]
    s = (c * jax.nn.sigmoid(c)).astype(BF16)
    o_ref[...] = _dot(s, w_ref[...].astype(BF16)) + b_ref[...]


def _mods(cond, w_ada, b_ada):
    depth, d, n = w_ada.shape
    tn = MODS_COLS
    return pl.pallas_call(
        _mods_kernel,
        grid=(depth, n // tn),
        in_specs=[
            pl.BlockSpec((MOD_ROWS, d), lambda l, j: (0, 0)),
            pl.BlockSpec((None, d, tn), lambda l, j: (l, 0, j)),
            pl.BlockSpec((None, 1, tn), lambda l, j: (l, 0, j)),
        ],
        out_specs=pl.BlockSpec((None, MOD_ROWS, tn), lambda l, j: (l, 0, j)),
        out_shape=jax.ShapeDtypeStruct((depth, MOD_ROWS, n), F32),
        compiler_params=_params("parallel", "parallel"),
        name="adaln_mods",
    )(cond, w_ada, b_ada.reshape(depth, 1, n))


def _wspec(w, block, index_fn, **kw):
    _, layer = w
    return pl.BlockSpec((None,) + tuple(block), lambda *idx: (layer,) + tuple(index_fn(*idx)), **kw)


def _mod_spec(which, row_fn):
    return pl.BlockSpec((None, None, 1, D_MODEL),
                        lambda i, *_: (row_fn(i), which, 0, 0))


def _interleave(pieces, chunks):
    if not pieces:
        for chunk in chunks:
            chunk()
        return
    slots = max(len(pieces) - 1, 1)
    per = -(-len(chunks) // slots)
    for n, piece in enumerate(pieces):
        piece()
        for chunk in chunks[n * per:(n + 1) * per]:
            chunk()


def _order_before_reads(ref, value):
    tile = (slice(0, BF16_SUBLANES), slice(0, LANES))
    bits = pltpu.bitcast(value[tile], jnp.uint32)
    zero = pltpu.bitcast((bits >> 16) >> 16, F32)
    ref[tile] = (ref[tile].astype(F32) + zero).astype(ref.dtype)


def _two_stage(i, n_tiles, run):
    @pl.when(i == 0)
    def _():
        run(0, None)

    for parity in range(2):
        @pl.when((i >= 1) & (i < n_tiles) & (i % 2 == parity))
        def _():
            run(parity, 1 - parity)

    @pl.when(i == n_tiles)
    def _():
        run(None, (n_tiles - 1) % 2)


def _inproj_kernel(x_ref, g_ref, sh_ref, sc_ref, w_ref, o_ref, h0_ref, h1_ref, *, n_tiles):
    h_refs = (h0_ref, h1_ref)
    tm = x_ref.shape[0]

    def run(norm_to, dot_from):
        pieces, chunks = [], []
        if dot_from is not None:
            for c0 in range(0, w_ref.shape[1], PROJ_COLS):
                def piece(cols=slice(c0, c0 + PROJ_COLS)):
                    o_ref[:, cols] = _dot(h_refs[dot_from][...], w_ref[:, cols]).astype(o_ref.dtype)
                pieces.append(piece)
        if norm_to is not None:
            for r0 in range(0, tm, NORM_ROWS):
                def chunk(rows=slice(r0, r0 + NORM_ROWS)):
                    val = _norm_mod(x_ref[rows, :], g_ref[...], sh_ref[...], sc_ref[...])
                    h_refs[norm_to][rows, :] = val.astype(BF16)
                    if dot_from is not None:
                        _order_before_reads(h_refs[dot_from], val)
                chunks.append(chunk)
        _interleave(pieces, chunks)

    _two_stage(pl.program_id(0), n_tiles, run)


def _inproj(x, g, mods, row_fn, w, tm):
    t, d = x.shape
    n = w[0].shape[2]
    n_tiles = t // tm
    cur = lambda i: jnp.minimum(i, n_tiles - 1)
    prev = lambda i: jnp.maximum(i - 1, 0)
    return pl.pallas_call(
        functools.partial(_inproj_kernel, n_tiles=n_tiles),
        grid=(n_tiles + 1,),
        in_specs=[
            pl.BlockSpec((tm, d), lambda i: (cur(i), 0)),
            pl.BlockSpec((1, d), lambda i: (0, 0)),
            _mod_spec(0, lambda i: row_fn(cur(i))),
            _mod_spec(1, lambda i: row_fn(cur(i))),
            _wspec(w, (d, n), lambda i: (0, 0), pipeline_mode=pl.Buffered(1)),
        ],
        out_specs=pl.BlockSpec((tm, n), lambda i: (prev(i), 0)),
        out_shape=jax.ShapeDtypeStruct((t, n), BF16),
        scratch_shapes=[pltpu.VMEM((tm, d), BF16)] * 2,
        compiler_params=_params("arbitrary"),
        name="norm_inproj",
    )(x, g.reshape(1, d), mods, mods, w[0])


def _softmax_pv(parts):
    m = functools.reduce(jnp.maximum, [s.max(axis=-1, keepdims=True) for s, _ in parts])
    num = 0.0
    den = 0.0
    for s, v in parts:
        p = jnp.exp(s - m)
        den = den + p.sum(axis=-1, keepdims=True)
        num = num + _dot(p.astype(BF16), v)
    return num / den


def _head_masks():
    lane = lax.broadcasted_iota(jnp.int32, (1, LANES), 1)
    first = lane < HEAD_DIM
    return first, [first, jnp.logical_not(first)]


def _attn_kernel(q_ref, k_ref, v_ref, kc_ref, vc_ref, slab_ref, o_ref, tab_ref, *, n_tiles, rows):
    first, sels = _head_masks()
    kc = kc_ref[...]
    vc = vc_ref[...]

    @pl.when(pl.program_id(1) == 0)
    def _():
        pattern = _attn_slab_pattern(rows)
        for p in range(pattern.shape[0]):
            for hh in range(HEAD_PAIR):
                for qr in range(Q_ROWS):
                    for kr in range(K_ROWS):
                        tab_ref[p, hh, qr * GRID_W:(qr + 1) * GRID_W, kr * GRID_W:(kr + 1) * GRID_W] = (
                            slab_ref[hh, int(pattern[p, qr, kr])])

    def scores(t, hh):
        q0 = pl.multiple_of(t * TQ, TQ)
        k0 = pl.multiple_of(jnp.clip(t * Q_ROWS - WIN_ROWS // 2, 0, rows - K_ROWS) * GRID_W, GRID_W)
        pat = jnp.where(t == 0, 0, jnp.where(t == n_tiles - 1, 2, 1))
        q = q_ref[pl.ds(q0, TQ), :] * ATTN_SCALE
        qm = jnp.where(sels[hh], q, jnp.zeros_like(q))
        s_loc = _dot_nt(qm, k_ref[pl.ds(k0, NK), :]) + tab_ref[pat, hh]
        s_ctx = _dot_nt(qm, kc)
        return k0, s_loc, s_ctx

    def tile_group(tg, carry):
        units = [(tg * ATTN_UNROLL + ti, hh) for ti in range(ATTN_UNROLL) for hh in range(HEAD_PAIR)]
        pending = None
        outs = {}
        for unit in units + [None]:
            nxt = scores(*unit) if unit is not None else None
            if pending is not None:
                (t, hh), (k0, s_loc, s_ctx) = pending
                outs[hh] = _softmax_pv([(s_loc, v_ref[pl.ds(k0, NK), :]), (s_ctx, vc)])
                if hh == HEAD_PAIR - 1:
                    q0 = pl.multiple_of(t * TQ, TQ)
                    o_ref[pl.ds(q0, TQ), :] = jnp.where(first, outs[0], outs[1]).astype(o_ref.dtype)
            pending = (unit, nxt)
        return carry

    lax.fori_loop(0, n_tiles // ATTN_UNROLL, tile_group, 0)


def _attn(proj, cproj, slabs, kc_blk, vc_blk):
    b, n, _ = proj.shape
    rows = n // GRID_W
    n_tiles = rows // Q_ROWS
    lat = lambda off: pl.BlockSpec((None, n, LANES), lambda p, i: (i, 0, off + p))
    cx = lambda off: pl.BlockSpec((None, CTX_LEN, LANES), lambda p, i: (i, 0, off + p))
    return pl.pallas_call(
        functools.partial(_attn_kernel, n_tiles=n_tiles, rows=rows),
        grid=(N_PAIRS, b),
        in_specs=[
            lat(0), lat(N_PAIRS), lat(2 * N_PAIRS), cx(kc_blk), cx(vc_blk),
            pl.BlockSpec((HEAD_PAIR,) + slabs.shape[1:], lambda p, i: (p, 0, 0, 0)),
        ],
        out_specs=pl.BlockSpec((None, n, LANES), lambda p, i: (i, 0, p)),
        out_shape=jax.ShapeDtypeStruct((b, n, NA_WIDTH), BF16),
        scratch_shapes=[pltpu.VMEM((3, HEAD_PAIR, TQ, NK), F32)],
        compiler_params=_params("arbitrary", "arbitrary"),
        name="nbr_attn",
    )(proj, proj, proj, cproj, cproj, slabs)


def _ctx_attn_kernel(q_ref, k_ref, v_ref, o_ref):
    first, sels = _head_masks()
    q = q_ref[...] * ATTN_SCALE
    k = k_ref[...]
    v = v_ref[...]
    outs = []
    for hh in range(HEAD_PAIR):
        qm = jnp.where(sels[hh], q, jnp.zeros_like(q))
        outs.append(_softmax_pv([(_dot_nt(qm, k), v)]))
    o_ref[...] = jnp.where(first, outs[0], outs[1]).astype(o_ref.dtype)


def _ctx_attn(cproj):
    b, n, _ = cproj.shape
    spec = lambda off: pl.BlockSpec((None, n, LANES), lambda i, p: (i, 0, off + p))
    return pl.pallas_call(
        _ctx_attn_kernel,
        grid=(b, N_PAIRS),
        in_specs=[spec(0), spec(N_PAIRS), spec(2 * N_PAIRS)],
        out_specs=spec(0),
        out_shape=jax.ShapeDtypeStruct((b, n, NA_WIDTH), BF16),
        compiler_params=_params("parallel", "parallel"),
        name="ctx_attn",
    )(cproj, cproj, cproj)


MASKED_SLAB = 2 * WIN_ROWS - 1


def _attn_bias_slabs(rpb):
    qc = np.arange(GRID_W)[:, None]
    kc = np.arange(GRID_W)[None, :]
    cs = np.clip(qc - WIN_COLS // 2, 0, GRID_W - WIN_COLS)
    col_ok = (kc >= cs) & (kc < cs + WIN_COLS)
    dc = kc - qc + WIN_COLS - 1
    onehot = np.zeros((2 * WIN_COLS - 1, GRID_W, GRID_W), np.float32)
    qi, ki = np.nonzero(col_ok)
    onehot[dc[qi, ki], qi, ki] = 1.0
    t1 = jnp.einsum('hrd,dqk->hrqk', rpb, jnp.asarray(onehot), precision=lax.Precision.HIGHEST)
    t1 = jnp.where(jnp.asarray(col_ok)[None, None], t1, NEG)
    neg_slab = jnp.full((NA_HEADS, 1, GRID_W, GRID_W), NEG, F32)
    return jnp.concatenate([t1, neg_slab], axis=1)


def _attn_slab_pattern(rows):
    n_tiles = rows // Q_ROWS
    idx = np.full((3, Q_ROWS, K_ROWS), MASKED_SLAB, np.int32)
    for p, t in enumerate((0, 1, n_tiles - 1)):
        k_start = int(np.clip(t * Q_ROWS - WIN_ROWS // 2, 0, rows - K_ROWS))
        for qr in range(Q_ROWS):
            r = t * Q_ROWS + qr
            rs = int(np.clip(r - WIN_ROWS // 2, 0, rows - WIN_ROWS))
            for kr in range(K_ROWS):
                krow = k_start + kr
                if rs <= krow < rs + WIN_ROWS:
                    idx[p, qr, kr] = krow - r + WIN_ROWS - 1
    return idx


def _split_bf16(a):
    hi = a.astype(BF16)
    lo = (a - hi.astype(F32)).astype(BF16)
    return hi, lo


def _dot3(a, b):
    ah, al = _split_bf16(a)
    bh, bl = _split_bf16(b)
    return _dot(ah, bh) + (_dot(ah, bl) + _dot(al, bh))


def _four_w_kernel(cc_ref, sc_ref, w_ref, o_ref):
    w = w_ref[...]
    o_ref[:, :FOURIER_GROUP_DIM] = _dot3(cc_ref[...], w).astype(BF16)
    o_ref[:, FOURIER_GROUP_DIM:] = _dot3(sc_ref[...], w).astype(BF16)


def _four_w(cc, sc, w_four):
    depth, g, c, _ = w_four.shape
    tab = pl.BlockSpec((c, c), lambda l, i: (0, 0))
    return pl.pallas_call(
        _four_w_kernel,
        grid=(depth, g),
        in_specs=[tab, tab, pl.BlockSpec((None, None, c, c), lambda l, i: (l, i, 0, 0))],
        out_specs=pl.BlockSpec((None, None, c, 2 * c), lambda l, i: (l, i, 0, 0)),
        out_shape=jax.ShapeDtypeStruct((depth, g, c, 2 * c), BF16),
        compiler_params=_params("parallel", "parallel"),
        name="fourier_weights",
    )(cc, sc, w_four)


def _four_chan_kernel(f_ref, m_ref, *refs):
    out_refs, (yc_scr, ys_scr) = refs[:2 * DFT_RADIX], refs[2 * DFT_RADIX:]
    c = FOURIER_GROUP_DIM
    part_rows = f_ref.shape[0] // DFT_RADIX
    for g in range(FOURIER_GROUPS):
        cols = slice(g * c, (g + 1) * c)
        y = _dot(f_ref[:, cols], m_ref[g])
        yc_scr[...] = y[:, :c]
        ys_scr[...] = y[:, c:]
        for p in range(DFT_RADIX):
            out_refs[2 * p][:, cols] = yc_scr[pl.ds(p, part_rows, stride=DFT_RADIX), :].astype(BF16)
            out_refs[2 * p + 1][:, cols] = ys_scr[pl.ds(p, part_rows, stride=DFT_RADIX), :].astype(BF16)


def _four_chan(proj, mw, tm):
    t = proj.shape[0]
    f_blk = proj.shape[1] // FOURIER_WIDTH - 1
    out = pl.BlockSpec((tm // DFT_RADIX, FOURIER_WIDTH), lambda i: (i, 0))
    return pl.pallas_call(
        _four_chan_kernel,
        grid=(t // tm,),
        in_specs=[
            pl.BlockSpec((tm, FOURIER_WIDTH), lambda i: (i, f_blk)),
            pl.BlockSpec(mw.shape, lambda i: (0, 0, 0)),
        ],
        out_specs=[out] * (2 * DFT_RADIX),
        out_shape=[jax.ShapeDtypeStruct((t // DFT_RADIX, FOURIER_WIDTH), BF16)] * (2 * DFT_RADIX),
        scratch_shapes=[pltpu.VMEM((tm, FOURIER_GROUP_DIM), F32)] * 2,
        compiler_params=_params("parallel"),
        name="fourier_chan",
    )(proj, mw)


def _four_pos_kernel(c0, s0, c1, s1, c2, s2, c3, s3, ns1, nc3, a0, b0, a1, b1, a2, b2, a3, b3, o_ref):
    step = min(DFT_ROWS, c0.shape[0])
    for r0 in range(0, c0.shape[0], step):
        rows = slice(r0, r0 + step)
        dot = lambda table, data: _dot(table[rows, :], data[...])
        re0 = dot(c0, a0) + dot(s0, b0)
        re2 = dot(c2, a2) + dot(s2, b2)
        re13 = dot(c1, a1) + dot(s1, b1) + dot(c3, a3) + dot(s3, b3)
        im13 = dot(c1, b1) + dot(ns1, a1) + dot(nc3, b3) + dot(s3, a3)
        plus, minus = re0 + re2, re0 - re2
        o_ref[0, rows, :] = (plus + re13).astype(o_ref.dtype)
        o_ref[1, rows, :] = (minus + im13).astype(o_ref.dtype)
        o_ref[2, rows, :] = (plus - re13).astype(o_ref.dtype)
        o_ref[3, rows, :] = (minus - im13).astype(o_ref.dtype)


def _four_pos(tables, y_parts, b, tm, tn):
    data = [y.reshape(b, y.shape[0] // b, y.shape[1]) for y in y_parts]
    _, part, w = data[0].shape
    mat = pl.BlockSpec((tm, part), lambda i, bb, j: (i, 0), pipeline_mode=pl.Buffered(1))
    dat = pl.BlockSpec((None, part, tn), lambda i, bb, j: (bb, 0, j))
    out = pl.pallas_call(
        _four_pos_kernel,
        grid=(part // tm, b, w // tn),
        in_specs=[mat] * len(tables) + [dat] * len(data),
        out_specs=pl.BlockSpec((None, DFT_RADIX, tm, tn), lambda i, bb, j: (bb, 0, i, j)),
        out_shape=jax.ShapeDtypeStruct((b, DFT_RADIX, part, w), BF16),
        compiler_params=_params("parallel", "parallel", "parallel"),
        name="fourier_pos",
    )(*tables, *data)
    return out.reshape(b, DFT_RADIX * part, w)


def _dft_tables(n):
    part = n // DFT_RADIX
    split = 2 ** (part.bit_length() // 2)
    j = jnp.arange(part, dtype=jnp.int32)[:, None]
    hi = jnp.arange(part // split, dtype=jnp.int32)[None, :] * split
    lo = jnp.arange(split, dtype=jnp.int32)[None, :]
    ang = lambda k: (2.0 * np.pi / part) * ((j * k) % part).astype(F32)
    ch, sh, cl, sl = jnp.cos(ang(hi)), jnp.sin(ang(hi)), jnp.cos(ang(lo)), jnp.sin(ang(lo))
    c0 = (ch[:, :, None] * cl[:, None, :] - sh[:, :, None] * sl[:, None, :]).reshape(part, part)
    s0 = (sh[:, :, None] * cl[:, None, :] + ch[:, :, None] * sl[:, None, :]).reshape(part, part)
    tables = [c0, s0]
    for p in range(1, DFT_RADIX):
        phi = (2.0 * np.pi * p / n) * j.astype(F32)
        cp, sp = jnp.cos(phi), jnp.sin(phi)
        tables += [c0 * cp - s0 * sp, s0 * cp + c0 * sp]
    tables += [-tables[3], -tables[6]]
    return tuple(t.astype(BF16) for t in tables)


def _outproj_kernel(a_ref, f_ref, w_ref, x_ref, gpost_ref, gate_ref, gpre_ref, sh_ref, sc_ref,
                    xo_ref, h_ref, m0_ref, m1_ref, lhs_ref, *, n_tiles):
    mix_refs = (m0_ref, m1_ref)
    tm = x_ref.shape[0]

    def run(mix_to, epi_from):
        pieces, chunks = [], []
        if mix_to is not None:
            lhs_ref[:, :NA_WIDTH] = a_ref[...]
            lhs_ref[:, NA_WIDTH:] = f_ref[...]
            for c0 in range(0, w_ref.shape[1], OUT_COLS):
                def piece(cols=slice(c0, c0 + OUT_COLS)):
                    mix_refs[mix_to][:, cols] = _dot(lhs_ref[...], w_ref[:, cols])
                pieces.append(piece)
        if epi_from is not None:
            for r0 in range(0, tm, OUT_ROWS):
                def chunk(rows=slice(r0, r0 + OUT_ROWS)):
                    x = x_ref[rows, :] + gate_ref[...] * _rms(mix_refs[epi_from][rows, :], gpost_ref[...])
                    xo_ref[rows, :] = x
                    h = _norm_mod(x, gpre_ref[...], sh_ref[...], sc_ref[...])
                    h_ref[rows, :] = h.astype(BF16)
                    if mix_to is not None:
                        _order_before_reads(lhs_ref, h)
                chunks.append(chunk)
        _interleave(pieces, chunks)

    _two_stage(pl.program_id(0), n_tiles, run)


def _outproj(attn, four, w, x, g_post, g_pre, mods, row_fn, tm):
    t, d = x.shape
    n_tiles = t // tm
    cur = lambda i: jnp.minimum(i, n_tiles - 1)
    prev = lambda i: jnp.maximum(i - 1, 0)
    vec = pl.BlockSpec((1, d), lambda i: (0, 0))
    row = pl.BlockSpec((tm, d), lambda i: (prev(i), 0))
    half = pl.BlockSpec((tm, NA_WIDTH), lambda i: (cur(i), 0))
    mod = lambda which: _mod_spec(which, lambda i: row_fn(prev(i)))
    return pl.pallas_call(
        functools.partial(_outproj_kernel, n_tiles=n_tiles),
        grid=(n_tiles + 1,),
        in_specs=[half, half, _wspec(w, w[0].shape[1:], lambda i: (0, 0), pipeline_mode=pl.Buffered(1)), row, vec,
                  mod(2), vec, mod(3), mod(4)],
        out_specs=[row, row],
        out_shape=[jax.ShapeDtypeStruct((t, d), F32), jax.ShapeDtypeStruct((t, d), BF16)],
        scratch_shapes=[pltpu.VMEM((tm, d), F32), pltpu.VMEM((tm, d), F32),
                        pltpu.VMEM((tm, NA_WIDTH + FOURIER_WIDTH), BF16)],
        compiler_params=_params("arbitrary"),
        name="outproj_residual",
    )(attn, four, w[0], x, g_post.reshape(1, d), mods, g_pre.reshape(1, d), mods, mods)


def _matmul_kernel(x_ref, w_ref, o_ref):
    o_ref[...] = _dot(x_ref[...], w_ref[...]).astype(o_ref.dtype)


def _matmul(x, w, tn, out_dtype, name):
    m, k = x.shape
    n = w[0].shape[2]
    return pl.pallas_call(
        _matmul_kernel,
        grid=(n // tn,),
        in_specs=[pl.BlockSpec((m, k), lambda j: (0, 0)), _wspec(w, (k, tn), lambda j: (0, j))],
        out_specs=pl.BlockSpec((m, tn), lambda j: (0, j)),
        out_shape=jax.ShapeDtypeStruct((m, n), out_dtype),
        compiler_params=_params("parallel"),
        name=name,
    )(x, w[0])


def _ffn_kernel(h_ref, wa_ref, wg_ref, edge_ref, cp_ref, wd_ref, x_ref, gate_ref, gpost_ref, o_ref,
                acc_ref, ua0, ug0, ua1, ug1, act0, act1,
                *, tm, n_ff_tiles, n_row_tiles, seq_len):
    i = pl.program_id(0)
    j = pl.program_id(1)
    nj = n_ff_tiles
    pad = F32_SUBLANES
    tf = wa_ref.shape[1]
    u_refs = ((ua0, ug0), (ua1, ug1))
    act_refs = (act0, act1)

    def up_proj_part(half, k0):
        w_ref = (wa_ref, wg_ref)[half]
        return _dot(h_ref[:, k0:k0 + UP_K], w_ref[k0:k0 + UP_K, :])

    def up_proj_store(parity, half, u):
        u_ref = u_refs[parity][half]
        u_ref[0:pad, :] = jnp.broadcast_to(edge_ref[half:half + 1, :], (pad, tf))
        u_ref[pad:pad + tm, :] = u
        u_ref[pad + tm:, :] = jnp.broadcast_to(edge_ref[2 + half:3 + half, :], (pad, tf))

    def conv(u_ref, r0, cols, p0):
        win = u_ref[r0:r0 + CONV_ROWS + 2 * pad, cols]
        mid = win[pad:pad + CONV_ROWS]
        down = pltpu.roll(win, 1, axis=0)[pad:pad + CONV_ROWS]
        up = pltpu.roll(win, CONV_ROWS + 2 * pad - 1, axis=0)[pad:pad + CONV_ROWS]
        row = lax.broadcasted_iota(jnp.int32, (CONV_ROWS, 1), 0)
        if r0 > 0 and r0 % seq_len == 0:
            down = jnp.where(row == 0, 0.0, down)
        if r0 + CONV_ROWS < tm and (r0 + CONV_ROWS) % seq_len == 0:
            up = jnp.where(row == CONV_ROWS - 1, 0.0, up)
        return (down * cp_ref[p0:p0 + 1, cols] + mid * cp_ref[p0 + 1:p0 + 2, cols]
                + up * cp_ref[p0 + 2:p0 + 3, cols] + cp_ref[p0 + 3:p0 + 4, cols])

    def conv_gate(parity, r0, c0):
        ua_ref, ug_ref = u_refs[parity]
        cols = slice(c0, c0 + LANES)
        a = conv(ua_ref, r0, cols, 0)
        g = conv(ug_ref, r0, cols, CONV_TAPS + 1)
        act_refs[parity][r0:r0 + CONV_ROWS, cols] = (g * jax.nn.sigmoid(g) * a).astype(BF16)

    def down_proj(parity, cols):
        return _dot(act_refs[parity][...], wd_ref[:, cols])

    chunks = [(r0, c0) for r0 in range(0, tm, CONV_ROWS) for c0 in range(0, tf, LANES)]
    d_out = wd_ref.shape[1]
    down_groups = [slice(c, c + DOWN_COLS) for c in range(0, d_out, DOWN_COLS)]

    def finish_rows(r0):
        rows = slice(r0, r0 + FINISH_ROWS)
        o_ref[rows, :] = x_ref[rows, :] + gate_ref[...] * _rms(acc_ref[rows, :], gpost_ref[...])
        acc_ref[rows, :] = jnp.zeros((FINISH_ROWS, d_out), F32)

    def run_step(up=None, conv_of=None, down=None, finish=False):
        pieces = []
        if down is not None:
            for cols in down_groups:
                def piece(cols=cols):
                    acc_ref[:, cols] += down_proj(down, cols)
                pieces.append((piece, tf * DOWN_COLS))
        n_down = len(pieces)
        if up is not None:
            d_in = h_ref.shape[1]
            for half in range(2):
                partial_u = []
                for k0 in range(0, d_in, UP_K):
                    def piece(half=half, k0=k0, partial_u=partial_u):
                        part = up_proj_part(half, k0)
                        partial_u[:] = [part if not partial_u else partial_u[0] + part]
                        if k0 + UP_K == d_in:
                            up_proj_store(up, half, partial_u[0])
                    pieces.append((piece, UP_K * tf))
        conv_todo = [functools.partial(conv_gate, conv_of, *c) for c in chunks] if conv_of is not None else []
        fin_todo = [functools.partial(finish_rows, r0) for r0 in range(0, tm, FINISH_ROWS)] if finish else []
        n_conv, n_fin = len(conv_todo), len(fin_todo)
        total = sum(w for _, w in pieces)
        total_up = sum(w for _, w in pieces[n_down:])
        done = done_up = 0.0
        for n, (piece, w) in enumerate(pieces):
            piece()
            done += w
            is_last = n == len(pieces) - 1
            conv_upto = n_conv if is_last else round(n_conv * done / total)
            while n_conv - len(conv_todo) < conv_upto:
                conv_todo.pop(0)()
            if n >= n_down:
                done_up += w
                fin_upto = n_fin if is_last else round(n_fin * done_up / total_up)
                while n_fin - len(fin_todo) < fin_upto:
                    fin_todo.pop(0)()
        for chunk in conv_todo + fin_todo:
            chunk()

    s = i * nj + j
    last = n_row_tiles * nj
    finishing = (j == 1) & (i >= 1)

    @pl.when(s == 0)
    def _():
        acc_ref[...] = jnp.zeros_like(acc_ref)
        run_step(up=0)

    @pl.when(s == 1)
    def _():
        run_step(up=1, conv_of=0)

    for parity in range(2):
        steady = (s >= 2) & (s < last) & (s % 2 == parity)

        @pl.when(steady & jnp.logical_not(finishing))
        def _():
            run_step(up=parity, conv_of=1 - parity, down=parity)

        @pl.when(steady & finishing)
        def _():
            run_step(up=parity, conv_of=1 - parity, down=parity, finish=True)

    @pl.when(s == last)
    def _():
        run_step(conv_of=(last - 1) % 2, down=last % 2)

    @pl.when(s == last + 1)
    def _():
        run_step(down=(last + 1) % 2, finish=True)


def _ffn(h, w_up, conv_w, conv_b, w_down, x, g_post, mods, row_fn, seq_len, tm):
    t, d = x.shape
    ff = w_down[0].shape[1]
    tf = FF_TILE
    nj = ff // tf
    n_tiles = t // tm
    assert seq_len % tm == 0 or (tm % seq_len == 0 and seq_len % CONV_ROWS == 0)

    tile0 = np.arange(n_tiles) * tm
    start = tile0 % seq_len == 0
    end = (tile0 + tm) % seq_len == 0
    prev = jnp.take(h, jnp.asarray(np.maximum(tile0 - 1, 0)), axis=0)
    nxt = jnp.take(h, jnp.asarray(np.minimum(tile0 + tm, t - 1)), axis=0)
    prev = jnp.where(jnp.asarray(start)[:, None], jnp.zeros_like(prev), prev)
    nxt = jnp.where(jnp.asarray(end)[:, None], jnp.zeros_like(nxt), nxt)
    edge_rows = jnp.concatenate([prev, nxt], axis=0)
    pad = -edge_rows.shape[0] % BF16_SUBLANES
    edge_rows = jnp.pad(edge_rows, ((0, pad), (0, 0)))
    u_edge = _matmul(edge_rows, w_up, 2 * ff // EDGE_STEPS, F32, "conv_ffn_edges")
    u_prev, u_next = u_edge[:n_tiles], u_edge[n_tiles:2 * n_tiles]
    edges = jnp.stack([u_prev[:, :ff], u_prev[:, ff:], u_next[:, :ff], u_next[:, ff:]], axis=1)
    edges = edges.reshape(n_tiles, 4, nj, tf).transpose(0, 2, 1, 3)
    conv_p = jnp.concatenate([conv_w[:, :ff], conv_b[None, :ff], conv_w[:, ff:], conv_b[None, ff:]], axis=0)
    conv_p = conv_p.reshape(2 * (CONV_TAPS + 1), nj, tf).transpose(1, 0, 2)

    assert nj >= 3
    vec = pl.BlockSpec((1, d), lambda i, j: (0, 0))
    cur = lambda i: jnp.minimum(i, n_tiles - 1)
    fin = lambda i, j, switch: jnp.where(j <= switch, jnp.maximum(i - 1, 0), cur(i))
    lag = lambda j, k: jnp.where(j >= k, j - k, j + nj - k)
    up_cols = lambda off: _wspec(w_up, (d, tf), lambda i, j: (0, off + j))
    u_buf = pltpu.VMEM((tm + 2 * F32_SUBLANES, tf), F32)
    act_buf = pltpu.VMEM((tm, tf), BF16)
    return pl.pallas_call(
        functools.partial(_ffn_kernel, tm=tm, n_ff_tiles=nj, n_row_tiles=n_tiles, seq_len=seq_len),
        grid=(n_tiles + 1, nj),
        in_specs=[
            pl.BlockSpec((tm, d), lambda i, j: (cur(i), 0)),
            up_cols(0), up_cols(nj),
            pl.BlockSpec((None, None, 4, tf), lambda i, j: (cur(i), j, 0, 0)),
            pl.BlockSpec((None, 2 * (CONV_TAPS + 1), tf), lambda i, j: (lag(j, 1), 0, 0)),
            _wspec(w_down, (tf, d), lambda i, j: (lag(j, 2), 0)),
            pl.BlockSpec((tm, d), lambda i, j: (fin(i, j, 1), 0)),
            pl.BlockSpec((None, None, 1, d), lambda i, j: (row_fn(fin(i, j, 1)), 5, 0, 0)),
            vec,
        ],
        out_specs=pl.BlockSpec((tm, d), lambda i, j: (fin(i, j, 3), 0)),
        out_shape=jax.ShapeDtypeStruct((t, d), F32),
        scratch_shapes=[pltpu.VMEM((tm, d), F32), u_buf, u_buf, u_buf, u_buf, act_buf, act_buf],
        compiler_params=_params("arbitrary", "arbitrary"),
        name="conv_ffn",
    )(h, w_up[0], w_up[0], edges, conv_p, w_down[0], x, mods,
      g_post.reshape(1, d))


def kernel(x, c, ctx, c_ctx, w_ada, b_ada, g_pre_mix, w_in, rpb, w_four, w_out, g_post_mix,
           g_pre_ffn, w_up, conv_w, conv_b, w_down, g_post_ffn):
    b, n, d = x.shape
    n_ctx = ctx.shape[1]
    assert (d, n, n_ctx, b) == (D_MODEL, GRID_W * GRID_W, CTX_LEN, CTX_MOD_ROW)
    t_lat, t_ctx = b * n, b * n_ctx
    tm_lat = tm_ctx = ROW_TILE
    tiles_per_seq = n // tm_lat
    assert tiles_per_seq & (tiles_per_seq - 1) == 0
    lat_row = lambda i: i >> (tiles_per_seq.bit_length() - 1)
    ctx_row = lambda i: CTX_MOD_ROW

    cond = jnp.zeros((MOD_ROWS, d), F32).at[:b].set(c).at[CTX_MOD_ROW].set(c_ctx)
    mods_all = _mods(cond, w_ada, b_ada)

    scale = (n * FOURIER_GROUP_DIM) ** -0.5
    scale_ctx = (n_ctx * FOURIER_GROUP_DIM) ** -0.5
    kk = np.outer(np.arange(FOURIER_GROUP_DIM), np.arange(FOURIER_GROUP_DIM)) % FOURIER_GROUP_DIM
    ang = 2.0 * np.pi * kk / FOURIER_GROUP_DIM
    cc, sc = np.cos(ang), -np.sin(ang)
    mw_lat = _four_w(jnp.asarray(cc * scale, F32), jnp.asarray(sc * scale, F32), w_four)
    dft_lat = _dft_tables(n)

    w_in_b, w_out_b, w_up_b, w_down_b = (w.astype(BF16) for w in (w_in, w_out, w_up, w_down))
    xt = x.reshape(t_lat, d)
    ct = ctx.reshape(t_ctx, d)
    for l in range(DEPTH):
        last = l == DEPTH - 1
        mods = mods_all[l].reshape(MOD_ROWS, N_MOD, 1, d)
        w_in_l, w_out_l, w_up_l, w_down_l = ((w, l) for w in (w_in_b, w_out_b, w_up_b, w_down_b))

        proj = _inproj(xt, g_pre_mix[l], mods, lat_row, w_in_l, tm_lat)
        if last:
            w_kv = (w_in_b[l:l + 1, :, NA_WIDTH:3 * NA_WIDTH], 0)
            cproj = _inproj(ct, g_pre_mix[l], mods, ctx_row, w_kv, tm_ctx)
            kc_blk, vc_blk = 0, N_PAIRS
        else:
            cproj = _inproj(ct, g_pre_mix[l], mods, ctx_row, w_in_l, tm_ctx)
            kc_blk, vc_blk = N_PAIRS, 2 * N_PAIRS
        cproj3 = cproj.reshape(b, n_ctx, cproj.shape[1])

        attn = _attn(proj.reshape(b, n, PROJ_WIDTH), cproj3, _attn_bias_slabs(rpb[l]), kc_blk, vc_blk)
        four = _four_pos(dft_lat, _four_chan(proj, mw_lat[l], CHAN_TILE), b, n // DFT_RADIX, DFT_COLS)
        xt, h2 = _outproj(attn.reshape(t_lat, NA_WIDTH), four.reshape(t_lat, FOURIER_WIDTH), w_out_l, xt,
                          g_post_mix[l], g_pre_ffn[l], mods, lat_row, tm_lat)
        xt = _ffn(h2, w_up_l, conv_w[l], conv_b[l], w_down_l, xt, g_post_ffn[l], mods, lat_row, n, tm_lat)

        if not last:
            mw_ctx = _four_w(jnp.asarray(cc * scale_ctx, F32), jnp.asarray(sc * scale_ctx, F32),
                             w_four[l:l + 1])[0]
            dft_ctx = _dft_tables(n_ctx)
            attn_c = _ctx_attn(cproj3)
            four_c = _four_pos(dft_ctx, _four_chan(cproj, mw_ctx, ROW_TILE), b, n_ctx // DFT_RADIX, DFT_COLS)
            ct, hc2 = _outproj(attn_c.reshape(t_ctx, NA_WIDTH), four_c.reshape(t_ctx, FOURIER_WIDTH),
                               w_out_l, ct, g_post_mix[l], g_pre_ffn[l], mods, ctx_row, tm_ctx)
            ct = _ffn(hc2, w_up_l, conv_w[l], conv_b[l], w_down_l, ct, g_post_ffn[l], mods, ctx_row,
                      n_ctx, tm_ctx)
    return xt.reshape(b, n, d)
```

```python
import functools

import jax
import jax.numpy as jnp
import numpy as np
from jax import lax
from jax.experimental import pallas as pl
from jax.experimental.pallas import tpu as pltpu

D_MODEL = 2048
DEPTH = 2
GRID_W = 64
CTX_LEN = 256
NA_HEADS = 16
HEAD_DIM = 64
NA_WIDTH = NA_HEADS * HEAD_DIM
FOURIER_GROUPS = 8
FOURIER_GROUP_DIM = 128
FOURIER_WIDTH = FOURIER_GROUPS * FOURIER_GROUP_DIM
PROJ_WIDTH = 3 * NA_WIDTH + FOURIER_WIDTH
WIN_ROWS = 8
WIN_COLS = 16
D_FF = 5632
N_MOD = 6
EPS = 1e-6
ATTN_SCALE = HEAD_DIM ** -0.5
MOD_ROWS = 16
CTX_MOD_ROW = 8

ROW_TILE = 512
DFT_COLS = 512
MODS_COLS = 1024
EDGE_STEPS = 8
LANES = 128
DFT_RADIX = 4
DFT_ROWS = 512
FF_TILE = 512
CONV_TAPS = 3
FINISH_ROWS = 64
UP_K = 256
DOWN_COLS = 512
OUT_COLS = 512
OUT_ROWS = 64
PROJ_COLS = 512
NORM_ROWS = 16
CONV_ROWS = 64
F32_SUBLANES = 8
BF16_SUBLANES = 16
HEAD_PAIR = LANES // HEAD_DIM
N_PAIRS = NA_HEADS // HEAD_PAIR
Q_ROWS = 4
K_ROWS = Q_ROWS + WIN_ROWS - 1
TQ = Q_ROWS * GRID_W
NK = K_ROWS * GRID_W
ATTN_UNROLL = 16
NEG = -1e30
VMEM_LIMIT = 56 * 1024 * 1024

F32 = jnp.float32
BF16 = jnp.bfloat16


def _params(*sem, flags=None):
    return pltpu.CompilerParams(dimension_semantics=sem, vmem_limit_bytes=VMEM_LIMIT, flags=flags)


def _dot(a, b):
    return jnp.dot(a, b, preferred_element_type=F32)


def _dot_nt(a, b):
    return lax.dot_general(a, b, (((1,), (1,)), ((), ())), preferred_element_type=F32)


def _rms(x, g):
    ms = jnp.mean(x * x, axis=-1, keepdims=True)
    return x * lax.rsqrt(ms + EPS) * g


def _norm_mod(x, g, shift, scale):
    return _rms(x, g) * (1.0 + scale) + shift


def _mods_kernel(cond_ref, w_ref, b_ref, o_ref):
    c = cond_ref[...]
    s = (c * jax.nn.sigmoid(c)).astype(BF16)
    o_ref[...] = _dot(s, w_ref[...].astype(BF16)) + b_ref[...]


def _mods(cond, w_ada, b_ada):
    depth, d, n = w_ada.shape
    tn = MODS_COLS
    return pl.pallas_call(
        _mods_kernel,
        grid=(depth, n // tn),
        in_specs=[
            pl.BlockSpec((MOD_ROWS, d), lambda l, j: (0, 0)),
            pl.BlockSpec((None, d, tn), lambda l, j: (l, 0, j)),
            pl.BlockSpec((None, 1, tn), lambda l, j: (l, 0, j)),
        ],
        out_specs=pl.BlockSpec((None, MOD_ROWS, tn), lambda l, j: (l, 0, j)),
        out_shape=jax.ShapeDtypeStruct((depth, MOD_ROWS, n), F32),
        compiler_params=_params("parallel", "parallel"),
        name="adaln_mods",
    )(cond, w_ada, b_ada.reshape(depth, 1, n))


def _wspec(w, block, index_fn, **kw):
    _, layer = w
    return pl.BlockSpec((None,) + tuple(block), lambda *idx: (layer,) + tuple(index_fn(*idx)), **kw)


def _mod_spec(which, row_fn):
    return pl.BlockSpec((None, None, 1, D_MODEL),
                        lambda i, *_: (row_fn(i), which, 0, 0))


def _interleave(pieces, chunks):
    if not pieces:
        for chunk in chunks:
            chunk()
        return
    slots = max(len(pieces) - 1, 1)
    per = -(-len(chunks) // slots)
    for n, piece in enumerate(pieces):
        piece()
        for chunk in chunks[n * per:(n + 1) * per]:
            chunk()


def _order_before_reads(ref, value):
    tile = (slice(0, BF16_SUBLANES), slice(0, LANES))
    bits = pltpu.bitcast(value[tile], jnp.uint32)
    zero = pltpu.bitcast((bits >> 16) >> 16, F32)
    ref[tile] = (ref[tile].astype(F32) + zero).astype(ref.dtype)


def _two_stage(i, n_tiles, run):
    @pl.when(i == 0)
    def _():
        run(0, None)

    for parity in range(2):
        @pl.when((i >= 1) & (i < n_tiles) & (i % 2 == parity))
        def _():
            run(parity, 1 - parity)

    @pl.when(i == n_tiles)
    def _():
        run(None, (n_tiles - 1) % 2)


def _inproj_kernel(x_ref, g_ref, sh_ref, sc_ref, w_ref, *refs, n_tiles, fourier):
    if fourier:
        m_ref, o_ref = refs[0], refs[1]
        y_refs = refs[2:2 + 2 * DFT_RADIX]
        h0_ref, h1_ref, yc_scr, ys_scr = refs[2 + 2 * DFT_RADIX:]
    else:
        o_ref, h0_ref, h1_ref = refs
    h_refs = (h0_ref, h1_ref)
    tm = x_ref.shape[0]
    n_plain = o_ref.shape[1]
    c = FOURIER_GROUP_DIM

    def fourier_piece(cols):
        f = _dot(h_refs_dot[0][...], w_ref[:, cols]).astype(BF16)
        for k in range(PROJ_COLS // c):
            grp = (cols.start - n_plain) // c + k
            out_cols = slice(grp * c, (grp + 1) * c)
            y = _dot(f[:, k * c:(k + 1) * c], m_ref[grp])
            yc_scr[...] = y[:, :c]
            ys_scr[...] = y[:, c:]
            for p in range(DFT_RADIX):
                rows = pl.ds(p, tm // DFT_RADIX, stride=DFT_RADIX)
                y_refs[2 * p][:, out_cols] = yc_scr[rows, :].astype(BF16)
                y_refs[2 * p + 1][:, out_cols] = ys_scr[rows, :].astype(BF16)

    h_refs_dot = [None]

    def run(norm_to, dot_from):
        pieces, chunks = [], []
        if dot_from is not None:
            h_refs_dot[0] = h_refs[dot_from]
            for c0 in range(0, w_ref.shape[1], PROJ_COLS):
                def piece(cols=slice(c0, c0 + PROJ_COLS)):
                    if cols.start >= n_plain:
                        fourier_piece(cols)
                    else:
                        o_ref[:, cols] = _dot(h_refs[dot_from][...], w_ref[:, cols]).astype(o_ref.dtype)
                pieces.append(piece)
        if norm_to is not None:
            for r0 in range(0, tm, NORM_ROWS):
                def chunk(rows=slice(r0, r0 + NORM_ROWS)):
                    val = _norm_mod(x_ref[rows, :], g_ref[...], sh_ref[...], sc_ref[...])
                    h_refs[norm_to][rows, :] = val.astype(BF16)
                    if dot_from is not None:
                        _order_before_reads(h_refs[dot_from], val)
                chunks.append(chunk)
        _interleave(pieces, chunks)

    _two_stage(pl.program_id(0), n_tiles, run)


def _inproj(x, g, mods, row_fn, w, tm, four_m=None):
    t, d = x.shape
    n = w[0].shape[2]
    n_tiles = t // tm
    fourier = four_m is not None
    n_plain = n - FOURIER_WIDTH if fourier else n
    cur = lambda i: jnp.minimum(i, n_tiles - 1)
    prev = lambda i: jnp.maximum(i - 1, 0)
    in_specs = [
        pl.BlockSpec((tm, d), lambda i: (cur(i), 0)),
        pl.BlockSpec((1, d), lambda i: (0, 0)),
        _mod_spec(0, lambda i: row_fn(cur(i))),
        _mod_spec(1, lambda i: row_fn(cur(i))),
        _wspec(w, (d, n), lambda i: (0, 0), pipeline_mode=pl.Buffered(1)),
    ]
    operands = [x, g.reshape(1, d), mods, mods, w[0]]
    out_specs = [pl.BlockSpec((tm, n_plain), lambda i: (prev(i), 0))]
    out_shape = [jax.ShapeDtypeStruct((t, n_plain), BF16)]
    scratch = [pltpu.VMEM((tm, d), BF16)] * 2
    if fourier:
        in_specs.append(pl.BlockSpec(four_m.shape, lambda i: (0, 0, 0)))
        operands.append(four_m)
        out_specs += [pl.BlockSpec((tm // DFT_RADIX, FOURIER_WIDTH), lambda i: (prev(i), 0))] * (2 * DFT_RADIX)
        out_shape += [jax.ShapeDtypeStruct((t // DFT_RADIX, FOURIER_WIDTH), BF16)] * (2 * DFT_RADIX)
        scratch += [pltpu.VMEM((tm, FOURIER_GROUP_DIM), F32)] * 2
    outs = pl.pallas_call(
        functools.partial(_inproj_kernel, n_tiles=n_tiles, fourier=fourier),
        grid=(n_tiles + 1,),
        in_specs=in_specs,
        out_specs=out_specs,
        out_shape=out_shape,
        scratch_shapes=scratch,
        compiler_params=_params("arbitrary"),
        name="norm_inproj",
    )(*operands)
    return (outs[0], tuple(outs[1:])) if fourier else outs[0]


def _softmax_pv(parts):
    m = functools.reduce(jnp.maximum, [s.max(axis=-1, keepdims=True) for s, _ in parts])
    num = 0.0
    den = 0.0
    for s, v in parts:
        p = jnp.exp(s - m)
        den = den + p.sum(axis=-1, keepdims=True)
        num = num + _dot(p.astype(BF16), v)
    return num / den


def _head_masks():
    lane = lax.broadcasted_iota(jnp.int32, (1, LANES), 1)
    first = lane < HEAD_DIM
    return first, [first, jnp.logical_not(first)]


def _attn_kernel(q_ref, k_ref, v_ref, kc_ref, vc_ref, slab_ref, o_ref, tab_ref, *, n_tiles, rows):
    first, sels = _head_masks()
    kc = kc_ref[...]
    vc = vc_ref[...]

    @pl.when(pl.program_id(1) == 0)
    def _():
        pattern = _attn_slab_pattern(rows)
        for p in range(pattern.shape[0]):
            for hh in range(HEAD_PAIR):
                for qr in range(Q_ROWS):
                    for kr in range(K_ROWS):
                        tab_ref[p, hh, qr * GRID_W:(qr + 1) * GRID_W, kr * GRID_W:(kr + 1) * GRID_W] = (
                            slab_ref[hh, int(pattern[p, qr, kr])])

    def scores(t, hh):
        q0 = pl.multiple_of(t * TQ, TQ)
        k0 = pl.multiple_of(jnp.clip(t * Q_ROWS - WIN_ROWS // 2, 0, rows - K_ROWS) * GRID_W, GRID_W)
        pat = jnp.where(t == 0, 0, jnp.where(t == n_tiles - 1, 2, 1))
        q = q_ref[pl.ds(q0, TQ), :] * ATTN_SCALE
        qm = jnp.where(sels[hh], q, jnp.zeros_like(q))
        s_loc = _dot_nt(qm, k_ref[pl.ds(k0, NK), :]) + tab_ref[pat, hh]
        s_ctx = _dot_nt(qm, kc)
        return k0, s_loc, s_ctx

    def tile_group(tg, carry):
        units = [(tg * ATTN_UNROLL + ti, hh) for ti in range(ATTN_UNROLL) for hh in range(HEAD_PAIR)]
        pending = None
        outs = {}
        for unit in units + [None]:
            nxt = scores(*unit) if unit is not None else None
            if pending is not None:
                (t, hh), (k0, s_loc, s_ctx) = pending
                outs[hh] = _softmax_pv([(s_loc, v_ref[pl.ds(k0, NK), :]), (s_ctx, vc)])
                if hh == HEAD_PAIR - 1:
                    q0 = pl.multiple_of(t * TQ, TQ)
                    o_ref[pl.ds(q0, TQ), :] = jnp.where(first, outs[0], outs[1]).astype(o_ref.dtype)
            pending = (unit, nxt)
        return carry

    lax.fori_loop(0, n_tiles // ATTN_UNROLL, tile_group, 0)


def _attn(proj, cproj, slabs, kc_blk, vc_blk):
    b, n, _ = proj.shape
    rows = n // GRID_W
    n_tiles = rows // Q_ROWS
    lat = lambda off: pl.BlockSpec((None, n, LANES), lambda p, i: (i, 0, off + p))
    cx = lambda off: pl.BlockSpec((None, CTX_LEN, LANES), lambda p, i: (i, 0, off + p))
    return pl.pallas_call(
        functools.partial(_attn_kernel, n_tiles=n_tiles, rows=rows),
        grid=(N_PAIRS, b),
        in_specs=[
            lat(0), lat(N_PAIRS), lat(2 * N_PAIRS), cx(kc_blk), cx(vc_blk),
            pl.BlockSpec((HEAD_PAIR,) + slabs.shape[1:], lambda p, i: (p, 0, 0, 0)),
        ],
        out_specs=pl.BlockSpec((None, n, LANES), lambda p, i: (i, 0, p)),
        out_shape=jax.ShapeDtypeStruct((b, n, NA_WIDTH), BF16),
        scratch_shapes=[pltpu.VMEM((3, HEAD_PAIR, TQ, NK), F32)],
        compiler_params=_params("arbitrary", "arbitrary"),
        name="nbr_attn",
    )(proj, proj, proj, cproj, cproj, slabs)


def _ctx_attn_kernel(q_ref, k_ref, v_ref, o_ref):
    first, sels = _head_masks()
    q = q_ref[...] * ATTN_SCALE
    k = k_ref[...]
    v = v_ref[...]
    outs = []
    for hh in range(HEAD_PAIR):
        qm = jnp.where(sels[hh], q, jnp.zeros_like(q))
        outs.append(_softmax_pv([(_dot_nt(qm, k), v)]))
    o_ref[...] = jnp.where(first, outs[0], outs[1]).astype(o_ref.dtype)


def _ctx_attn(cproj):
    b, n, _ = cproj.shape
    spec = lambda off: pl.BlockSpec((None, n, LANES), lambda i, p: (i, 0, off + p))
    return pl.pallas_call(
        _ctx_attn_kernel,
        grid=(b, N_PAIRS),
        in_specs=[spec(0), spec(N_PAIRS), spec(2 * N_PAIRS)],
        out_specs=spec(0),
        out_shape=jax.ShapeDtypeStruct((b, n, NA_WIDTH), BF16),
        compiler_params=_params("parallel", "parallel"),
        name="ctx_attn",
    )(cproj, cproj, cproj)


MASKED_SLAB = 2 * WIN_ROWS - 1


def _attn_bias_slabs(rpb):
    qc = np.arange(GRID_W)[:, None]
    kc = np.arange(GRID_W)[None, :]
    cs = np.clip(qc - WIN_COLS // 2, 0, GRID_W - WIN_COLS)
    col_ok = (kc >= cs) & (kc < cs + WIN_COLS)
    dc = kc - qc + WIN_COLS - 1
    onehot = np.zeros((2 * WIN_COLS - 1, GRID_W, GRID_W), np.float32)
    qi, ki = np.nonzero(col_ok)
    onehot[dc[qi, ki], qi, ki] = 1.0
    t1 = jnp.einsum('hrd,dqk->hrqk', rpb, jnp.asarray(onehot), precision=lax.Precision.HIGHEST)
    t1 = jnp.where(jnp.asarray(col_ok)[None, None], t1, NEG)
    neg_slab = jnp.full((NA_HEADS, 1, GRID_W, GRID_W), NEG, F32)
    return jnp.concatenate([t1, neg_slab], axis=1)


def _attn_slab_pattern(rows):
    n_tiles = rows // Q_ROWS
    idx = np.full((3, Q_ROWS, K_ROWS), MASKED_SLAB, np.int32)
    for p, t in enumerate((0, 1, n_tiles - 1)):
        k_start = int(np.clip(t * Q_ROWS - WIN_ROWS // 2, 0, rows - K_ROWS))
        for qr in range(Q_ROWS):
            r = t * Q_ROWS + qr
            rs = int(np.clip(r - WIN_ROWS // 2, 0, rows - WIN_ROWS))
            for kr in range(K_ROWS):
                krow = k_start + kr
                if rs <= krow < rs + WIN_ROWS:
                    idx[p, qr, kr] = krow - r + WIN_ROWS - 1
    return idx


def _split_bf16(a):
    hi = a.astype(BF16)
    lo = (a - hi.astype(F32)).astype(BF16)
    return hi, lo


def _dot3(a, b):
    ah, al = _split_bf16(a)
    bh, bl = _split_bf16(b)
    return _dot(ah, bh) + (_dot(ah, bl) + _dot(al, bh))


def _four_w_kernel(cc_ref, sc_ref, w_ref, o_ref):
    w = w_ref[...]
    o_ref[:, :FOURIER_GROUP_DIM] = _dot3(cc_ref[...], w).astype(BF16)
    o_ref[:, FOURIER_GROUP_DIM:] = _dot3(sc_ref[...], w).astype(BF16)


def _four_w(cc, sc, w_four):
    depth, g, c, _ = w_four.shape
    tab = pl.BlockSpec((c, c), lambda l, i: (0, 0))
    return pl.pallas_call(
        _four_w_kernel,
        grid=(depth, g),
        in_specs=[tab, tab, pl.BlockSpec((None, None, c, c), lambda l, i: (l, i, 0, 0))],
        out_specs=pl.BlockSpec((None, None, c, 2 * c), lambda l, i: (l, i, 0, 0)),
        out_shape=jax.ShapeDtypeStruct((depth, g, c, 2 * c), BF16),
        compiler_params=_params("parallel", "parallel"),
        name="fourier_weights",
    )(cc, sc, w_four)


def _four_pos_kernel(c0, s0, c1, s1, c2, s2, c3, s3, ns1, nc3, a0, b0, a1, b1, a2, b2, a3, b3, o_ref):
    step = min(DFT_ROWS, c0.shape[0])
    for r0 in range(0, c0.shape[0], step):
        rows = slice(r0, r0 + step)
        dot = lambda table, data: _dot(table[rows, :], data[...])
        re0 = dot(c0, a0) + dot(s0, b0)
        re2 = dot(c2, a2) + dot(s2, b2)
        re13 = dot(c1, a1) + dot(s1, b1) + dot(c3, a3) + dot(s3, b3)
        im13 = dot(c1, b1) + dot(ns1, a1) + dot(nc3, b3) + dot(s3, a3)
        plus, minus = re0 + re2, re0 - re2
        o_ref[0, rows, :] = (plus + re13).astype(o_ref.dtype)
        o_ref[1, rows, :] = (minus + im13).astype(o_ref.dtype)
        o_ref[2, rows, :] = (plus - re13).astype(o_ref.dtype)
        o_ref[3, rows, :] = (minus - im13).astype(o_ref.dtype)


def _four_pos(tables, y_parts, b, tm, tn):
    data = [y.reshape(b, y.shape[0] // b, y.shape[1]) for y in y_parts]
    _, part, w = data[0].shape
    mat = pl.BlockSpec((tm, part), lambda i, bb, j: (i, 0), pipeline_mode=pl.Buffered(1))
    dat = pl.BlockSpec((None, part, tn), lambda i, bb, j: (bb, 0, j))
    out = pl.pallas_call(
        _four_pos_kernel,
        grid=(part // tm, b, w // tn),
        in_specs=[mat] * len(tables) + [dat] * len(data),
        out_specs=pl.BlockSpec((None, DFT_RADIX, tm, tn), lambda i, bb, j: (bb, 0, i, j)),
        out_shape=jax.ShapeDtypeStruct((b, DFT_RADIX, part, w), BF16),
        compiler_params=_params("parallel", "parallel", "parallel"),
        name="fourier_pos",
    )(*tables, *data)
    return out.reshape(b, DFT_RADIX * part, w)


def _dft_tables(n):
    part = n // DFT_RADIX
    split = 2 ** (part.bit_length() // 2)
    j = jnp.arange(part, dtype=jnp.int32)[:, None]
    hi = jnp.arange(part // split, dtype=jnp.int32)[None, :] * split
    lo = jnp.arange(split, dtype=jnp.int32)[None, :]
    ang = lambda k: (2.0 * np.pi / part) * ((j * k) % part).astype(F32)
    ch, sh, cl, sl = jnp.cos(ang(hi)), jnp.sin(ang(hi)), jnp.cos(ang(lo)), jnp.sin(ang(lo))
    c0 = (ch[:, :, None] * cl[:, None, :] - sh[:, :, None] * sl[:, None, :]).reshape(part, part)
    s0 = (sh[:, :, None] * cl[:, None, :] + ch[:, :, None] * sl[:, None, :]).reshape(part, part)
    tables = [c0, s0]
    for p in range(1, DFT_RADIX):
        phi = (2.0 * np.pi * p / n) * j.astype(F32)
        cp, sp = jnp.cos(phi), jnp.sin(phi)
        tables += [c0 * cp - s0 * sp, s0 * cp + c0 * sp]
    tables += [-tables[3], -tables[6]]
    return tuple(t.astype(BF16) for t in tables)


def _outproj_kernel(a_ref, f_ref, w_ref, x_ref, gpost_ref, gate_ref, gpre_ref, sh_ref, sc_ref,
                    xo_ref, h_ref, m0_ref, m1_ref, lhs_ref, *, n_tiles):
    mix_refs = (m0_ref, m1_ref)
    tm = x_ref.shape[0]

    def run(mix_to, epi_from):
        pieces, chunks = [], []
        if mix_to is not None:
            lhs_ref[:, :NA_WIDTH] = a_ref[...]
            lhs_ref[:, NA_WIDTH:] = f_ref[...]
            for c0 in range(0, w_ref.shape[1], OUT_COLS):
                def piece(cols=slice(c0, c0 + OUT_COLS)):
                    mix_refs[mix_to][:, cols] = _dot(lhs_ref[...], w_ref[:, cols])
                pieces.append(piece)
        if epi_from is not None:
            for r0 in range(0, tm, OUT_ROWS):
                def chunk(rows=slice(r0, r0 + OUT_ROWS)):
                    x = x_ref[rows, :] + gate_ref[...] * _rms(mix_refs[epi_from][rows, :], gpost_ref[...])
                    xo_ref[rows, :] = x
                    h = _norm_mod(x, gpre_ref[...], sh_ref[...], sc_ref[...])
                    h_ref[rows, :] = h.astype(BF16)
                    if mix_to is not None:
                        _order_before_reads(lhs_ref, h)
                chunks.append(chunk)
        _interleave(pieces, chunks)

    _two_stage(pl.program_id(0), n_tiles, run)


def _outproj(attn, four, w, x, g_post, g_pre, mods, row_fn, tm):
    t, d = x.shape
    n_tiles = t // tm
    cur = lambda i: jnp.minimum(i, n_tiles - 1)
    prev = lambda i: jnp.maximum(i - 1, 0)
    vec = pl.BlockSpec((1, d), lambda i: (0, 0))
    row = pl.BlockSpec((tm, d), lambda i: (prev(i), 0))
    half = pl.BlockSpec((tm, NA_WIDTH), lambda i: (cur(i), 0))
    mod = lambda which: _mod_spec(which, lambda i: row_fn(prev(i)))
    return pl.pallas_call(
        functools.partial(_outproj_kernel, n_tiles=n_tiles),
        grid=(n_tiles + 1,),
        in_specs=[half, half, _wspec(w, w[0].shape[1:], lambda i: (0, 0), pipeline_mode=pl.Buffered(1)), row, vec,
                  mod(2), vec, mod(3), mod(4)],
        out_specs=[row, row],
        out_shape=[jax.ShapeDtypeStruct((t, d), F32), jax.ShapeDtypeStruct((t, d), BF16)],
        scratch_shapes=[pltpu.VMEM((tm, d), F32), pltpu.VMEM((tm, d), F32),
                        pltpu.VMEM((tm, NA_WIDTH + FOURIER_WIDTH), BF16)],
        compiler_params=_params("arbitrary"),
        name="outproj_residual",
    )(attn, four, w[0], x, g_post.reshape(1, d), mods, g_pre.reshape(1, d), mods, mods)


def _matmul_kernel(x_ref, w_ref, o_ref):
    o_ref[...] = _dot(x_ref[...], w_ref[...]).astype(o_ref.dtype)


def _matmul(x, w, tn, out_dtype, name):
    m, k = x.shape
    n = w[0].shape[2]
    return pl.pallas_call(
        _matmul_kernel,
        grid=(n // tn,),
        in_specs=[pl.BlockSpec((m, k), lambda j: (0, 0)), _wspec(w, (k, tn), lambda j: (0, j))],
        out_specs=pl.BlockSpec((m, tn), lambda j: (0, j)),
        out_shape=jax.ShapeDtypeStruct((m, n), out_dtype),
        compiler_params=_params("parallel"),
        name=name,
    )(x, w[0])


def _ffn_kernel(h_ref, wa_ref, wg_ref, edge_ref, cp_ref, wd_ref, x_ref, gate_ref, gpost_ref, o_ref,
                acc_ref, ua0, ug0, ua1, ug1, act0, act1,
                *, tm, n_ff_tiles, n_row_tiles, seq_len):
    i = pl.program_id(0)
    j = pl.program_id(1)
    nj = n_ff_tiles
    pad = F32_SUBLANES
    tf = wa_ref.shape[1]
    u_refs = ((ua0, ug0), (ua1, ug1))
    act_refs = (act0, act1)

    def up_proj_part(half, k0):
        w_ref = (wa_ref, wg_ref)[half]
        return _dot(h_ref[:, k0:k0 + UP_K], w_ref[k0:k0 + UP_K, :])

    def up_proj_store(parity, half, u):
        u_ref = u_refs[parity][half]
        u_ref[0:pad, :] = jnp.broadcast_to(edge_ref[half:half + 1, :], (pad, tf))
        u_ref[pad:pad + tm, :] = u
        u_ref[pad + tm:, :] = jnp.broadcast_to(edge_ref[2 + half:3 + half, :], (pad, tf))

    def conv(u_ref, r0, cols, p0):
        win = u_ref[r0:r0 + CONV_ROWS + 2 * pad, cols]
        mid = win[pad:pad + CONV_ROWS]
        down = pltpu.roll(win, 1, axis=0)[pad:pad + CONV_ROWS]
        up = pltpu.roll(win, CONV_ROWS + 2 * pad - 1, axis=0)[pad:pad + CONV_ROWS]
        row = lax.broadcasted_iota(jnp.int32, (CONV_ROWS, 1), 0)
        if r0 > 0 and r0 % seq_len == 0:
            down = jnp.where(row == 0, 0.0, down)
        if r0 + CONV_ROWS < tm and (r0 + CONV_ROWS) % seq_len == 0:
            up = jnp.where(row == CONV_ROWS - 1, 0.0, up)
        return (down * cp_ref[p0:p0 + 1, cols] + mid * cp_ref[p0 + 1:p0 + 2, cols]
                + up * cp_ref[p0 + 2:p0 + 3, cols] + cp_ref[p0 + 3:p0 + 4, cols])

    def conv_gate(parity, r0, c0):
        ua_ref, ug_ref = u_refs[parity]
        cols = slice(c0, c0 + LANES)
        a = conv(ua_ref, r0, cols, 0)
        g = conv(ug_ref, r0, cols, CONV_TAPS + 1)
        act_refs[parity][r0:r0 + CONV_ROWS, cols] = (g * jax.nn.sigmoid(g) * a).astype(BF16)

    def down_proj(parity, cols):
        return _dot(act_refs[parity][...], wd_ref[:, cols])

    chunks = [(r0, c0) for r0 in range(0, tm, CONV_ROWS) for c0 in range(0, tf, LANES)]
    d_out = wd_ref.shape[1]
    down_groups = [slice(c, c + DOWN_COLS) for c in range(0, d_out, DOWN_COLS)]

    def finish_rows(r0):
        rows = slice(r0, r0 + FINISH_ROWS)
        o_ref[rows, :] = x_ref[rows, :] + gate_ref[...] * _rms(acc_ref[rows, :], gpost_ref[...])
        acc_ref[rows, :] = jnp.zeros((FINISH_ROWS, d_out), F32)

    def run_step(up=None, conv_of=None, down=None, finish=False):
        pieces = []
        if down is not None:
            for cols in down_groups:
                def piece(cols=cols):
                    acc_ref[:, cols] += down_proj(down, cols)
                pieces.append((piece, tf * DOWN_COLS))
        n_down = len(pieces)
        if up is not None:
            d_in = h_ref.shape[1]
            for half in range(2):
                partial_u = []
                for k0 in range(0, d_in, UP_K):
                    def piece(half=half, k0=k0, partial_u=partial_u):
                        part = up_proj_part(half, k0)
                        partial_u[:] = [part if not partial_u else partial_u[0] + part]
                        if k0 + UP_K == d_in:
                            up_proj_store(up, half, partial_u[0])
                    pieces.append((piece, UP_K * tf))
        conv_todo = [functools.partial(conv_gate, conv_of, *c) for c in chunks] if conv_of is not None else []
        fin_todo = [functools.partial(finish_rows, r0) for r0 in range(0, tm, FINISH_ROWS)] if finish else []
        n_conv, n_fin = len(conv_todo), len(fin_todo)
        total = sum(w for _, w in pieces)
        total_up = sum(w for _, w in pieces[n_down:])
        done = done_up = 0.0
        for n, (piece, w) in enumerate(pieces):
            piece()
            done += w
            is_last = n == len(pieces) - 1
            conv_upto = n_conv if is_last else round(n_conv * done / total)
            while n_conv - len(conv_todo) < conv_upto:
                conv_todo.pop(0)()
            if n >= n_down:
                done_up += w
                fin_upto = n_fin if is_last else round(n_fin * done_up / total_up)
                while n_fin - len(fin_todo) < fin_upto:
                    fin_todo.pop(0)()
        for chunk in conv_todo + fin_todo:
            chunk()

    s = i * nj + j
    last = n_row_tiles * nj
    finishing = (j == 1) & (i >= 1)

    @pl.when(s == 0)
    def _():
        acc_ref[...] = jnp.zeros_like(acc_ref)
        run_step(up=0)

    @pl.when(s == 1)
    def _():
        run_step(up=1, conv_of=0)

    for parity in range(2):
        steady = (s >= 2) & (s < last) & (s % 2 == parity)

        @pl.when(steady & jnp.logical_not(finishing))
        def _():
            run_step(up=parity, conv_of=1 - parity, down=parity)

        @pl.when(steady & finishing)
        def _():
            run_step(up=parity, conv_of=1 - parity, down=parity, finish=True)

    @pl.when(s == last)
    def _():
        run_step(conv_of=(last - 1) % 2, down=last % 2)

    @pl.when(s == last + 1)
    def _():
        run_step(down=(last + 1) % 2, finish=True)


def _ffn(h, w_up, conv_w, conv_b, w_down, x, g_post, mods, row_fn, seq_len, tm):
    t, d = x.shape
    ff = w_down[0].shape[1]
    tf = FF_TILE
    nj = ff // tf
    n_tiles = t // tm
    assert seq_len % tm == 0 or (tm % seq_len == 0 and seq_len % CONV_ROWS == 0)

    tile0 = np.arange(n_tiles) * tm
    start = tile0 % seq_len == 0
    end = (tile0 + tm) % seq_len == 0
    prev = jnp.take(h, jnp.asarray(np.maximum(tile0 - 1, 0)), axis=0)
    nxt = jnp.take(h, jnp.asarray(np.minimum(tile0 + tm, t - 1)), axis=0)
    prev = jnp.where(jnp.asarray(start)[:, None], jnp.zeros_like(prev), prev)
    nxt = jnp.where(jnp.asarray(end)[:, None], jnp.zeros_like(nxt), nxt)
    edge_rows = jnp.concatenate([prev, nxt], axis=0)
    pad = -edge_rows.shape[0] % BF16_SUBLANES
    edge_rows = jnp.pad(edge_rows, ((0, pad), (0, 0)))
    u_edge = _matmul(edge_rows, w_up, 2 * ff // EDGE_STEPS, F32, "conv_ffn_edges")
    u_prev, u_next = u_edge[:n_tiles], u_edge[n_tiles:2 * n_tiles]
    edges = jnp.stack([u_prev[:, :ff], u_prev[:, ff:], u_next[:, :ff], u_next[:, ff:]], axis=1)
    edges = edges.reshape(n_tiles, 4, nj, tf).transpose(0, 2, 1, 3)
    conv_p = jnp.concatenate([conv_w[:, :ff], conv_b[None, :ff], conv_w[:, ff:], conv_b[None, ff:]], axis=0)
    conv_p = conv_p.reshape(2 * (CONV_TAPS + 1), nj, tf).transpose(1, 0, 2)

    assert nj >= 3
    vec = pl.BlockSpec((1, d), lambda i, j: (0, 0))
    cur = lambda i: jnp.minimum(i, n_tiles - 1)
    fin = lambda i, j, switch: jnp.where(j <= switch, jnp.maximum(i - 1, 0), cur(i))
    lag = lambda j, k: jnp.where(j >= k, j - k, j + nj - k)
    up_cols = lambda off: _wspec(w_up, (d, tf), lambda i, j: (0, off + j))
    u_buf = pltpu.VMEM((tm + 2 * F32_SUBLANES, tf), F32)
    act_buf = pltpu.VMEM((tm, tf), BF16)
    return pl.pallas_call(
        functools.partial(_ffn_kernel, tm=tm, n_ff_tiles=nj, n_row_tiles=n_tiles, seq_len=seq_len),
        grid=(n_tiles + 1, nj),
        in_specs=[
            pl.BlockSpec((tm, d), lambda i, j: (cur(i), 0)),
            up_cols(0), up_cols(nj),
            pl.BlockSpec((None, None, 4, tf), lambda i, j: (cur(i), j, 0, 0)),
            pl.BlockSpec((None, 2 * (CONV_TAPS + 1), tf), lambda i, j: (lag(j, 1), 0, 0)),
            _wspec(w_down, (tf, d), lambda i, j: (lag(j, 2), 0)),
            pl.BlockSpec((tm, d), lambda i, j: (fin(i, j, 1), 0)),
            pl.BlockSpec((None, None, 1, d), lambda i, j: (row_fn(fin(i, j, 1)), 5, 0, 0)),
            vec,
        ],
        out_specs=pl.BlockSpec((tm, d), lambda i, j: (fin(i, j, 3), 0)),
        out_shape=jax.ShapeDtypeStruct((t, d), F32),
        scratch_shapes=[pltpu.VMEM((tm, d), F32), u_buf, u_buf, u_buf, u_buf, act_buf, act_buf],
        compiler_params=_params("arbitrary", "arbitrary"),
        name="conv_ffn",
    )(h, w_up[0], w_up[0], edges, conv_p, w_down[0], x, mods,
      g_post.reshape(1, d))


def kernel(x, c, ctx, c_ctx, w_ada, b_ada, g_pre_mix, w_in, rpb, w_four, w_out, g_post_mix,
           g_pre_ffn, w_up, conv_w, conv_b, w_down, g_post_ffn):
    b, n, d = x.shape
    n_ctx = ctx.shape[1]
    assert (d, n, n_ctx, b) == (D_MODEL, GRID_W * GRID_W, CTX_LEN, CTX_MOD_ROW)
    t_lat, t_ctx = b * n, b * n_ctx
    tm_lat = tm_ctx = ROW_TILE
    tiles_per_seq = n // tm_lat
    assert tiles_per_seq & (tiles_per_seq - 1) == 0
    lat_row = lambda i: i >> (tiles_per_seq.bit_length() - 1)
    ctx_row = lambda i: CTX_MOD_ROW

    cond = jnp.zeros((MOD_ROWS, d), F32).at[:b].set(c).at[CTX_MOD_ROW].set(c_ctx)
    mods_all = _mods(cond, w_ada, b_ada)

    scale = (n * FOURIER_GROUP_DIM) ** -0.5
    scale_ctx = (n_ctx * FOURIER_GROUP_DIM) ** -0.5
    kk = np.outer(np.arange(FOURIER_GROUP_DIM), np.arange(FOURIER_GROUP_DIM)) % FOURIER_GROUP_DIM
    ang = 2.0 * np.pi * kk / FOURIER_GROUP_DIM
    cc, sc = np.cos(ang), -np.sin(ang)
    mw_lat = _four_w(jnp.asarray(cc * scale, F32), jnp.asarray(sc * scale, F32), w_four)
    dft_lat = _dft_tables(n)

    w_in_b, w_out_b, w_up_b, w_down_b = (w.astype(BF16) for w in (w_in, w_out, w_up, w_down))
    xt = x.reshape(t_lat, d)
    ct = ctx.reshape(t_ctx, d)
    for l in range(DEPTH):
        last = l == DEPTH - 1
        mods = mods_all[l].reshape(MOD_ROWS, N_MOD, 1, d)
        w_in_l, w_out_l, w_up_l, w_down_l = ((w, l) for w in (w_in_b, w_out_b, w_up_b, w_down_b))

        proj, y_lat = _inproj(xt, g_pre_mix[l], mods, lat_row, w_in_l, tm_lat, four_m=mw_lat[l])
        if last:
            w_kv = (w_in_b[l:l + 1, :, NA_WIDTH:3 * NA_WIDTH], 0)
            cproj = _inproj(ct, g_pre_mix[l], mods, ctx_row, w_kv, tm_ctx)
            kc_blk, vc_blk = 0, N_PAIRS
        else:
            mw_ctx = _four_w(jnp.asarray(cc * scale_ctx, F32), jnp.asarray(sc * scale_ctx, F32),
                             w_four[l:l + 1])[0]
            cproj, y_ctx = _inproj(ct, g_pre_mix[l], mods, ctx_row, w_in_l, tm_ctx, four_m=mw_ctx)
            kc_blk, vc_blk = N_PAIRS, 2 * N_PAIRS
        cproj3 = cproj.reshape(b, n_ctx, cproj.shape[1])

        attn = _attn(proj.reshape(b, n, proj.shape[1]), cproj3, _attn_bias_slabs(rpb[l]), kc_blk, vc_blk)
        four = _four_pos(dft_lat, y_lat, b, n // DFT_RADIX, DFT_COLS)
        xt, h2 = _outproj(attn.reshape(t_lat, NA_WIDTH), four.reshape(t_lat, FOURIER_WIDTH), w_out_l, xt,
                          g_post_mix[l], g_pre_ffn[l], mods, lat_row, tm_lat)
        xt = _ffn(h2, w_up_l, conv_w[l], conv_b[l], w_down_l, xt, g_post_ffn[l], mods, lat_row, n, tm_lat)

        if not last:
            dft_ctx = _dft_tables(n_ctx)
            attn_c = _ctx_attn(cproj3)
            four_c = _four_pos(dft_ctx, y_ctx, b, n_ctx // DFT_RADIX, DFT_COLS)
            ct, hc2 = _outproj(attn_c.reshape(t_ctx, NA_WIDTH), four_c.reshape(t_ctx, FOURIER_WIDTH),
                               w_out_l, ct, g_post_mix[l], g_pre_ffn[l], mods, ctx_row, tm_ctx)
            ct = _ffn(hc2, w_up_l, conv_w[l], conv_b[l], w_down_l, ct, g_post_ffn[l], mods, ctx_row,
                      n_ctx, tm_ctx)
    return xt.reshape(b, n, d)
```

```python
import functools

import jax
import jax.numpy as jnp
import numpy as np
from jax import lax
from jax.experimental import pallas as pl
from jax.experimental.pallas import tpu as pltpu

D_MODEL = 2048
DEPTH = 2
GRID_W = 64
CTX_LEN = 256
NA_HEADS = 16
HEAD_DIM = 64
NA_WIDTH = NA_HEADS * HEAD_DIM
FOURIER_GROUPS = 8
FOURIER_GROUP_DIM = 128
FOURIER_WIDTH = FOURIER_GROUPS * FOURIER_GROUP_DIM
PROJ_WIDTH = 3 * NA_WIDTH + FOURIER_WIDTH
WIN_ROWS = 8
WIN_COLS = 16
D_FF = 5632
N_MOD = 6
EPS = 1e-6
ATTN_SCALE = HEAD_DIM ** -0.5
MOD_ROWS = 16
CTX_MOD_ROW = 8

ROW_TILE = 512
CHAN_TILE = 1024
DFT_COLS = 512
MODS_COLS = 1024
EDGE_STEPS = 8
LANES = 128
DFT_RADIX = 4
DFT_ROWS = 512
FF_TILE = 512
CONV_TAPS = 3
FINISH_ROWS = 64
UP_K = 256
DOWN_COLS = 512
OUT_COLS = 512
OUT_ROWS = 64
PROJ_COLS = 512
NORM_ROWS = 16
CONV_ROWS = 64
F32_SUBLANES = 8
BF16_SUBLANES = 16
HEAD_PAIR = LANES // HEAD_DIM
N_PAIRS = NA_HEADS // HEAD_PAIR
Q_ROWS = 4
K_ROWS = Q_ROWS + WIN_ROWS - 1
TQ = Q_ROWS * GRID_W
NK = K_ROWS * GRID_W
ATTN_UNROLL = 16
NEG = -1e30
VMEM_LIMIT = 56 * 1024 * 1024

F32 = jnp.float32
BF16 = jnp.bfloat16


def _params(*sem, flags=None):
    return pltpu.CompilerParams(dimension_semantics=sem, vmem_limit_bytes=VMEM_LIMIT, flags=flags)


def _dot(a, b):
    return jnp.dot(a, b, preferred_element_type=F32)


def _dot_nt(a, b):
    return lax.dot_general(a, b, (((1,), (1,)), ((), ())), preferred_element_type=F32)


def _rms(x, g):
    ms = jnp.mean(x * x, axis=-1, keepdims=True)
    return x * lax.rsqrt(ms + EPS) * g


def _norm_mod(x, g, shift, scale):
    return _rms(x, g) * (1.0 + scale) + shift


def _mods_kernel(cond_ref, w_ref, b_ref, o_ref):
    c = cond_ref[...]
    s = (c * jax.nn.sigmoid(c)).astype(BF16)
    o_ref[...] = _dot(s, w_ref[...].astype(BF16)) + b_ref[...]


def _mods(cond, w_ada, b_ada):
    depth, d, n = w_ada.shape
    tn = MODS_COLS
    return pl.pallas_call(
        _mods_kernel,
        grid=(depth, n // tn),
        in_specs=[
            pl.BlockSpec((MOD_ROWS, d), lambda l, j: (0, 0)),
            pl.BlockSpec((None, d, tn), lambda l, j: (l, 0, j)),
            pl.BlockSpec((None, 1, tn), lambda l, j: (l, 0, j)),
        ],
        out_specs=pl.BlockSpec((None, MOD_ROWS, tn), lambda l, j: (l, 0, j)),
        out_shape=jax.ShapeDtypeStruct((depth, MOD_ROWS, n), F32),
        compiler_params=_params("parallel", "parallel"),
        name="adaln_mods",
    )(cond, w_ada, b_ada.reshape(depth, 1, n))


def _wspec(w, block, index_fn, **kw):
    _, layer = w
    return pl.BlockSpec((None,) + tuple(block), lambda *idx: (layer,) + tuple(index_fn(*idx)), **kw)


def _mod_spec(which, row_fn):
    return pl.BlockSpec((None, None, 1, D_MODEL),
                        lambda i, *_: (row_fn(i), which, 0, 0))


def _interleave(pieces, chunks):
    if not pieces:
        for chunk in chunks:
            chunk()
        return
    slots = max(len(pieces) - 1, 1)
    per = -(-len(chunks) // slots)
    for n, piece in enumerate(pieces):
        piece()
        for chunk in chunks[n * per:(n + 1) * per]:
            chunk()


def _order_before_reads(ref, value):
    tile = (slice(0, BF16_SUBLANES), slice(0, LANES))
    bits = pltpu.bitcast(value[tile], jnp.uint32)
    zero = pltpu.bitcast((bits >> 16) >> 16, F32)
    ref[tile] = (ref[tile].astype(F32) + zero).astype(ref.dtype)


def _two_stage(i, n_tiles, run):
    @pl.when(i == 0)
    def _():
        run(0, None)

    for parity in range(2):
        @pl.when((i >= 1) & (i < n_tiles) & (i % 2 == parity))
        def _():
            run(parity, 1 - parity)

    @pl.when(i == n_tiles)
    def _():
        run(None, (n_tiles - 1) % 2)


def _inproj_kernel(x_ref, g_ref, sh_ref, sc_ref, w_ref, o_ref, h0_ref, h1_ref, *, n_tiles):
    h_refs = (h0_ref, h1_ref)
    tm = x_ref.shape[0]

    def run(norm_to, dot_from):
        pieces, chunks = [], []
        if dot_from is not None:
            for c0 in range(0, w_ref.shape[1], PROJ_COLS):
                def piece(cols=slice(c0, c0 + PROJ_COLS)):
                    o_ref[:, cols] = _dot(h_refs[dot_from][...], w_ref[:, cols]).astype(o_ref.dtype)
                pieces.append(piece)
        if norm_to is not None:
            for r0 in range(0, tm, NORM_ROWS):
                def chunk(rows=slice(r0, r0 + NORM_ROWS)):
                    val = _norm_mod(x_ref[rows, :], g_ref[...], sh_ref[...], sc_ref[...])
                    h_refs[norm_to][rows, :] = val.astype(BF16)
                    if dot_from is not None:
                        _order_before_reads(h_refs[dot_from], val)
                chunks.append(chunk)
        _interleave(pieces, chunks)

    _two_stage(pl.program_id(0), n_tiles, run)


def _inproj(x, g, mods, row_fn, w, tm):
    t, d = x.shape
    n = w[0].shape[2]
    n_tiles = t // tm
    cur = lambda i: jnp.minimum(i, n_tiles - 1)
    prev = lambda i: jnp.maximum(i - 1, 0)
    return pl.pallas_call(
        functools.partial(_inproj_kernel, n_tiles=n_tiles),
        grid=(n_tiles + 1,),
        in_specs=[
            pl.BlockSpec((tm, d), lambda i: (cur(i), 0)),
            pl.BlockSpec((1, d), lambda i: (0, 0)),
            _mod_spec(0, lambda i: row_fn(cur(i))),
            _mod_spec(1, lambda i: row_fn(cur(i))),
            _wspec(w, (d, n), lambda i: (0, 0), pipeline_mode=pl.Buffered(1)),
        ],
        out_specs=pl.BlockSpec((tm, n), lambda i: (prev(i), 0)),
        out_shape=jax.ShapeDtypeStruct((t, n), BF16),
        scratch_shapes=[pltpu.VMEM((tm, d), BF16)] * 2,
        compiler_params=_params("arbitrary"),
        name="norm_inproj",
    )(x, g.reshape(1, d), mods, mods, w[0])


def _softmax_pv(parts):
    m = functools.reduce(jnp.maximum, [s.max(axis=-1, keepdims=True) for s, _ in parts])
    num = 0.0
    den = 0.0
    for s, v in parts:
        p = jnp.exp(s - m)
        den = den + p.sum(axis=-1, keepdims=True)
        num = num + _dot(p.astype(BF16), v)
    return num / den


def _head_masks():
    lane = lax.broadcasted_iota(jnp.int32, (1, LANES), 1)
    first = lane < HEAD_DIM
    return first, [first, jnp.logical_not(first)]


def _attn_kernel(q_ref, k_ref, v_ref, kc_ref, vc_ref, slab_ref, o_ref, tab_ref, *, n_tiles, rows):
    first, sels = _head_masks()
    kc = kc_ref[...]
    vc = vc_ref[...]

    @pl.when(pl.program_id(1) == 0)
    def _():
        pattern = _attn_slab_pattern(rows)
        for p in range(pattern.shape[0]):
            for hh in range(HEAD_PAIR):
                for qr in range(Q_ROWS):
                    for kr in range(K_ROWS):
                        tab_ref[p, hh, qr * GRID_W:(qr + 1) * GRID_W, kr * GRID_W:(kr + 1) * GRID_W] = (
                            slab_ref[hh, int(pattern[p, qr, kr])])

    def scores(t, hh):
        q0 = pl.multiple_of(t * TQ, TQ)
        k0 = pl.multiple_of(jnp.clip(t * Q_ROWS - WIN_ROWS // 2, 0, rows - K_ROWS) * GRID_W, GRID_W)
        pat = jnp.where(t == 0, 0, jnp.where(t == n_tiles - 1, 2, 1))
        q = q_ref[pl.ds(q0, TQ), :] * ATTN_SCALE
        qm = jnp.where(sels[hh], q, jnp.zeros_like(q))
        s_loc = _dot_nt(qm, k_ref[pl.ds(k0, NK), :]) + tab_ref[pat, hh]
        s_ctx = _dot_nt(qm, kc)
        return k0, s_loc, s_ctx

    def tile_group(tg, carry):
        units = [(tg * ATTN_UNROLL + ti, hh) for ti in range(ATTN_UNROLL) for hh in range(HEAD_PAIR)]
        pending = None
        outs = {}
        for unit in units + [None]:
            nxt = scores(*unit) if unit is not None else None
            if pending is not None:
                (t, hh), (k0, s_loc, s_ctx) = pending
                outs[hh] = _softmax_pv([(s_loc, v_ref[pl.ds(k0, NK), :]), (s_ctx, vc)])
                if hh == HEAD_PAIR - 1:
                    q0 = pl.multiple_of(t * TQ, TQ)
                    o_ref[pl.ds(q0, TQ), :] = jnp.where(first, outs[0], outs[1]).astype(o_ref.dtype)
            pending = (unit, nxt)
        return carry

    lax.fori_loop(0, n_tiles // ATTN_UNROLL, tile_group, 0)


def _attn(proj, cproj, slabs, kc_blk, vc_blk):
    b, n, _ = proj.shape
    rows = n // GRID_W
    n_tiles = rows // Q_ROWS
    lat = lambda off: pl.BlockSpec((None, n, LANES), lambda p, i: (i, 0, off + p))
    cx = lambda off: pl.BlockSpec((None, CTX_LEN, LANES), lambda p, i: (i, 0, off + p))
    return pl.pallas_call(
        functools.partial(_attn_kernel, n_tiles=n_tiles, rows=rows),
        grid=(N_PAIRS, b),
        in_specs=[
            lat(0), lat(N_PAIRS), lat(2 * N_PAIRS), cx(kc_blk), cx(vc_blk),
            pl.BlockSpec((HEAD_PAIR,) + slabs.shape[1:], lambda p, i: (p, 0, 0, 0)),
        ],
        out_specs=pl.BlockSpec((None, n, LANES), lambda p, i: (i, 0, p)),
        out_shape=jax.ShapeDtypeStruct((b, n, NA_WIDTH), BF16),
        scratch_shapes=[pltpu.VMEM((3, HEAD_PAIR, TQ, NK), F32)],
        compiler_params=_params("arbitrary", "arbitrary"),
        name="nbr_attn",
    )(proj, proj, proj, cproj, cproj, slabs)


def _ctx_attn_kernel(q_ref, k_ref, v_ref, o_ref):
    first, sels = _head_masks()
    q = q_ref[...] * ATTN_SCALE
    k = k_ref[...]
    v = v_ref[...]
    outs = []
    for hh in range(HEAD_PAIR):
        qm = jnp.where(sels[hh], q, jnp.zeros_like(q))
        outs.append(_softmax_pv([(_dot_nt(qm, k), v)]))
    o_ref[...] = jnp.where(first, outs[0], outs[1]).astype(o_ref.dtype)


def _ctx_attn(cproj):
    b, n, _ = cproj.shape
    spec = lambda off: pl.BlockSpec((None, n, LANES), lambda i, p: (i, 0, off + p))
    return pl.pallas_call(
        _ctx_attn_kernel,
        grid=(b, N_PAIRS),
        in_specs=[spec(0), spec(N_PAIRS), spec(2 * N_PAIRS)],
        out_specs=spec(0),
        out_shape=jax.ShapeDtypeStruct((b, n, NA_WIDTH), BF16),
        compiler_params=_params("parallel", "parallel"),
        name="ctx_attn",
    )(cproj, cproj, cproj)


MASKED_SLAB = 2 * WIN_ROWS - 1


def _attn_bias_slabs(rpb):
    qc = np.arange(GRID_W)[:, None]
    kc = np.arange(GRID_W)[None, :]
    cs = np.clip(qc - WIN_COLS // 2, 0, GRID_W - WIN_COLS)
    col_ok = (kc >= cs) & (kc < cs + WIN_COLS)
    dc = kc - qc + WIN_COLS - 1
    onehot = np.zeros((2 * WIN_COLS - 1, GRID_W, GRID_W), np.float32)
    qi, ki = np.nonzero(col_ok)
    onehot[dc[qi, ki], qi, ki] = 1.0
    t1 = jnp.einsum('hrd,dqk->hrqk', rpb, jnp.asarray(onehot), precision=lax.Precision.HIGHEST)
    t1 = jnp.where(jnp.asarray(col_ok)[None, None], t1, NEG)
    neg_slab = jnp.full((NA_HEADS, 1, GRID_W, GRID_W), NEG, F32)
    return jnp.concatenate([t1, neg_slab], axis=1)


def _attn_slab_pattern(rows):
    n_tiles = rows // Q_ROWS
    idx = np.full((3, Q_ROWS, K_ROWS), MASKED_SLAB, np.int32)
    for p, t in enumerate((0, 1, n_tiles - 1)):
        k_start = int(np.clip(t * Q_ROWS - WIN_ROWS // 2, 0, rows - K_ROWS))
        for qr in range(Q_ROWS):
            r = t * Q_ROWS + qr
            rs = int(np.clip(r - WIN_ROWS // 2, 0, rows - WIN_ROWS))
            for kr in range(K_ROWS):
                krow = k_start + kr
                if rs <= krow < rs + WIN_ROWS:
                    idx[p, qr, kr] = krow - r + WIN_ROWS - 1
    return idx


def _split_bf16(a):
    hi = a.astype(BF16)
    lo = (a - hi.astype(F32)).astype(BF16)
    return hi, lo


def _dot3(a, b):
    ah, al = _split_bf16(a)
    bh, bl = _split_bf16(b)
    return _dot(ah, bh) + (_dot(ah, bl) + _dot(al, bh))


def _four_w_kernel(cc_ref, sc_ref, w_ref, o_ref):
    w = w_ref[...]
    o_ref[:, :FOURIER_GROUP_DIM] = _dot3(cc_ref[...], w).astype(BF16)
    o_ref[:, FOURIER_GROUP_DIM:] = _dot3(sc_ref[...], w).astype(BF16)


def _four_w(cc, sc, w_four):
    depth, g, c, _ = w_four.shape
    tab = pl.BlockSpec((c, c), lambda l, i: (0, 0))
    return pl.pallas_call(
        _four_w_kernel,
        grid=(depth, g),
        in_specs=[tab, tab, pl.BlockSpec((None, None, c, c), lambda l, i: (l, i, 0, 0))],
        out_specs=pl.BlockSpec((None, None, c, 2 * c), lambda l, i: (l, i, 0, 0)),
        out_shape=jax.ShapeDtypeStruct((depth, g, c, 2 * c), BF16),
        compiler_params=_params("parallel", "parallel"),
        name="fourier_weights",
    )(cc, sc, w_four)


def _four_chan_kernel(f_ref, m_ref, *refs):
    out_refs, (yc_scr, ys_scr) = refs[:2 * DFT_RADIX], refs[2 * DFT_RADIX:]
    c = FOURIER_GROUP_DIM
    part_rows = f_ref.shape[0] // DFT_RADIX
    for g in range(FOURIER_GROUPS):
        cols = slice(g * c, (g + 1) * c)
        y = _dot(f_ref[:, cols], m_ref[g])
        yc_scr[...] = y[:, :c]
        ys_scr[...] = y[:, c:]
        for p in range(DFT_RADIX):
            out_refs[2 * p][:, cols] = yc_scr[pl.ds(p, part_rows, stride=DFT_RADIX), :].astype(BF16)
            out_refs[2 * p + 1][:, cols] = ys_scr[pl.ds(p, part_rows, stride=DFT_RADIX), :].astype(BF16)


def _four_chan(proj, mw, tm):
    t = proj.shape[0]
    f_blk = proj.shape[1] // FOURIER_WIDTH - 1
    out = pl.BlockSpec((tm // DFT_RADIX, FOURIER_WIDTH), lambda i: (i, 0))
    return pl.pallas_call(
        _four_chan_kernel,
        grid=(t // tm,),
        in_specs=[
            pl.BlockSpec((tm, FOURIER_WIDTH), lambda i: (i, f_blk)),
            pl.BlockSpec(mw.shape, lambda i: (0, 0, 0)),
        ],
        out_specs=[out] * (2 * DFT_RADIX),
        out_shape=[jax.ShapeDtypeStruct((t // DFT_RADIX, FOURIER_WIDTH), BF16)] * (2 * DFT_RADIX),
        scratch_shapes=[pltpu.VMEM((tm, FOURIER_GROUP_DIM), F32)] * 2,
        compiler_params=_params("parallel"),
        name="fourier_chan",
    )(proj, mw)


def _four_pos_kernel(c0, s0, c1, s1, c2, s2, c3, s3, ns1, nc3, a0, b0, a1, b1, a2, b2, a3, b3, o_ref):
    step = min(DFT_ROWS, c0.shape[0])
    for r0 in range(0, c0.shape[0], step):
        rows = slice(r0, r0 + step)
        dot = lambda table, data: _dot(table[rows, :], data[...])
        re0 = dot(c0, a0) + dot(s0, b0)
        re2 = dot(c2, a2) + dot(s2, b2)
        re13 = dot(c1, a1) + dot(s1, b1) + dot(c3, a3) + dot(s3, b3)
        im13 = dot(c1, b1) + dot(ns1, a1) + dot(nc3, b3) + dot(s3, a3)
        plus, minus = re0 + re2, re0 - re2
        o_ref[0, rows, :] = (plus + re13).astype(o_ref.dtype)
        o_ref[1, rows, :] = (minus + im13).astype(o_ref.dtype)
        o_ref[2, rows, :] = (plus - re13).astype(o_ref.dtype)
        o_ref[3, rows, :] = (minus - im13).astype(o_ref.dtype)


def _four_pos(tables, y_parts, b, tm, tn):
    data = [y.reshape(b, y.shape[0] // b, y.shape[1]) for y in y_parts]
    _, part, w = data[0].shape
    mat = pl.BlockSpec((tm, part), lambda i, bb, j: (i, 0), pipeline_mode=pl.Buffered(1))
    dat = pl.BlockSpec((None, part, tn), lambda i, bb, j: (bb, 0, j))
    out = pl.pallas_call(
        _four_pos_kernel,
        grid=(part // tm, b, w // tn),
        in_specs=[mat] * len(tables) + [dat] * len(data),
        out_specs=pl.BlockSpec((None, DFT_RADIX, tm, tn), lambda i, bb, j: (bb, 0, i, j)),
        out_shape=jax.ShapeDtypeStruct((b, DFT_RADIX, part, w), BF16),
        compiler_params=_params("parallel", "parallel", "parallel"),
        name="fourier_pos",
    )(*tables, *data)
    return out.reshape(b, DFT_RADIX * part, w)


def _dft_tables(n):
    part = n // DFT_RADIX
    split = 2 ** (part.bit_length() // 2)
    j = jnp.arange(part, dtype=jnp.int32)[:, None]
    hi = jnp.arange(part // split, dtype=jnp.int32)[None, :] * split
    lo = jnp.arange(split, dtype=jnp.int32)[None, :]
    ang = lambda k: (2.0 * np.pi / part) * ((j * k) % part).astype(F32)
    ch, sh, cl, sl = jnp.cos(ang(hi)), jnp.sin(ang(hi)), jnp.cos(ang(lo)), jnp.sin(ang(lo))
    c0 = (ch[:, :, None] * cl[:, None, :] - sh[:, :, None] * sl[:, None, :]).reshape(part, part)
    s0 = (sh[:, :, None] * cl[:, None, :] + ch[:, :, None] * sl[:, None, :]).reshape(part, part)
    tables = [c0, s0]
    for p in range(1, DFT_RADIX):
        phi = (2.0 * np.pi * p / n) * j.astype(F32)
        cp, sp = jnp.cos(phi), jnp.sin(phi)
        tables += [c0 * cp - s0 * sp, s0 * cp + c0 * sp]
    tables += [-tables[3], -tables[6]]
    return tuple(t.astype(BF16) for t in tables)


def _outproj_kernel(a_ref, f_ref, w_ref, x_ref, gpost_ref, gate_ref, gpre_ref, sh_ref, sc_ref,
                    xo_ref, h_ref, m0_ref, m1_ref, lhs_ref, *, n_tiles):
    mix_refs = (m0_ref, m1_ref)
    tm = x_ref.shape[0]

    def run(mix_to, epi_from):
        pieces, chunks = [], []
        if mix_to is not None:
            lhs_ref[:, :NA_WIDTH] = a_ref[...]
            lhs_ref[:, NA_WIDTH:] = f_ref[...]
            for c0 in range(0, w_ref.shape[1], OUT_COLS):
                def piece(cols=slice(c0, c0 + OUT_COLS)):
                    mix_refs[mix_to][:, cols] = _dot(lhs_ref[...], w_ref[:, cols])
                pieces.append(piece)
        if epi_from is not None:
            for r0 in range(0, tm, OUT_ROWS):
                def chunk(rows=slice(r0, r0 + OUT_ROWS)):
                    x = x_ref[rows, :] + gate_ref[...] * _rms(mix_refs[epi_from][rows, :], gpost_ref[...])
                    xo_ref[rows, :] = x
                    h = _norm_mod(x, gpre_ref[...], sh_ref[...], sc_ref[...])
                    h_ref[rows, :] = h.astype(BF16)
                    if mix_to is not None:
                        _order_before_reads(lhs_ref, h)
                chunks.append(chunk)
        _interleave(pieces, chunks)

    _two_stage(pl.program_id(0), n_tiles, run)


def _outproj(attn, four, w, x, g_post, g_pre, mods, row_fn, tm):
    t, d = x.shape
    n_tiles = t // tm
    cur = lambda i: jnp.minimum(i, n_tiles - 1)
    prev = lambda i: jnp.maximum(i - 1, 0)
    vec = pl.BlockSpec((1, d), lambda i: (0, 0))
    row = pl.BlockSpec((tm, d), lambda i: (prev(i), 0))
    half = pl.BlockSpec((tm, NA_WIDTH), lambda i: (cur(i), 0))
    mod = lambda which: _mod_spec(which, lambda i: row_fn(prev(i)))
    return pl.pallas_call(
        functools.partial(_outproj_kernel, n_tiles=n_tiles),
        grid=(n_tiles + 1,),
        in_specs=[half, half, _wspec(w, w[0].shape[1:], lambda i: (0, 0), pipeline_mode=pl.Buffered(1)), row, vec,
                  mod(2), vec, mod(3), mod(4)],
        out_specs=[row, row],
        out_shape=[jax.ShapeDtypeStruct((t, d), F32), jax.ShapeDtypeStruct((t, d), BF16)],
        scratch_shapes=[pltpu.VMEM((tm, d), F32), pltpu.VMEM((tm, d), F32),
                        pltpu.VMEM((tm, NA_WIDTH + FOURIER_WIDTH), BF16)],
        compiler_params=_params("arbitrary"),
        name="outproj_residual",
    )(attn, four, w[0], x, g_post.reshape(1, d), mods, g_pre.reshape(1, d), mods, mods)


def _matmul_kernel(x_ref, w_ref, o_ref):
    o_ref[...] = _dot(x_ref[...], w_ref[...]).astype(o_ref.dtype)


def _matmul(x, w, tn, out_dtype, name):
    m, k = x.shape
    n = w[0].shape[2]
    return pl.pallas_call(
        _matmul_kernel,
        grid=(n // tn,),
        in_specs=[pl.BlockSpec((m, k), lambda j: (0, 0)), _wspec(w, (k, tn), lambda j: (0, j))],
        out_specs=pl.BlockSpec((m, tn), lambda j: (0, j)),
        out_shape=jax.ShapeDtypeStruct((m, n), out_dtype),
        compiler_params=_params("parallel"),
        name=name,
    )(x, w[0])


def _ffn_kernel(h_ref, wa_ref, wg_ref, edge_ref, cp_ref, wd_ref, x_ref, gate_ref, gpost_ref, o_ref,
                acc_ref, ua0, ug0, ua1, ug1, act0, act1,
                *, tm, n_ff_tiles, n_row_tiles, seq_len):
    i = pl.program_id(0)
    j = pl.program_id(1)
    nj = n_ff_tiles
    pad = F32_SUBLANES
    tf = wa_ref.shape[1]
    u_refs = ((ua0, ug0), (ua1, ug1))
    act_refs = (act0, act1)

    def up_proj_part(half, k0):
        w_ref = (wa_ref, wg_ref)[half]
        return _dot(h_ref[:, k0:k0 + UP_K], w_ref[k0:k0 + UP_K, :])

    def up_proj_store(parity, half, u):
        u_ref = u_refs[parity][half]
        u_ref[0:pad, :] = jnp.broadcast_to(edge_ref[half:half + 1, :], (pad, tf))
        u_ref[pad:pad + tm, :] = u
        u_ref[pad + tm:, :] = jnp.broadcast_to(edge_ref[2 + half:3 + half, :], (pad, tf))

    def conv(u_ref, r0, cols, p0):
        win = u_ref[r0:r0 + CONV_ROWS + 2 * pad, cols]
        mid = win[pad:pad + CONV_ROWS]
        down = pltpu.roll(win, 1, axis=0)[pad:pad + CONV_ROWS]
        up = pltpu.roll(win, CONV_ROWS + 2 * pad - 1, axis=0)[pad:pad + CONV_ROWS]
        row = lax.broadcasted_iota(jnp.int32, (CONV_ROWS, 1), 0)
        if r0 > 0 and r0 % seq_len == 0:
            down = jnp.where(row == 0, 0.0, down)
        if r0 + CONV_ROWS < tm and (r0 + CONV_ROWS) % seq_len == 0:
            up = jnp.where(row == CONV_ROWS - 1, 0.0, up)
        return (down * cp_ref[p0:p0 + 1, cols] + mid * cp_ref[p0 + 1:p0 + 2, cols]
                + up * cp_ref[p0 + 2:p0 + 3, cols] + cp_ref[p0 + 3:p0 + 4, cols])

    def conv_gate(parity, r0, c0):
        ua_ref, ug_ref = u_refs[parity]
        cols = slice(c0, c0 + LANES)
        a = conv(ua_ref, r0, cols, 0)
        g = conv(ug_ref, r0, cols, CONV_TAPS + 1)
        act_refs[parity][r0:r0 + CONV_ROWS, cols] = (g * jax.nn.sigmoid(g) * a).astype(BF16)

    def down_proj(parity, cols):
        return _dot(act_refs[parity][...], wd_ref[:, cols])

    chunks = [(r0, c0) for r0 in range(0, tm, CONV_ROWS) for c0 in range(0, tf, LANES)]
    d_out = wd_ref.shape[1]
    down_groups = [slice(c, c + DOWN_COLS) for c in range(0, d_out, DOWN_COLS)]

    def finish_rows(r0):
        rows = slice(r0, r0 + FINISH_ROWS)
        o_ref[rows, :] = x_ref[rows, :] + gate_ref[...] * _rms(acc_ref[rows, :], gpost_ref[...])
        acc_ref[rows, :] = jnp.zeros((FINISH_ROWS, d_out), F32)

    def run_step(up=None, conv_of=None, down=None, finish=False):
        pieces = []
        if down is not None:
            for cols in down_groups:
                def piece(cols=cols):
                    acc_ref[:, cols] += down_proj(down, cols)
                pieces.append((piece, tf * DOWN_COLS))
        n_down = len(pieces)
        if up is not None:
            d_in = h_ref.shape[1]
            for half in range(2):
                partial_u = []
                for k0 in range(0, d_in, UP_K):
                    def piece(half=half, k0=k0, partial_u=partial_u):
                        part = up_proj_part(half, k0)
                        partial_u[:] = [part if not partial_u else partial_u[0] + part]
                        if k0 + UP_K == d_in:
                            up_proj_store(up, half, partial_u[0])
                    pieces.append((piece, UP_K * tf))
        conv_todo = [functools.partial(conv_gate, conv_of, *c) for c in chunks] if conv_of is not None else []
        fin_todo = [functools.partial(finish_rows, r0) for r0 in range(0, tm, FINISH_ROWS)] if finish else []
        n_conv, n_fin = len(conv_todo), len(fin_todo)
        total = sum(w for _, w in pieces)
        total_up = sum(w for _, w in pieces[n_down:])
        done = done_up = 0.0
        for n, (piece, w) in enumerate(pieces):
            piece()
            done += w
            is_last = n == len(pieces) - 1
            conv_upto = n_conv if is_last else round(n_conv * done / total)
            while n_conv - len(conv_todo) < conv_upto:
                conv_todo.pop(0)()
            if n >= n_down:
                done_up += w
                fin_upto = n_fin if is_last else round(n_fin * done_up / total_up)
                while n_fin - len(fin_todo) < fin_upto:
                    fin_todo.pop(0)()
        for chunk in conv_todo + fin_todo:
            chunk()

    s = i * nj + j
    last = n_row_tiles * nj
    finishing = (j == 1) & (i >= 1)

    @pl.when(s == 0)
    def _():
        acc_ref[...] = jnp.zeros_like(acc_ref)
        run_step(up=0)

    @pl.when(s == 1)
    def _():
        run_step(up=1, conv_of=0)

    for parity in range(2):
        steady = (s >= 2) & (s < last) & (s % 2 == parity)

        @pl.when(steady & jnp.logical_not(finishing))
        def _():
            run_step(up=parity, conv_of=1 - parity, down=parity)

        @pl.when(steady & finishing)
        def _():
            run_step(up=parity, conv_of=1 - parity, down=parity, finish=True)

    @pl.when(s == last)
    def _():
        run_step(conv_of=(last - 1) % 2, down=last % 2)

    @pl.when(s == last + 1)
    def _():
        run_step(down=(last + 1) % 2, finish=True)


def _ffn(h, w_up, conv_w, conv_b, w_down, x, g_post, mods, row_fn, seq_len, tm):
    t, d = x.shape
    ff = w_down[0].shape[1]
    tf = FF_TILE
    nj = ff // tf
    n_tiles = t // tm
    assert seq_len % tm == 0 or (tm % seq_len == 0 and seq_len % CONV_ROWS == 0)

    tile0 = np.arange(n_tiles) * tm
    start = tile0 % seq_len == 0
    end = (tile0 + tm) % seq_len == 0
    prev = jnp.take(h, jnp.asarray(np.maximum(tile0 - 1, 0)), axis=0)
    nxt = jnp.take(h, jnp.asarray(np.minimum(tile0 + tm, t - 1)), axis=0)
    prev = jnp.where(jnp.asarray(start)[:, None], jnp.zeros_like(prev), prev)
    nxt = jnp.where(jnp.asarray(end)[:, None], jnp.zeros_like(nxt), nxt)
    edge_rows = jnp.concatenate([prev, nxt], axis=0)
    pad = -edge_rows.shape[0] % BF16_SUBLANES
    edge_rows = jnp.pad(edge_rows, ((0, pad), (0, 0)))
    u_edge = _matmul(edge_rows, w_up, 2 * ff // EDGE_STEPS, F32, "conv_ffn_edges")
    u_prev, u_next = u_edge[:n_tiles], u_edge[n_tiles:2 * n_tiles]
    edges = jnp.stack([u_prev[:, :ff], u_prev[:, ff:], u_next[:, :ff], u_next[:, ff:]], axis=1)
    edges = edges.reshape(n_tiles, 4, nj, tf).transpose(0, 2, 1, 3)
    conv_p = jnp.concatenate([conv_w[:, :ff], conv_b[None, :ff], conv_w[:, ff:], conv_b[None, ff:]], axis=0)
    conv_p = conv_p.reshape(2 * (CONV_TAPS + 1), nj, tf).transpose(1, 0, 2)

    assert nj >= 3
    vec = pl.BlockSpec((1, d), lambda i, j: (0, 0))
    cur = lambda i: jnp.minimum(i, n_tiles - 1)
    fin = lambda i, j, switch: jnp.where(j <= switch, jnp.maximum(i - 1, 0), cur(i))
    lag = lambda j, k: jnp.where(j >= k, j - k, j + nj - k)
    up_cols = lambda off: _wspec(w_up, (d, tf), lambda i, j: (0, off + j))
    u_buf = pltpu.VMEM((tm + 2 * F32_SUBLANES, tf), F32)
    act_buf = pltpu.VMEM((tm, tf), BF16)
    return pl.pallas_call(
        functools.partial(_ffn_kernel, tm=tm, n_ff_tiles=nj, n_row_tiles=n_tiles, seq_len=seq_len),
        grid=(n_tiles + 1, nj),
        in_specs=[
            pl.BlockSpec((tm, d), lambda i, j: (cur(i), 0)),
            up_cols(0), up_cols(nj),
            pl.BlockSpec((None, None, 4, tf), lambda i, j: (cur(i), j, 0, 0)),
            pl.BlockSpec((None, 2 * (CONV_TAPS + 1), tf), lambda i, j: (lag(j, 1), 0, 0)),
            _wspec(w_down, (tf, d), lambda i, j: (lag(j, 2), 0)),
            pl.BlockSpec((tm, d), lambda i, j: (fin(i, j, 5), 0)),
            pl.BlockSpec((None, None, 1, d), lambda i, j: (row_fn(fin(i, j, 1)), 5, 0, 0)),
            vec,
        ],
        out_specs=pl.BlockSpec((tm, d), lambda i, j: (fin(i, j, 8), 0)),
        out_shape=jax.ShapeDtypeStruct((t, d), F32),
        scratch_shapes=[pltpu.VMEM((tm, d), F32), u_buf, u_buf, u_buf, u_buf, act_buf, act_buf],
        compiler_params=_params("arbitrary", "arbitrary"),
        name="conv_ffn",
    )(h, w_up[0], w_up[0], edges, conv_p, w_down[0], x, mods,
      g_post.reshape(1, d))


def kernel(x, c, ctx, c_ctx, w_ada, b_ada, g_pre_mix, w_in, rpb, w_four, w_out, g_post_mix,
           g_pre_ffn, w_up, conv_w, conv_b, w_down, g_post_ffn):
    b, n, d = x.shape
    n_ctx = ctx.shape[1]
    assert (d, n, n_ctx, b) == (D_MODEL, GRID_W * GRID_W, CTX_LEN, CTX_MOD_ROW)
    t_lat, t_ctx = b * n, b * n_ctx
    tm_lat = tm_ctx = ROW_TILE
    tiles_per_seq = n // tm_lat
    assert tiles_per_seq & (tiles_per_seq - 1) == 0
    lat_row = lambda i: i >> (tiles_per_seq.bit_length() - 1)
    ctx_row = lambda i: CTX_MOD_ROW

    cond = jnp.zeros((MOD_ROWS, d), F32).at[:b].set(c).at[CTX_MOD_ROW].set(c_ctx)
    mods_all = _mods(cond, w_ada, b_ada)

    scale = (n * FOURIER_GROUP_DIM) ** -0.5
    scale_ctx = (n_ctx * FOURIER_GROUP_DIM) ** -0.5
    kk = np.outer(np.arange(FOURIER_GROUP_DIM), np.arange(FOURIER_GROUP_DIM)) % FOURIER_GROUP_DIM
    ang = 2.0 * np.pi * kk / FOURIER_GROUP_DIM
    cc, sc = np.cos(ang), -np.sin(ang)
    mw_lat = _four_w(jnp.asarray(cc * scale, F32), jnp.asarray(sc * scale, F32), w_four)
    dft_lat = _dft_tables(n)

    w_in_b, w_out_b, w_up_b, w_down_b = (w.astype(BF16) for w in (w_in, w_out, w_up, w_down))
    xt = x.reshape(t_lat, d)
    ct = ctx.reshape(t_ctx, d)
    for l in range(DEPTH):
        last = l == DEPTH - 1
        mods = mods_all[l].reshape(MOD_ROWS, N_MOD, 1, d)
        w_in_l, w_out_l, w_up_l, w_down_l = ((w, l) for w in (w_in_b, w_out_b, w_up_b, w_down_b))

        proj = _inproj(xt, g_pre_mix[l], mods, lat_row, w_in_l, tm_lat)
        if last:
            w_kv = (w_in_b[l:l + 1, :, NA_WIDTH:3 * NA_WIDTH], 0)
            cproj = _inproj(ct, g_pre_mix[l], mods, ctx_row, w_kv, tm_ctx)
            kc_blk, vc_blk = 0, N_PAIRS
        else:
            cproj = _inproj(ct, g_pre_mix[l], mods, ctx_row, w_in_l, tm_ctx)
            kc_blk, vc_blk = N_PAIRS, 2 * N_PAIRS
        cproj3 = cproj.reshape(b, n_ctx, cproj.shape[1])

        attn = _attn(proj.reshape(b, n, PROJ_WIDTH), cproj3, _attn_bias_slabs(rpb[l]), kc_blk, vc_blk)
        four = _four_pos(dft_lat, _four_chan(proj, mw_lat[l], CHAN_TILE), b, n // DFT_RADIX, DFT_COLS)
        xt, h2 = _outproj(attn.reshape(t_lat, NA_WIDTH), four.reshape(t_lat, FOURIER_WIDTH), w_out_l, xt,
                          g_post_mix[l], g_pre_ffn[l], mods, lat_row, tm_lat)
        xt = _ffn(h2, w_up_l, conv_w[l], conv_b[l], w_down_l, xt, g_post_ffn[l], mods, lat_row, n, tm_lat)

        if not last:
            mw_ctx = _four_w(jnp.asarray(cc * scale_ctx, F32), jnp.asarray(sc * scale_ctx, F32),
                             w_four[l:l + 1])[0]
            dft_ctx = _dft_tables(n_ctx)
            attn_c = _ctx_attn(cproj3)
            four_c = _four_pos(dft_ctx, _four_chan(cproj, mw_ctx, ROW_TILE), b, n_ctx // DFT_RADIX, DFT_COLS)
            ct, hc2 = _outproj(attn_c.reshape(t_ctx, NA_WIDTH), four_c.reshape(t_ctx, FOURIER_WIDTH),
                               w_out_l, ct, g_post_mix[l], g_pre_ffn[l], mods, ctx_row, tm_ctx)
            ct = _ffn(hc2, w_up_l, conv_w[l], conv_b[l], w_down_l, ct, g_post_ffn[l], mods, ctx_row,
                      n_ctx, tm_ctx)
    return xt.reshape(b, n, d)
```
